```python
import math
import jax, jax.numpy as jnp
from jax import lax
import numpy as np

D_MODEL = 1024
BATCH = 2
SEQ = 16384
DEPTH = 2

HEAD_DIM = 64
GDN_HEADS = 6
GDN_CONV = 4
GDN_CHUNK = 64
GDN_WIDTH = GDN_HEADS * HEAD_DIM
NSA_HEADS = 6
NSA_KV_HEADS = 2
NSA_GROUP = NSA_HEADS // NSA_KV_HEADS
NSA_WIDTH = NSA_HEADS * HEAD_DIM
NSA_KV_WIDTH = NSA_KV_HEADS * HEAD_DIM
CMP_LEN = 32
CMP_STRIDE = 16
CMP_HIDDEN = 256
SEL_BLOCK = 64
SEL_TOPK = 16
WINDOW = 512
Q_BLOCK = 128
FORCED_SCORE = 1e4
POOL_WINDOWS = (2, 4, 8, 16)
POOL_GROUP_DIM = 64
POOL_WIDTH = 4 * POOL_GROUP_DIM
D_MIX = GDN_WIDTH + NSA_WIDTH + POOL_WIDTH
IN_SPLITS = (3 * GDN_WIDTH, GDN_WIDTH, GDN_HEADS, GDN_HEADS, NSA_WIDTH, 6 * NSA_KV_WIDTH, 3 * NSA_HEADS, POOL_WIDTH)
D_IN = 3 * GDN_WIDTH + GDN_WIDTH + GDN_HEADS + GDN_HEADS + NSA_WIDTH + 6 * NSA_KV_WIDTH + 3 * NSA_HEADS + POOL_WIDTH
D_FF = 4 * D_MODEL
EPS = 1e-6

kernel_name = "hymba_gdn_nsa_pool_hybrid"


def rms_norm(x, g):
    xf = x.astype(jnp.float32)
    return (xf * lax.rsqrt(jnp.mean(xf * xf, axis=-1, keepdims=True) + EPS) * g).astype(x.dtype)


def l2norm(x):
    xf = x.astype(jnp.float32)
    return xf * lax.rsqrt(jnp.sum(xf * xf, axis=-1, keepdims=True) + EPS)


def masked_softmax(s, mask):
    s = jnp.where(mask, s.astype(jnp.float32), -1e30)
    return jax.nn.softmax(s, axis=-1) * mask


def split_columns(proj):
    offs = []
    acc = 0
    for s in IN_SPLITS[:-1]:
        acc += s
        offs.append(acc)
    return jnp.split(proj, offs, axis=-1)


def causal_conv(x, w):
    T = x.shape[1]
    K = w.shape[0]
    xp = jnp.pad(x, ((0, 0), (K - 1, 0), (0, 0)))
    return sum(xp[:, k:k + T] * w[k] for k in range(K))


def gated_delta_rule(q, k, v, beta, g):
    B, T, H, Dk = q.shape
    Dv = v.shape[-1]
    C = GDN_CHUNK
    N = T // C
    f32 = jnp.float32

    def chunks(a):
        a = a.astype(f32).reshape((B, N, C, H) + a.shape[3:])
        return jnp.moveaxis(a, 3, 1)

    q, k, v, beta, g = map(chunks, (q, k, v, beta, g))
    q = q * Dk ** -0.5
    gc = jnp.cumsum(g, axis=-1)
    causal = jnp.tril(jnp.ones((C, C), bool))
    strict = jnp.tril(jnp.ones((C, C), bool), -1)
    decay = jnp.exp(jnp.where(causal, gc[..., :, None] - gc[..., None, :], -jnp.inf))
    k_beta = k * beta[..., None]
    L = jnp.where(strict, jnp.einsum('bhncd,bhnsd->bhncs', k_beta, k) * decay, 0.0)
    A = jnp.eye(C, dtype=f32) + L
    u = lax.linalg.triangular_solve(A, v * beta[..., None], left_side=True, lower=True, unit_diagonal=True)
    w = lax.linalg.triangular_solve(A, k_beta * jnp.exp(gc)[..., None], left_side=True, lower=True, unit_diagonal=True)
    qk = jnp.einsum('bhncd,bhnsd->bhncs', q, k) * decay
    q_dec = q * jnp.exp(gc)[..., None]
    k_dec = k * jnp.exp(gc[..., -1:] - gc)[..., None]
    g_last = jnp.exp(gc[..., -1])

    def step(S, xs):
        u_i, w_i, qk_i, qd_i, kd_i, gl_i = xs
        v_new = u_i - jnp.einsum('bhcd,bhde->bhce', w_i, S)
        o = jnp.einsum('bhcd,bhde->bhce', qd_i, S) + jnp.einsum('bhcs,bhse->bhce', qk_i, v_new)
        S = S * gl_i[..., None, None] + jnp.einsum('bhcd,bhce->bhde', kd_i, v_new)
        return S, o

    xs = tuple(jnp.moveaxis(a, 2, 0) for a in (u, w, qk, q_dec, k_dec, g_last))
    S0 = jnp.zeros((B, H, Dk, Dv), f32)
    _, o = lax.scan(step, S0, xs)
    return jnp.transpose(o, (1, 0, 3, 2, 4)).reshape(B, T, H, Dv)


def gdn_mixer(qkv, z, b, a, conv_w, a_log, dt_bias, norm_g):
    B, T, _ = qkv.shape
    qkv = jax.nn.silu(causal_conv(qkv, conv_w))
    q, k, v = jnp.split(qkv, 3, axis=-1)
    heads = lambda t: t.reshape(B, T, GDN_HEADS, HEAD_DIM)
    q, k, v = l2norm(heads(q)), l2norm(heads(k)), heads(v)
    beta = jax.nn.sigmoid(b.astype(jnp.float32))
    g = -jnp.exp(a_log.astype(jnp.float32)) * jax.nn.softplus(a.astype(jnp.float32) + dt_bias)
    o = gated_delta_rule(q, k, v, beta, g)
    o = rms_norm(o, norm_g) * jax.nn.silu(heads(z).astype(jnp.float32))
    return o.reshape(B, T, GDN_WIDTH).astype(qkv.dtype)


def compress(x, pos, w1, w2):
    B, T, H, D = x.shape
    R = CMP_LEN // CMP_STRIDE
    Nr = T // CMP_STRIDE
    r = x.reshape(B, Nr, CMP_STRIDE, H, D)
    blocks = jnp.concatenate([r[:, i:Nr - R + 1 + i] for i in range(R)], axis=2)
    blocks = blocks + pos[None, None, :, None, :]
    Nc = blocks.shape[1]
    flat = blocks.transpose(0, 1, 3, 2, 4).reshape(B, Nc, H, CMP_LEN * D)
    out = jax.nn.silu(flat @ w1) @ w2
    return out.transpose(0, 2, 1, 3)


def cmp_to_sel(p):
    R = SEL_BLOCK // CMP_STRIDE
    P = CMP_LEN // CMP_STRIDE - 1
    Ns = (p.shape[-1] + P) // R
    pp = jnp.pad(p, [(0, 0)] * (p.ndim - 1) + [(P, P)])
    return sum(lax.slice_in_dim(pp, o, o + R * (Ns - 1) + 1, stride=R, axis=-1) for o in range(R + P))


def nsa_mixer(q, kv, gate_logits, q_norm, k_norm, cmp_pos, cmp_w1, cmp_w2):
    B, T, _ = q.shape
    Hkv, G, Dh = NSA_KV_HEADS, NSA_GROUP, HEAD_DIM
    out_dtype = q.dtype
    scale = Dh ** -0.5
    q = rms_norm(q.reshape(B, T, NSA_HEADS, Dh), q_norm)
    q = q.reshape(B, T, Hkv, G, Dh).transpose(0, 2, 3, 1, 4)
    kc, vc, ks, vs, kw, vw = [t.reshape(B, T, Hkv, Dh) for t in jnp.split(kv, 6, axis=-1)]
    kc = rms_norm(compress(kc, cmp_pos[0], cmp_w1[0], cmp_w2[0]), k_norm[0])
    vc = compress(vc, cmp_pos[1], cmp_w1[1], cmp_w2[1])
    Nc = kc.shape[2]
    Ns = T // SEL_BLOCK
    K_SEL = min(SEL_TOPK, Ns)
    ks = rms_norm(ks, k_norm[1]).transpose(0, 2, 1, 3).reshape(B, Hkv, Ns, SEL_BLOCK, Dh)
    vs = vs.transpose(0, 2, 1, 3).reshape(B, Hkv, Ns, SEL_BLOCK, Dh)
    pad = ((0, 0), (0, 0), (WINDOW, 0), (0, 0))
    kw = jnp.pad(rms_norm(kw, k_norm[2]).transpose(0, 2, 1, 3), pad)
    vw = jnp.pad(vw.transpose(0, 2, 1, 3), pad)
    gates = jax.nn.sigmoid(gate_logits.astype(jnp.float32)).reshape(B, T, Hkv, G, 3).transpose(0, 2, 3, 1, 4)
    cmp_end = jnp.arange(Nc) * CMP_STRIDE + CMP_LEN - 1
    blk = jnp.arange(Ns)
    bidx = jnp.arange(B)[:, None, None, None]
    hidx = jnp.arange(Hkv)[None, :, None, None]

    def block(i):
        t0 = i * Q_BLOCK
        pos = t0 + jnp.arange(Q_BLOCK)
        qb = lax.dynamic_slice_in_dim(q, t0, Q_BLOCK, axis=3) * scale
        gb = lax.dynamic_slice_in_dim(gates, t0, Q_BLOCK, axis=3)
        s = jnp.einsum('bhgqd,bhcd->bhgqc', qb, kc)
        p_c = masked_softmax(s, cmp_end[None, :] <= pos[:, None])
        o_c = jnp.einsum('bhgqc,bhcd->bhgqd', p_c, vc)
        imp = cmp_to_sel(p_c.sum(axis=2))
        cur = pos // SEL_BLOCK
        forced = (blk[None] == 0) | (blk[None] == cur[:, None]) | (blk[None] == cur[:, None] - 1)
        valid = blk[None] * SEL_BLOCK <= pos[:, None]
        imp = jnp.where(forced, FORCED_SCORE, jnp.where(valid, imp, -1.0))
        _, idx = lax.top_k(imp, K_SEL)
        ksel = ks[bidx, hidx, idx]
        vsel = vs[bidx, hidx, idx].reshape(B, Hkv, Q_BLOCK, K_SEL * SEL_BLOCK, Dh)
        s = jnp.einsum('bhgqd,bhqksd->bhgqks', qb, ksel).reshape(B, Hkv, G, Q_BLOCK, K_SEL * SEL_BLOCK)
        key_pos = idx[..., None] * SEL_BLOCK + jnp.arange(SEL_BLOCK)
        mask_s = (key_pos <= pos[:, None, None]).reshape(B, Hkv, 1, Q_BLOCK, K_SEL * SEL_BLOCK)
        o_s = jnp.einsum('bhgqn,bhqnd->bhgqd', masked_softmax(s, mask_s), vsel)
        kwb = lax.dynamic_slice_in_dim(kw, t0, Q_BLOCK + WINDOW, axis=2)
        vwb = lax.dynamic_slice_in_dim(vw, t0, Q_BLOCK + WINDOW, axis=2)
        kpos = t0 - WINDOW + jnp.arange(Q_BLOCK + WINDOW)
        mask_w = (kpos[None] <= pos[:, None]) & (kpos[None] > pos[:, None] - WINDOW) & (kpos[None] >= 0)
        s = jnp.einsum('bhgqd,bhkd->bhgqk', qb, kwb)
        o_w = jnp.einsum('bhgqk,bhkd->bhgqd', masked_softmax(s, mask_w), vwb)
        return gb[..., 0:1] * o_c + gb[..., 1:2] * o_s + gb[..., 2:3] * o_w

    out = lax.map(block, jnp.arange(T // Q_BLOCK))
    return out.transpose(1, 0, 4, 2, 3, 5).reshape(B, T, NSA_WIDTH).astype(out_dtype)


def pool_mixer(u, pool_w, pool_scale):
    B, T, _ = u.shape
    uf = u.astype(jnp.float32)
    c = jnp.pad(jnp.cumsum(uf, axis=1), ((0, 0), (1, 0), (0, 0)))
    t1 = jnp.arange(1, T + 1, dtype=jnp.float32)
    outs = []
    for gi, w in enumerate(POOL_WINDOWS):
        sl = slice(gi * POOL_GROUP_DIM, (gi + 1) * POOL_GROUP_DIM)
        cg = c[..., sl]
        cg_lag = jnp.pad(cg, ((0, 0), (w - 1, 0), (0, 0)))[:, :T]
        mean = (cg[:, 1:] - cg_lag) / jnp.minimum(t1, float(w))[None, :, None]
        outs.append(jnp.einsum('btc,cd->btd', mean - uf[..., sl], pool_w[gi]))
    return (jnp.concatenate(outs, axis=-1) * pool_scale).astype(u.dtype)


def setup_inputs(seed: int = 0) -> dict:
    key = jax.random.key(seed)
    keys = jax.random.split(key, 20)
    nrm = lambda k, shape, s: jax.random.normal(k, shape, jnp.float32) * s
    L = DEPTH
    x = nrm(keys[0], (BATCH, SEQ, D_MODEL), 1.0)
    norm_mix = 1.0 + nrm(keys[1], (L, D_MODEL), 0.02)
    w_in = nrm(keys[2], (L, D_MODEL, D_IN), D_MODEL ** -0.5)
    conv_w = nrm(keys[3], (L, GDN_CONV, 3 * GDN_WIDTH), GDN_CONV ** -0.5)
    a_log = jnp.log(jax.random.uniform(keys[4], (L, GDN_HEADS), jnp.float32, 1.0, 16.0))
    dt = jnp.exp(jax.random.uniform(keys[5], (L, GDN_HEADS), jnp.float32, math.log(1e-3), math.log(1e-1)))
    dt_bias = dt + jnp.log(-jnp.expm1(-dt))
    gdn_norm = 1.0 + nrm(keys[6], (L, HEAD_DIM), 0.02)
    nsa_q_norm = 1.0 + nrm(keys[7], (L, HEAD_DIM), 0.02)
    nsa_k_norm = 1.0 + nrm(keys[8], (L, 3, HEAD_DIM), 0.02)
    cmp_pos = nrm(keys[9], (L, 2, CMP_LEN, HEAD_DIM), 0.1)
    cmp_w1 = nrm(keys[10], (L, 2, CMP_LEN * HEAD_DIM, CMP_HIDDEN), (CMP_LEN * HEAD_DIM) ** -0.5)
    cmp_w2 = nrm(keys[11], (L, 2, CMP_HIDDEN, HEAD_DIM), CMP_HIDDEN ** -0.5)
    pool_w = nrm(keys[12], (L, len(POOL_WINDOWS), POOL_GROUP_DIM, POOL_GROUP_DIM), POOL_GROUP_DIM ** -0.5)
    pool_scale = 1.0 + nrm(keys[13], (L, POOL_WIDTH), 0.1)
    w_out = nrm(keys[14], (L, D_MIX, D_MODEL), (2 * DEPTH * D_MIX) ** -0.5)
    norm_ffn = 1.0 + nrm(keys[15], (L, D_MODEL), 0.02)
    w_ffn1 = nrm(keys[16], (L, D_MODEL, D_FF), D_MODEL ** -0.5)
    w_ffn2 = nrm(keys[17], (L, D_FF, D_MODEL), (2 * DEPTH * D_FF) ** -0.5)
    return {"x": x, "norm_mix": norm_mix, "w_in": w_in, "conv_w": conv_w, "a_log": a_log, "dt_bias": dt_bias,
            "gdn_norm": gdn_norm, "nsa_q_norm": nsa_q_norm, "nsa_k_norm": nsa_k_norm, "cmp_pos": cmp_pos,
            "cmp_w1": cmp_w1, "cmp_w2": cmp_w2, "pool_w": pool_w, "pool_scale": pool_scale, "w_out": w_out,
            "norm_ffn": norm_ffn, "w_ffn1": w_ffn1, "w_ffn2": w_ffn2}


def reference(x, norm_mix, w_in, conv_w, a_log, dt_bias, gdn_norm, nsa_q_norm, nsa_k_norm, cmp_pos,
              cmp_w1, cmp_w2, pool_w, pool_scale, w_out, norm_ffn, w_ffn1, w_ffn2):
    for l in range(DEPTH):
        h = rms_norm(x, norm_mix[l])
        proj = h @ w_in[l]
        qkv_a, z_a, b_a, a_a, q_b, kv_b, gate_b, u_c = split_columns(proj)
        y_a = gdn_mixer(qkv_a, z_a, b_a, a_a, conv_w[l], a_log[l], dt_bias[l], gdn_norm[l])
        y_b = nsa_mixer(q_b, kv_b, gate_b, nsa_q_norm[l], nsa_k_norm[l], cmp_pos[l], cmp_w1[l], cmp_w2[l])
        y_c = pool_mixer(u_c, pool_w[l], pool_scale[l])
        y = jnp.concatenate([y_a.astype(x.dtype), y_b.astype(x.dtype), y_c.astype(x.dtype)], axis=-1)
        x = x + y @ w_out[l]
        h = rms_norm(x, norm_ffn[l])
        x = x + jnp.square(jax.nn.relu(h @ w_ffn1[l])) @ w_ffn2[l]
    return x
```

```python
import functools

import jax
import jax.numpy as jnp
from jax import lax
from jax.experimental import pallas as pl
from jax.experimental.pallas import tpu as pltpu

F32 = jnp.float32
BF16 = jnp.bfloat16

D_MODEL = 1024
HEAD_DIM = 64
GDN_HEADS = 6
GDN_CHUNK = 64
GDN_WIDTH = GDN_HEADS * HEAD_DIM
NSA_HEADS = 6
NSA_KV_HEADS = 2
NSA_GROUP = NSA_HEADS // NSA_KV_HEADS
NSA_WIDTH = NSA_HEADS * HEAD_DIM
CMP_LEN = 32
CMP_STRIDE = 16
CMP_HIDDEN = 256
SEL_BLOCK = 64
SEL_TOPK = 16
WINDOW = 512
Q_BLOCK = 128
FORCED_SCORE = 1e4
POOL_WINDOWS = (2, 4, 8, 16)
POOL_GROUP_DIM = 64
POOL_WIDTH = 4 * POOL_GROUP_DIM
D_FF = 4 * D_MODEL
EPS = 1e-6

LANES = 128
SUBLANES = 8
VMEM_LIMIT = 56 * 1024 * 1024
NEG = -1e30

IN_GROUPS = (3 * GDN_WIDTH, GDN_WIDTH, LANES, NSA_WIDTH, 6 * LANES, LANES, POOL_WIDTH)


def _sigmoid(x):
    return 1.0 / (1.0 + jnp.exp(-x))


def _silu(x):
    return x * _sigmoid(x)


def _softplus(x):
    return jnp.maximum(x, 0.0) + jnp.log(1.0 + jnp.exp(-jnp.abs(x)))


def _split_bf16(x):
    hi = x.astype(BF16)
    lo = (x - hi.astype(F32)).astype(BF16)
    return hi, lo


def _dot_x_exact(x, m):
    hi, lo = _split_bf16(x)
    return jnp.dot(hi, m, preferred_element_type=F32) + jnp.dot(lo, m, preferred_element_type=F32)


def _dot_exact_x(m, x):
    hi, lo = _split_bf16(x)
    return jnp.dot(m, hi, preferred_element_type=F32) + jnp.dot(m, lo, preferred_element_type=F32)


def _mm(a, b):
    return jnp.dot(a.astype(BF16), b.astype(BF16), preferred_element_type=F32)


def _mm_nt(a, b):
    return lax.dot_general(a.astype(BF16), b.astype(BF16), (((1,), (1,)), ((), ())),
                           preferred_element_type=F32)


def _mm_tn(a, b):
    return lax.dot_general(a.astype(BF16), b.astype(BF16), (((0,), (0,)), ((), ())),
                           preferred_element_type=F32)


def _rms(x, g):
    return x * lax.rsqrt(jnp.mean(x * x, axis=-1, keepdims=True) + EPS) * g


def _const_spec(shape):
    nd = len(shape)
    return pl.BlockSpec(shape, lambda *_: (0,) * nd)


def _params(sem):
    return pltpu.CompilerParams(dimension_semantics=sem, vmem_limit_bytes=VMEM_LIMIT)


def _inproj_body(x_ref, g_ref, w_ref, *out_refs):
    h = _rms(x_ref[...], g_ref[...]).astype(BF16)
    off = 0
    for o_ref in out_refs:
        n = o_ref.shape[-1]
        o_ref[...] = jnp.dot(h, w_ref[:, off:off + n], preferred_element_type=F32)
        off += n


def _inproj(xf, g, w_pad, tm=512):
    n = xf.shape[0]
    d_in = w_pad.shape[1]
    return pl.pallas_call(
        _inproj_body,
        grid=(n // tm,),
        in_specs=[pl.BlockSpec((tm, D_MODEL), lambda i: (i, 0)),
                  _const_spec((1, D_MODEL)),
                  _const_spec((D_MODEL, d_in))],
        out_specs=[pl.BlockSpec((tm, c), lambda i: (i, 0)) for c in IN_GROUPS],
        out_shape=[jax.ShapeDtypeStruct((n, c), F32) for c in IN_GROUPS],
        compiler_params=_params(("parallel",)),
        name="inproj",
    )(xf, g.reshape(1, D_MODEL), w_pad)


def _gdn_prep_body(x_ref, halo_ref, ba_ref, cw_ref, alog_ref, dtb_ref, seg_ref, eb_ref, eg_ref,
                   q_ref, k_ref, v_ref, be_ref, ge_ref):
    i = pl.program_id(1)
    x = x_ref[0]
    halo = jnp.where(i > 0, halo_ref[0], 0.0)
    xe = jnp.concatenate([halo, x], axis=0)
    cw = cw_ref[...]
    y = x * cw[3:4, :]
    for s in (1, 2, 3):
        y = y + pltpu.roll(xe, s, 0)[SUBLANES:] * cw[3 - s:4 - s, :]
    a = _silu(y)
    q = a[:, :GDN_WIDTH]
    k = a[:, GDN_WIDTH:2 * GDN_WIDTH]
    seg = seg_ref[...]
    q_ref[0] = q * lax.rsqrt(_dot_x_exact(q * q, seg) + EPS) * (HEAD_DIM ** -0.5)
    k_ref[0] = k * lax.rsqrt(_dot_x_exact(k * k, seg) + EPS)
    v_ref[0] = a[:, 2 * GDN_WIDTH:]
    ba = ba_ref[0]
    beta = _sigmoid(ba)
    gval = -jnp.exp(alog_ref[...]) * _softplus(ba + dtb_ref[...])
    be_ref[0] = _dot_x_exact(beta, eb_ref[...])
    ge_ref[0] = _dot_x_exact(gval, eg_ref[...])


def _gdn_prep(qkv, ba, conv_w, a_log, dt_bias, tm=256):
    B, T, C = qkv.shape
    H = GDN_HEADS
    alog_p = jnp.zeros((1, LANES), F32).at[0, H:2 * H].set(a_log)
    dtb_p = jnp.zeros((1, LANES), F32).at[0, H:2 * H].set(dt_bias)
    ch = jnp.arange(GDN_WIDTH) // HEAD_DIM
    seg = (ch[:, None] == ch[None, :]).astype(BF16)
    row = jnp.arange(LANES)
    eb = (row[:, None] == ch[None, :]).astype(BF16)
    eg = (row[:, None] - H == ch[None, :]).astype(BF16)
    tok = lambda c: pl.BlockSpec((1, tm, c), lambda b, i: (b, i, 0))
    halo_blocks = tm // SUBLANES
    out = jax.ShapeDtypeStruct((B, T, GDN_WIDTH), F32)
    return pl.pallas_call(
        _gdn_prep_body,
        grid=(B, T // tm),
        in_specs=[tok(C),
                  pl.BlockSpec((1, SUBLANES, C), lambda b, i: (b, jnp.maximum(i * halo_blocks - 1, 0), 0)),
                  tok(LANES),
                  _const_spec(conv_w.shape), _const_spec((1, LANES)), _const_spec((1, LANES)),
                  _const_spec(seg.shape), _const_spec(eb.shape), _const_spec(eg.shape)],
        out_specs=[tok(GDN_WIDTH)] * 5,
        out_shape=[out] * 5,
        compiler_params=_params(("parallel", "parallel")),
        name="gdn_prep",
    )(qkv, qkv, ba, conv_w, alog_p, dtb_p, seg, eb, eg)


def _gdn_scan_body(q_ref, k_ref, v_ref, be_ref, ge_ref, z_ref, tri_ref, gn_ref, o_ref, s_ref, *, cps):
    C = GDN_CHUNK
    P = 2 * C

    @pl.when(pl.program_id(1) == 0)
    def _():
        s_ref[...] = jnp.zeros_like(s_ref)

    lo_lane = lax.broadcasted_iota(jnp.int32, (C, LANES), 1) < HEAD_DIM
    r = lax.broadcasted_iota(jnp.int32, (P, P), 0)
    c = lax.broadcasted_iota(jnp.int32, (P, P), 1)
    same = (r < C) == (c < C)
    causal = same & (r >= c)
    strict = same & (r > c)
    eye = (r == c).astype(F32)
    tri = tri_ref[...]
    gn = gn_ref[...]

    def blk(x):
        return jnp.concatenate([jnp.where(lo_lane, x, 0.0), jnp.where(lo_lane, 0.0, x)], axis=0)

    for b in range(q_ref.shape[0]):
        S = s_ref[b]
        for j in range(cps):
            rows = slice(C * j, C * (j + 1))
            q2, k2, v2 = blk(q_ref[b, rows, :]), blk(k_ref[b, rows, :]), blk(v_ref[b, rows, :])
            b2, g2 = blk(be_ref[b, rows, :]), blk(ge_ref[b, rows, :])
            gc = _dot_exact_x(tri, g2)
            decay = jnp.exp(jnp.where(causal, gc - gc.T, NEG))
            kb = k2 * b2
            L = jnp.where(strict, _mm_nt(kb, k2) * decay, 0.0)
            X = eye - L
            Pw = _mm(L, L)
            for it in range(5):
                X = X + _mm(X, Pw)
                if it < 4:
                    Pw = _mm(Pw, Pw)
            egc = jnp.exp(gc)
            uw = _mm(X, jnp.concatenate([v2 * b2, kb * egc], axis=1))
            u2, w2 = uw[:, :LANES], uw[:, LANES:]
            qk = _mm_nt(q2, k2) * decay
            glast = jnp.concatenate([jnp.broadcast_to(gc[C - 1:C, :], (C, LANES)),
                                     jnp.broadcast_to(gc[P - 1:P, :], (C, LANES))], axis=0)
            kd = k2 * jnp.exp(glast - gc)
            vnew = u2 - _mm(w2, S)
            o2 = _mm(q2 * egc, S) + _mm(qk, vnew)
            S = S * jnp.exp(glast) + _mm_tn(kd, vnew)
            ms = jnp.sum(o2 * o2, axis=-1, keepdims=True) * (1.0 / HEAD_DIM)
            on = o2 * lax.rsqrt(ms + EPS) * gn
            z = z_ref[b, rows, :]
            o_ref[b, rows, :] = (on[:C] + on[C:]) * _silu(z)
        s_ref[b] = S


def _gdn_scan(q, k, v, be, ge, z, gdn_norm, cps=4):
    B, T, _ = q.shape
    C = GDN_CHUNK
    tm = C * cps
    i2 = jnp.arange(2 * C)
    tri = (((i2[:, None] // C) == (i2[None, :] // C)) & (i2[:, None] >= i2[None, :])).astype(BF16)
    gn2 = jnp.tile(gdn_norm, 2).reshape(1, LANES)
    tok = pl.BlockSpec((B, tm, LANES), lambda p, c: (0, c, p))
    return pl.pallas_call(
        functools.partial(_gdn_scan_body, cps=cps),
        grid=(GDN_WIDTH // LANES, T // tm),
        in_specs=[tok] * 6 + [_const_spec(tri.shape), _const_spec((1, LANES))],
        out_specs=tok,
        out_shape=jax.ShapeDtypeStruct((B, T, GDN_WIDTH), F32),
        scratch_shapes=[pltpu.VMEM((B, 2 * C, LANES), F32)],
        compiler_params=_params(("parallel", "arbitrary")),
        name="gdn_scan",
    )(q, k, v, be, ge, z, tri, gn2)


def _nsa_prep_body(q_ref, kv_ref, segq_ref, segk_ref, qn_ref, kn_ref,
                   qT_ref, ks_ref, kw_ref, vsT_ref, vwT_ref):
    q = q_ref[0]
    ms = _dot_x_exact(q * q, segq_ref[...]) * (1.0 / HEAD_DIM)
    qn = q * lax.rsqrt(ms + EPS) * qn_ref[...] * (HEAD_DIM ** -0.5)
    qT_ref[0] = qn.T.astype(BF16)
    kv = kv_ref[0]
    segk = segk_ref[...]
    kn = kn_ref[...]

    def knorm(k, g):
        msk = _dot_x_exact(k * k, segk) * (1.0 / HEAD_DIM)
        return (k * lax.rsqrt(msk + EPS) * g).astype(BF16)

    ks_ref[0] = knorm(kv[:, 2 * LANES:3 * LANES], kn[0:1, :])
    kw_ref[0] = knorm(kv[:, 4 * LANES:5 * LANES], kn[1:2, :])
    vsT_ref[0] = kv[:, 3 * LANES:4 * LANES].T.astype(BF16)
    vwT_ref[0] = kv[:, 5 * LANES:6 * LANES].T.astype(BF16)


def _nsa_prep(qb, kvb, q_norm, k_norm, tm=512):
    B, T, _ = qb.shape
    chq = jnp.arange(NSA_WIDTH) // HEAD_DIM
    segq = (chq[:, None] == chq[None, :]).astype(BF16)
    segk = segq[:LANES, :LANES]
    qn = jnp.tile(q_norm, NSA_HEADS).reshape(1, NSA_WIDTH)
    kn = jnp.tile(k_norm[1:3], (1, NSA_KV_HEADS))
    tok = lambda c: pl.BlockSpec((1, tm, c), lambda b, i: (b, i, 0))
    tr = lambda c: pl.BlockSpec((1, c, tm), lambda b, i: (b, 0, i))
    return pl.pallas_call(
        _nsa_prep_body,
        grid=(B, T // tm),
        in_specs=[tok(NSA_WIDTH), tok(6 * LANES), _const_spec(segq.shape), _const_spec(segk.shape),
                  _const_spec(qn.shape), _const_spec(kn.shape)],
        out_specs=[tr(NSA_WIDTH), tok(LANES), tok(LANES), tr(LANES), tr(LANES)],
        out_shape=[jax.ShapeDtypeStruct((B, NSA_WIDTH, T), BF16),
                   jax.ShapeDtypeStruct((B, T, LANES), BF16),
                   jax.ShapeDtypeStruct((B, T, LANES), BF16),
                   jax.ShapeDtypeStruct((B, LANES, T), BF16),
                   jax.ShapeDtypeStruct((B, LANES, T), BF16)],
        compiler_params=_params(("parallel", "parallel")),
        name="nsa_prep",
    )(qb, kvb, segq, segk, qn, kn)


def _nsa_compress_body(xk_ref, xv_ref, w1k_ref, w1v_ref, w2k_ref, w2vT_ref, posk_ref, posv_ref, kn_ref,
                       kc_ref, vcT_ref):
    half = CMP_STRIDE * HEAD_DIM

    def hidden(x_ref, w1_ref, pos_ref):
        x = x_ref[0, 0]
        nr = x.shape[0]
        ya = jnp.dot(x, w1_ref[:half, :], preferred_element_type=F32)
        yb = jnp.dot(x, w1_ref[half:, :], preferred_element_type=F32)
        bias = jnp.dot(pos_ref[...], w1_ref[...], preferred_element_type=F32)[0:1, :]
        return _silu(ya + pltpu.roll(yb, nr - 1, 0) + bias).astype(BF16)

    hk = hidden(xk_ref, w1k_ref, posk_ref)
    kc = jnp.dot(hk, w2k_ref[...], preferred_element_type=F32)
    kc_ref[0, 0] = _rms(kc, kn_ref[...]).astype(BF16)
    hv = hidden(xv_ref, w1v_ref, posv_ref)
    vcT_ref[0, 0] = lax.dot_general(w2vT_ref[...], hv, (((1,), (1,)), ((), ())),
                                    preferred_element_type=F32).astype(BF16)


def _nsa_compress(kvb, cmp_pos, cmp_w1, cmp_w2, k_norm0):
    B, T, _ = kvb.shape
    nr = T // CMP_STRIDE
    wide = CMP_STRIDE * HEAD_DIM

    def rows16(cols):
        return cols.reshape(B, nr, CMP_STRIDE, NSA_KV_HEADS, HEAD_DIM).transpose(0, 3, 1, 2, 4) \
                   .reshape(B, NSA_KV_HEADS, nr, wide).astype(BF16)

    xk = rows16(kvb[..., :LANES])
    xv = rows16(kvb[..., LANES:2 * LANES])
    pos = jnp.broadcast_to(cmp_pos.reshape(2, 1, CMP_LEN * HEAD_DIM), (2, SUBLANES, CMP_LEN * HEAD_DIM)).astype(BF16)
    w1 = cmp_w1.astype(BF16)
    xspec = pl.BlockSpec((1, 1, nr, wide), lambda b, h: (b, h, 0, 0))
    return pl.pallas_call(
        _nsa_compress_body,
        grid=(B, NSA_KV_HEADS),
        in_specs=[xspec, xspec, _const_spec(w1[0].shape), _const_spec(w1[1].shape),
                  _const_spec((CMP_HIDDEN, HEAD_DIM)), _const_spec((HEAD_DIM, CMP_HIDDEN)),
                  _const_spec(pos[0].shape), _const_spec(pos[1].shape), _const_spec((1, HEAD_DIM))],
        out_specs=[pl.BlockSpec((1, 1, nr, HEAD_DIM), lambda b, h: (b, h, 0, 0)),
                   pl.BlockSpec((1, 1, HEAD_DIM, nr), lambda b, h: (b, h, 0, 0))],
        out_shape=[jax.ShapeDtypeStruct((B, NSA_KV_HEADS, nr, HEAD_DIM), BF16),
                   jax.ShapeDtypeStruct((B, NSA_KV_HEADS, HEAD_DIM, nr), BF16)],
        compiler_params=_params(("parallel", "parallel")),
        name="nsa_compress",
    )(xk, xv, w1[0], w1[1], cmp_w2[0].astype(BF16), cmp_w2[1].T.astype(BF16), pos[0], pos[1],
      k_norm0.reshape(1, HEAD_DIM))


KEY_TILE = 512
WIN_KEYS = WINDOW + Q_BLOCK


def _masked_softmax_cols(s, mask):
    s = jnp.where(mask, s, NEG)
    m = jnp.max(s, axis=0, keepdims=True)
    e = jnp.where(mask, jnp.exp(s - m), 0.0)
    l = jnp.sum(e, axis=0, keepdims=True)
    return e / jnp.where(l > 0.0, l, 1.0)


def _nsa_attn_body(qT_ref, kc_ref, vcT_ref, ks_ref, vsT_ref, kw_ref, vwT_ref, gate_ref, selm_ref,
                   o_ref, sel_ref, gT_ref, *, seq):
    G, Dh, Qb = NSA_GROUP, HEAD_DIM, Q_BLOCK
    h = pl.program_id(1)
    t0 = pl.program_id(2) * Qb
    nr = seq // CMP_STRIDE
    ns = seq // SEL_BLOCK
    pos = t0 + lax.broadcasted_iota(jnp.int32, (1, Qb), 1)

    qT = qT_ref[0]
    q3 = jnp.concatenate([qT[g * Dh:(g + 1) * Dh, :] for g in range(G)], axis=1)
    row = lax.broadcasted_iota(jnp.int32, (2 * Dh, G * Qb), 0)
    qpad = jnp.where((row < Dh) == (h == 0), jnp.concatenate([q3, q3], axis=0), 0).astype(BF16)

    sc = jnp.dot(kc_ref[0, 0], q3, preferred_element_type=F32)
    cmp_end = lax.broadcasted_iota(jnp.int32, (nr, 1), 0) * CMP_STRIDE + (CMP_LEN - 1)
    mask_c = cmp_end <= pos
    vcT = vcT_ref[0, 0]
    psum = jnp.zeros((nr, Qb), F32)
    o_c = []
    for g in range(G):
        p = _masked_softmax_cols(sc[:, g * Qb:(g + 1) * Qb], mask_c)
        psum = psum + p
        o_c.append(jnp.dot(vcT, p.astype(BF16), preferred_element_type=F32))

    imp = _dot_exact_x(selm_ref[...], psum)
    blk = lax.broadcasted_iota(jnp.int32, (ns, Qb), 0)
    cur = pos // SEL_BLOCK
    forced = (blk == 0) | (blk == cur) | (blk == cur - 1)
    valid = blk * SEL_BLOCK <= pos
    score = jnp.where(forced, FORCED_SCORE, jnp.where(valid, imp, -1.0))
    sel = jnp.zeros((ns, Qb), F32)
    for _ in range(min(SEL_TOPK, ns)):
        best = jnp.max(score, axis=0, keepdims=True)
        first = jnp.min(jnp.where(score == best, blk, ns), axis=0, keepdims=True)
        hit = blk == first
        sel = jnp.where(hit, 1.0, sel)
        score = jnp.where(hit, -jnp.inf, score)
    sel_ref[...] = sel

    bpt = KEY_TILE // SEL_BLOCK
    krow = lax.broadcasted_iota(jnp.int32, (KEY_TILE, 1), 0)

    def sel_step(kt, carry):
        ms, ls, accs = carry
        base = pl.multiple_of(kt * KEY_TILE, KEY_TILE)
        s_all = jnp.dot(ks_ref[0, pl.ds(base, KEY_TILE), :], qpad, preferred_element_type=F32)
        vT = vsT_ref[0, :, pl.ds(base, KEY_TILE)]
        selt = sel_ref[pl.ds(pl.multiple_of(kt * bpt, bpt), bpt), :]
        selx = jnp.broadcast_to(selt[:, None, :], (bpt, SEL_BLOCK, Qb)).reshape(KEY_TILE, Qb)
        mask = (selx > 0.5) & (base + krow <= pos)
        ms2, ls2, accs2 = [], [], []
        for g in range(G):
            s = jnp.where(mask, s_all[:, g * Qb:(g + 1) * Qb], NEG)
            m_new = jnp.maximum(ms[g], jnp.max(s, axis=0, keepdims=True))
            alpha = jnp.exp(ms[g] - m_new)
            e = jnp.where(mask, jnp.exp(s - m_new), 0.0)
            ms2.append(m_new)
            ls2.append(alpha * ls[g] + jnp.sum(e, axis=0, keepdims=True))
            accs2.append(alpha * accs[g] + jnp.dot(vT, e.astype(BF16), preferred_element_type=F32))
        return tuple(ms2), tuple(ls2), tuple(accs2)

    init = (tuple(jnp.full((1, Qb), NEG, F32) for _ in range(G)),
            tuple(jnp.zeros((1, Qb), F32) for _ in range(G)),
            tuple(jnp.zeros((Dh, Qb), F32) for _ in range(G)))
    n_tiles = (t0 + Qb + KEY_TILE - 1) // KEY_TILE
    _, ls, accs = lax.fori_loop(0, n_tiles, sel_step, init)
    o_s = [accs[g] / jnp.where(ls[g] > 0.0, ls[g], 1.0) for g in range(G)]

    ws = pl.multiple_of(jnp.maximum(t0 - WINDOW, 0), Qb)
    sw = jnp.dot(kw_ref[0, pl.ds(ws, WIN_KEYS), :], qpad, preferred_element_type=F32)
    vwT = vwT_ref[0, :, pl.ds(ws, WIN_KEYS)]
    kpos = ws + lax.broadcasted_iota(jnp.int32, (WIN_KEYS, 1), 0)
    mask_w = (kpos <= pos) & (kpos > pos - WINDOW)
    o_w = []
    for g in range(G):
        p = _masked_softmax_cols(sw[:, g * Qb:(g + 1) * Qb], mask_w)
        o_w.append(jnp.dot(vwT, p.astype(BF16), preferred_element_type=F32))

    gT_ref[...] = _sigmoid(gate_ref[0]).T
    for g in range(G):
        gates = gT_ref[pl.ds((h * G + g) * 3, 3), :]
        o_ref[0, g * Dh:(g + 1) * Dh, :] = (gates[0:1] * o_c[g] + gates[1:2] * o_s[g] + gates[2:3] * o_w[g])


def _nsa_attn(qT, kc, vcT, ks, vsT, kw, vwT, gate):
    B, _, T = qT.shape
    assert T % KEY_TILE == 0 and T >= WIN_KEYS
    nr, ns = T // CMP_STRIDE, T // SEL_BLOCK
    G, Dh, Qb = NSA_GROUP, HEAD_DIM, Q_BLOCK
    j = jnp.arange(ns)[:, None]
    c = jnp.arange(nr)[None, :]
    ratio = SEL_BLOCK // CMP_STRIDE
    selm = ((c >= ratio * j - 1) & (c <= ratio * j + ratio - 1) & (c < nr - 1)).astype(BF16)
    kfull = pl.BlockSpec((1, T, LANES), lambda b, h, i: (b, 0, 0))
    vfull = pl.BlockSpec((1, Dh, T), lambda b, h, i: (b, h, 0))
    return pl.pallas_call(
        functools.partial(_nsa_attn_body, seq=T),
        grid=(B, NSA_KV_HEADS, T // Qb),
        in_specs=[pl.BlockSpec((1, G * Dh, Qb), lambda b, h, i: (b, h, i)),
                  pl.BlockSpec((1, 1, nr, Dh), lambda b, h, i: (b, h, 0, 0)),
                  pl.BlockSpec((1, 1, Dh, nr), lambda b, h, i: (b, h, 0, 0)),
                  kfull, vfull, kfull, vfull,
                  pl.BlockSpec((1, Qb, LANES), lambda b, h, i: (b, i, 0)),
                  _const_spec(selm.shape)],
        out_specs=pl.BlockSpec((1, G * Dh, Qb), lambda b, h, i: (b, h, i)),
        out_shape=jax.ShapeDtypeStruct((B, NSA_WIDTH, T), F32),
        scratch_shapes=[pltpu.VMEM((ns, Qb), F32), pltpu.VMEM((LANES, Qb), F32)],
        compiler_params=_params(("parallel", "parallel", "arbitrary")),
        name="nsa_attn",
    )(qT, kc, vcT, ks, vsT, kw, vwT, gate, selm)


POOL_HALO = 16


def _pool_body(u_ref, halo_ref, w_ref, sc_ref, o_ref):
    i = pl.program_id(1)
    u = u_ref[0]
    tm = u.shape[0]
    halo = jnp.where(i > 0, halo_ref[0], 0.0)
    ue = jnp.concatenate([halo, u], axis=0)
    lane = lax.broadcasted_iota(jnp.int32, (1, POOL_WIDTH), 1)
    grp = lane // POOL_GROUP_DIM
    acc = ue
    wsum = jnp.zeros_like(ue)
    width = jnp.zeros((1, POOL_WIDTH), F32)
    for gi, w in enumerate(POOL_WINDOWS):
        while_shift = w // 2
        acc = acc + pltpu.roll(acc, while_shift, 0)
        wsum = jnp.where(grp == gi, acc, wsum)
        width = jnp.where(grp == gi, float(w), width)
    t1 = (i * tm + 1 + lax.broadcasted_iota(jnp.int32, (tm, 1), 0)).astype(F32)
    mean = wsum[POOL_HALO:] / jnp.minimum(t1, width)
    o_ref[0] = jnp.dot((mean - u).astype(BF16), w_ref[...], preferred_element_type=F32) * sc_ref[...]


def _pool(u, pool_w, pool_scale, tm=512):
    B, T, _ = u.shape
    wblk = jax.scipy.linalg.block_diag(*[pool_w[g] for g in range(len(POOL_WINDOWS))]).astype(BF16)
    tok = pl.BlockSpec((1, tm, POOL_WIDTH), lambda b, i: (b, i, 0))
    hb = tm // POOL_HALO
    return pl.pallas_call(
        _pool_body,
        grid=(B, T // tm),
        in_specs=[tok,
                  pl.BlockSpec((1, POOL_HALO, POOL_WIDTH), lambda b, i: (b, jnp.maximum(i * hb - 1, 0), 0)),
                  _const_spec(wblk.shape), _const_spec((1, POOL_WIDTH))],
        out_specs=tok,
        out_shape=jax.ShapeDtypeStruct((B, T, POOL_WIDTH), F32),
        compiler_params=_params(("parallel", "parallel")),
        name="pool",
    )(u, u, wblk, pool_scale.reshape(1, POOL_WIDTH))


def _outproj_body(x_ref, ya_ref, ybT_ref, yc_ref, wa_ref, wb_ref, wc_ref, o_ref):
    acc = x_ref[0]
    acc = acc + jnp.dot(ya_ref[0].astype(BF16), wa_ref[...], preferred_element_type=F32)
    acc = acc + jnp.dot(ybT_ref[0].T.astype(BF16), wb_ref[...], preferred_element_type=F32)
    acc = acc + jnp.dot(yc_ref[0].astype(BF16), wc_ref[...], preferred_element_type=F32)
    o_ref[0] = acc


def _outproj(x, ya, ybT, yc, w_out, tm=512):
    B, T, _ = x.shape
    w = w_out.astype(BF16)
    wa, wb, wc = w[:GDN_WIDTH], w[GDN_WIDTH:GDN_WIDTH + NSA_WIDTH], w[GDN_WIDTH + NSA_WIDTH:]
    tok = lambda c: pl.BlockSpec((1, tm, c), lambda b, i: (b, i, 0))
    return pl.pallas_call(
        _outproj_body,
        grid=(B, T // tm),
        in_specs=[tok(D_MODEL), tok(GDN_WIDTH), pl.BlockSpec((1, NSA_WIDTH, tm), lambda b, i: (b, 0, i)),
                  tok(POOL_WIDTH), _const_spec(wa.shape), _const_spec(wb.shape), _const_spec(wc.shape)],
        out_specs=tok(D_MODEL),
        out_shape=jax.ShapeDtypeStruct(x.shape, F32),
        compiler_params=_params(("parallel", "parallel")),
        name="outproj",
    )(x, ya, ybT, yc, wa, wb, wc)


FF_TILE = 1024


def _ffn_body(x_ref, g_ref, w1_ref, w2_ref, o_ref):
    x = x_ref[...]
    h = _rms(x, g_ref[...]).astype(BF16)
    acc = x
    for j in range(D_FF // FF_TILE):
        cols = slice(j * FF_TILE, (j + 1) * FF_TILE)
        a = jnp.maximum(jnp.dot(h, w1_ref[:, cols], preferred_element_type=F32), 0.0)
        acc = acc + jnp.dot((a * a).astype(BF16), w2_ref[cols, :], preferred_element_type=F32)
    o_ref[...] = acc


def _ffn(xf, g, w1, w2, tm=512):
    n = xf.shape[0]
    tok = pl.BlockSpec((tm, D_MODEL), lambda i: (i, 0))
    return pl.pallas_call(
        _ffn_body,
        grid=(n // tm,),
        in_specs=[tok, _const_spec((1, D_MODEL)), _const_spec(w1.shape), _const_spec(w2.shape)],
        out_specs=tok,
        out_shape=jax.ShapeDtypeStruct(xf.shape, F32),
        compiler_params=_params(("parallel",)),
        name="ffn",
    )(xf, g.reshape(1, D_MODEL), w1.astype(BF16), w2.astype(BF16))


def _pad_in_weights(w):
    H = GDN_HEADS
    o_ba = 4 * GDN_WIDTH
    o_q = o_ba + 2 * H
    o_gate = o_q + NSA_WIDTH + 6 * LANES
    o_u = o_gate + 3 * NSA_HEADS
    pad = lambda a: jnp.pad(a, ((0, 0), (0, LANES - a.shape[1])))
    return jnp.concatenate([w[:, :o_ba], pad(w[:, o_ba:o_q]), w[:, o_q:o_gate], pad(w[:, o_gate:o_u]),
                            w[:, o_u:]], axis=1).astype(BF16)


def _mixers(qkv, z, ba, qb, kvb, gate, u, conv_w, a_log, dt_bias, gdn_norm, nsa_q_norm, nsa_k_norm,
            cmp_pos, cmp_w1, cmp_w2, pool_w, pool_scale):
    q, k, v, be, ge = _gdn_prep(qkv, ba, conv_w, a_log, dt_bias)
    ya = _gdn_scan(q, k, v, be, ge, z, gdn_norm)
    qT, ks, kw, vsT, vwT = _nsa_prep(qb, kvb, nsa_q_norm, nsa_k_norm)
    kc, vcT = _nsa_compress(kvb, cmp_pos, cmp_w1, cmp_w2, nsa_k_norm[0])
    ybT = _nsa_attn(qT, kc, vcT, ks, vsT, kw, vwT, gate)
    yc = _pool(u, pool_w, pool_scale)
    return ya, ybT, yc


def kernel(x, norm_mix, w_in, conv_w, a_log, dt_bias, gdn_norm, nsa_q_norm, nsa_k_norm, cmp_pos, cmp_w1, cmp_w2,
           pool_w, pool_scale, w_out, norm_ffn, w_ffn1, w_ffn2):
    B, T, D = x.shape
    for l in range(w_in.shape[0]):
        proj = _inproj(x.reshape(B * T, D), norm_mix[l], _pad_in_weights(w_in[l]))
        qkv, z, ba, qb, kvb, gate, u = [p.reshape(B, T, -1) for p in proj]
        ya, ybT, yc = _mixers(qkv, z, ba, qb, kvb, gate, u, conv_w[l], a_log[l], dt_bias[l], gdn_norm[l],
                              nsa_q_norm[l], nsa_k_norm[l], cmp_pos[l], cmp_w1[l], cmp_w2[l],
                              pool_w[l], pool_scale[l])
        x = _outproj(x, ya, ybT, yc, w_out[l])
        x = _ffn(x.reshape(B * T, D), norm_ffn[l], w_ffn1[l], w_ffn2[l]).reshape(B, T, D)
    return x
```

```python
import functools

import jax
import jax.numpy as jnp
from jax import lax
from jax.experimental import pallas as pl
from jax.experimental.pallas import tpu as pltpu

F32 = jnp.float32
BF16 = jnp.bfloat16

D_MODEL = 1024
HEAD_DIM = 64
GDN_HEADS = 6
GDN_CHUNK = 64
GDN_WIDTH = GDN_HEADS * HEAD_DIM
NSA_HEADS = 6
NSA_KV_HEADS = 2
NSA_GROUP = NSA_HEADS // NSA_KV_HEADS
NSA_WIDTH = NSA_HEADS * HEAD_DIM
CMP_LEN = 32
CMP_STRIDE = 16
CMP_HIDDEN = 256
SEL_BLOCK = 64
SEL_TOPK = 16
WINDOW = 512
Q_BLOCK = 128
FORCED_SCORE = 1e4
POOL_WINDOWS = (2, 4, 8, 16)
POOL_GROUP_DIM = 64
POOL_WIDTH = 4 * POOL_GROUP_DIM
D_FF = 4 * D_MODEL
EPS = 1e-6

LANES = 128
SUBLANES = 8
VMEM_LIMIT = 56 * 1024 * 1024
NEG = -1e30
LOG2E = 1.4426950408889634
KEY_TILE = 1024
WIN_KEYS = 512 + 128
V_ROWS = 80

IN_GROUPS = (3 * GDN_WIDTH, GDN_WIDTH, LANES, NSA_WIDTH, 6 * LANES, LANES, POOL_WIDTH)


def _sigmoid(x):
    return 1.0 / (1.0 + jnp.exp(-x))


def _silu(x):
    return x * _sigmoid(x)


def _softplus(x):
    return jnp.maximum(x, 0.0) + jnp.log(1.0 + jnp.exp(-jnp.abs(x)))


def _split_bf16(x):
    hi = x.astype(BF16)
    lo = (x - hi.astype(F32)).astype(BF16)
    return hi, lo


def _dot_x_exact(x, m):
    hi, lo = _split_bf16(x)
    return jnp.dot(hi, m, preferred_element_type=F32) + jnp.dot(lo, m, preferred_element_type=F32)


def _dot_exact_x(m, x):
    hi, lo = _split_bf16(x)
    return jnp.dot(m, hi, preferred_element_type=F32) + jnp.dot(m, lo, preferred_element_type=F32)


def _mm(a, b):
    return jnp.dot(a.astype(BF16), b.astype(BF16), preferred_element_type=F32)


def _mm_nt(a, b):
    return lax.dot_general(a.astype(BF16), b.astype(BF16), (((1,), (1,)), ((), ())),
                           preferred_element_type=F32)


def _mm_tn(a, b):
    return lax.dot_general(a.astype(BF16), b.astype(BF16), (((0,), (0,)), ((), ())),
                           preferred_element_type=F32)


def _rms(x, g):
    return x * lax.rsqrt(jnp.mean(x * x, axis=-1, keepdims=True) + EPS) * g


def _const_spec(shape):
    nd = len(shape)
    return pl.BlockSpec(shape, lambda *_: (0,) * nd)


def _params(sem):
    return pltpu.CompilerParams(dimension_semantics=sem, vmem_limit_bytes=VMEM_LIMIT)


def _inproj_body(x_ref, g_ref, w_ref, *out_refs):
    h = _rms(x_ref[...], g_ref[...]).astype(BF16)
    off = 0
    for o_ref in out_refs:
        n = o_ref.shape[-1]
        o_ref[...] = jnp.dot(h, w_ref[:, off:off + n], preferred_element_type=F32)
        off += n


def _inproj(xf, g, w_pad, tm=512):
    n = xf.shape[0]
    d_in = w_pad.shape[1]
    return pl.pallas_call(
        _inproj_body,
        grid=(n // tm,),
        in_specs=[pl.BlockSpec((tm, D_MODEL), lambda i: (i, 0)),
                  _const_spec((1, D_MODEL)),
                  _const_spec((D_MODEL, d_in))],
        out_specs=[pl.BlockSpec((tm, c), lambda i: (i, 0)) for c in IN_GROUPS],
        out_shape=[jax.ShapeDtypeStruct((n, c), F32) for c in IN_GROUPS],
        compiler_params=_params(("parallel",)),
        name="inproj",
    )(xf, g.reshape(1, D_MODEL), w_pad)


def _gdn_prep_body(x_ref, halo_ref, ba_ref, cw_ref, alog_ref, dtb_ref, seg_ref, eb_ref, eg_ref,
                   q_ref, k_ref, v_ref, be_ref, ge_ref):
    i = pl.program_id(1)
    x = x_ref[0]
    halo = jnp.where(i > 0, halo_ref[0], 0.0)
    xe = jnp.concatenate([halo, x], axis=0)
    cw = cw_ref[...]
    y = x * cw[3:4, :]
    for s in (1, 2, 3):
        y = y + pltpu.roll(xe, s, 0)[SUBLANES:] * cw[3 - s:4 - s, :]
    a = _silu(y)
    q = a[:, :GDN_WIDTH]
    k = a[:, GDN_WIDTH:2 * GDN_WIDTH]
    seg = seg_ref[...]
    q_ref[0] = q * lax.rsqrt(_dot_x_exact(q * q, seg) + EPS) * (HEAD_DIM ** -0.5)
    k_ref[0] = k * lax.rsqrt(_dot_x_exact(k * k, seg) + EPS)
    v_ref[0] = a[:, 2 * GDN_WIDTH:]
    ba = ba_ref[0]
    beta = _sigmoid(ba)
    gval = -jnp.exp(alog_ref[...]) * _softplus(ba + dtb_ref[...])
    be_ref[0] = _dot_x_exact(beta, eb_ref[...])
    ge_ref[0] = _dot_x_exact(gval, eg_ref[...])


def _gdn_prep(qkv, ba, conv_w, a_log, dt_bias, tm=256):
    B, T, C = qkv.shape
    H = GDN_HEADS
    alog_p = jnp.zeros((1, LANES), F32).at[0, H:2 * H].set(a_log)
    dtb_p = jnp.zeros((1, LANES), F32).at[0, H:2 * H].set(dt_bias)
    ch = jnp.arange(GDN_WIDTH) // HEAD_DIM
    seg = (ch[:, None] == ch[None, :]).astype(BF16)
    row = jnp.arange(LANES)
    eb = (row[:, None] == ch[None, :]).astype(BF16)
    eg = (row[:, None] - H == ch[None, :]).astype(BF16)
    tok = lambda c: pl.BlockSpec((1, tm, c), lambda b, i: (b, i, 0))
    halo_blocks = tm // SUBLANES
    out = jax.ShapeDtypeStruct((B, T, GDN_WIDTH), F32)
    return pl.pallas_call(
        _gdn_prep_body,
        grid=(B, T // tm),
        in_specs=[tok(C),
                  pl.BlockSpec((1, SUBLANES, C), lambda b, i: (b, jnp.maximum(i * halo_blocks - 1, 0), 0)),
                  tok(LANES),
                  _const_spec(conv_w.shape), _const_spec((1, LANES)), _const_spec((1, LANES)),
                  _const_spec(seg.shape), _const_spec(eb.shape), _const_spec(eg.shape)],
        out_specs=[tok(GDN_WIDTH)] * 5,
        out_shape=[out] * 5,
        compiler_params=_params(("parallel", "parallel")),
        name="gdn_prep",
    )(qkv, qkv, ba, conv_w, alog_p, dtb_p, seg, eb, eg)


def _gdn_scan_body(q_ref, k_ref, v_ref, be_ref, ge_ref, z_ref, tri_ref, gn_ref, o_ref, s_ref, *, cps):
    C = GDN_CHUNK
    P = 2 * C

    @pl.when(pl.program_id(1) == 0)
    def _():
        s_ref[...] = jnp.zeros_like(s_ref)

    lo_lane = lax.broadcasted_iota(jnp.int32, (C, LANES), 1) < HEAD_DIM
    r = lax.broadcasted_iota(jnp.int32, (P, P), 0)
    c = lax.broadcasted_iota(jnp.int32, (P, P), 1)
    same = (r < C) == (c < C)
    causal = same & (r >= c)
    strict = same & (r > c)
    eye = (r == c).astype(F32)
    tri = tri_ref[...]
    gn = gn_ref[...]

    def blk(x):
        return jnp.concatenate([jnp.where(lo_lane, x, 0.0), jnp.where(lo_lane, 0.0, x)], axis=0)

    for b in range(q_ref.shape[0]):
        S = s_ref[b]
        for j in range(cps):
            rows = slice(C * j, C * (j + 1))
            q2, k2, v2 = blk(q_ref[b, rows, :]), blk(k_ref[b, rows, :]), blk(v_ref[b, rows, :])
            b2, g2 = blk(be_ref[b, rows, :]), blk(ge_ref[b, rows, :])
            gc = _dot_exact_x(tri, g2)
            decay = jnp.exp(jnp.where(causal, gc - gc.T, NEG))
            kb = k2 * b2
            L = jnp.where(strict, _mm_nt(kb, k2) * decay, 0.0)
            X = eye - L
            Pw = _mm(L, L)
            for it in range(5):
                X = X + _mm(X, Pw)
                if it < 4:
                    Pw = _mm(Pw, Pw)
            egc = jnp.exp(gc)
            uw = _mm(X, jnp.concatenate([v2 * b2, kb * egc], axis=1))
            u2, w2 = uw[:, :LANES], uw[:, LANES:]
            qk = _mm_nt(q2, k2) * decay
            glast = jnp.concatenate([jnp.broadcast_to(gc[C - 1:C, :], (C, LANES)),
                                     jnp.broadcast_to(gc[P - 1:P, :], (C, LANES))], axis=0)
            kd = k2 * jnp.exp(glast - gc)
            vnew = u2 - _mm(w2, S)
            o2 = _mm(q2 * egc, S) + _mm(qk, vnew)
            S = S * jnp.exp(glast) + _mm_tn(kd, vnew)
            ms = jnp.sum(o2 * o2, axis=-1, keepdims=True) * (1.0 / HEAD_DIM)
            on = o2 * lax.rsqrt(ms + EPS) * gn
            z = z_ref[b, rows, :]
            o_ref[b, rows, :] = (on[:C] + on[C:]) * _silu(z)
        s_ref[b] = S


def _gdn_scan(q, k, v, be, ge, z, gdn_norm, cps=4):
    B, T, _ = q.shape
    C = GDN_CHUNK
    tm = C * cps
    i2 = jnp.arange(2 * C)
    tri = (((i2[:, None] // C) == (i2[None, :] // C)) & (i2[:, None] >= i2[None, :])).astype(BF16)
    gn2 = jnp.tile(gdn_norm, 2).reshape(1, LANES)
    tok = pl.BlockSpec((B, tm, LANES), lambda p, c: (0, c, p))
    return pl.pallas_call(
        functools.partial(_gdn_scan_body, cps=cps),
        grid=(GDN_WIDTH // LANES, T // tm),
        in_specs=[tok] * 6 + [_const_spec(tri.shape), _const_spec((1, LANES))],
        out_specs=tok,
        out_shape=jax.ShapeDtypeStruct((B, T, GDN_WIDTH), F32),
        scratch_shapes=[pltpu.VMEM((B, 2 * C, LANES), F32)],
        compiler_params=_params(("parallel", "arbitrary")),
        name="gdn_scan",
    )(q, k, v, be, ge, z, tri, gn2)


def _nsa_prep_body(q_ref, kv_ref, segq_ref, segk_ref, qn_ref, kn_ref,
                   qT_ref, ks_ref, kw_ref, vs_ref, vw_ref):
    tm = q_ref.shape[1]
    q = q_ref[0]
    ms = _dot_x_exact(q * q, segq_ref[...]) * (1.0 / HEAD_DIM)
    qn = q * lax.rsqrt(ms + EPS) * qn_ref[...] * (HEAD_DIM ** -0.5 * LOG2E)
    qT_ref[0] = qn.T.astype(BF16)
    kv = kv_ref[0]
    segk = segk_ref[...]
    kn = kn_ref[...]

    def knorm(k, g):
        msk = _dot_x_exact(k * k, segk) * (1.0 / HEAD_DIM)
        return k * lax.rsqrt(msk + EPS) * g

    ks = knorm(kv[:, 2 * LANES:3 * LANES], kn[0:1, :])
    lane = lax.broadcasted_iota(jnp.int32, (tm, LANES), 1)
    tok = pl.program_id(1) * tm + lax.broadcasted_iota(jnp.int32, (tm, LANES), 0)
    onehot = (lane - HEAD_DIM == (tok // SEL_BLOCK) % (KEY_TILE // SEL_BLOCK)).astype(F32)
    ks_ref[0, 0] = jnp.where(lane < HEAD_DIM, ks, onehot).astype(BF16)
    ks_ref[0, 1] = jnp.where(lane < HEAD_DIM, pltpu.roll(ks, HEAD_DIM, 1), onehot).astype(BF16)
    kw_ref[0] = knorm(kv[:, 4 * LANES:5 * LANES], kn[1:2, :]).astype(BF16)
    ones_rows = (lax.broadcasted_iota(jnp.int32, (V_ROWS - HEAD_DIM, tm), 0) == 0).astype(BF16)
    for v_ref, cols in ((vs_ref, 3), (vw_ref, 5)):
        vT = kv[:, cols * LANES:(cols + 1) * LANES].T.astype(BF16)
        for h in range(NSA_KV_HEADS):
            v_ref[0, h, :HEAD_DIM, :] = vT[h * HEAD_DIM:(h + 1) * HEAD_DIM, :]
            v_ref[0, h, HEAD_DIM:, :] = ones_rows


def _nsa_prep(qb, kvb, q_norm, k_norm, tm=512):
    B, T, _ = qb.shape
    chq = jnp.arange(NSA_WIDTH) // HEAD_DIM
    segq = (chq[:, None] == chq[None, :]).astype(BF16)
    segk = segq[:LANES, :LANES]
    qn = jnp.tile(q_norm, NSA_HEADS).reshape(1, NSA_WIDTH)
    kn = jnp.tile(k_norm[1:3], (1, NSA_KV_HEADS))
    tok = lambda c: pl.BlockSpec((1, tm, c), lambda b, i: (b, i, 0))
    vspec = pl.BlockSpec((1, NSA_KV_HEADS, V_ROWS, tm), lambda b, i: (b, 0, 0, i))
    vshape = jax.ShapeDtypeStruct((B, NSA_KV_HEADS, V_ROWS, T), BF16)
    return pl.pallas_call(
        _nsa_prep_body,
        grid=(B, T // tm),
        in_specs=[tok(NSA_WIDTH), tok(6 * LANES), _const_spec(segq.shape), _const_spec(segk.shape),
                  _const_spec(qn.shape), _const_spec(kn.shape)],
        out_specs=[pl.BlockSpec((1, NSA_WIDTH, tm), lambda b, i: (b, 0, i)),
                   pl.BlockSpec((1, NSA_KV_HEADS, tm, LANES), lambda b, i: (b, 0, i, 0)),
                   tok(LANES), vspec, vspec],
        out_shape=[jax.ShapeDtypeStruct((B, NSA_WIDTH, T), BF16),
                   jax.ShapeDtypeStruct((B, NSA_KV_HEADS, T, LANES), BF16),
                   jax.ShapeDtypeStruct((B, T, LANES), BF16),
                   vshape, vshape],
        compiler_params=_params(("parallel", "parallel")),
        name="nsa_prep",
    )(qb, kvb, segq, segk, qn, kn)


def _nsa_compress_body(xk_ref, xv_ref, w1k_ref, w1v_ref, w2k_ref, w2vT_ref, posk_ref, posv_ref, kn_ref,
                       kc_ref, vcT_ref):
    half = CMP_STRIDE * HEAD_DIM

    def hidden(x_ref, w1_ref, pos_ref):
        x = x_ref[0, 0]
        nr = x.shape[0]
        ya = jnp.dot(x, w1_ref[:half, :], preferred_element_type=F32)
        yb = jnp.dot(x, w1_ref[half:, :], preferred_element_type=F32)
        bias = jnp.dot(pos_ref[...], w1_ref[...], preferred_element_type=F32)[0:1, :]
        return _silu(ya + pltpu.roll(yb, nr - 1, 0) + bias).astype(BF16)

    hk = hidden(xk_ref, w1k_ref, posk_ref)
    kc = jnp.dot(hk, w2k_ref[...], preferred_element_type=F32)
    kc_ref[0, 0] = _rms(kc, kn_ref[...]).astype(BF16)
    hv = hidden(xv_ref, w1v_ref, posv_ref)
    vcT_ref[0, 0] = lax.dot_general(w2vT_ref[...], hv, (((1,), (1,)), ((), ())),
                                    preferred_element_type=F32).astype(BF16)


def _nsa_compress(kvb, cmp_pos, cmp_w1, cmp_w2, k_norm0):
    B, T, _ = kvb.shape
    nr = T // CMP_STRIDE
    wide = CMP_STRIDE * HEAD_DIM

    def rows16(cols):
        return cols.reshape(B, nr, CMP_STRIDE, NSA_KV_HEADS, HEAD_DIM).transpose(0, 3, 1, 2, 4) \
                   .reshape(B, NSA_KV_HEADS, nr, wide).astype(BF16)

    xk = rows16(kvb[..., :LANES])
    xv = rows16(kvb[..., LANES:2 * LANES])
    pos = jnp.broadcast_to(cmp_pos.reshape(2, 1, CMP_LEN * HEAD_DIM), (2, SUBLANES, CMP_LEN * HEAD_DIM)).astype(BF16)
    w1 = cmp_w1.astype(BF16)
    xspec = pl.BlockSpec((1, 1, nr, wide), lambda b, h: (b, h, 0, 0))
    return pl.pallas_call(
        _nsa_compress_body,
        grid=(B, NSA_KV_HEADS),
        in_specs=[xspec, xspec, _const_spec(w1[0].shape), _const_spec(w1[1].shape),
                  _const_spec((CMP_HIDDEN, HEAD_DIM)), _const_spec((HEAD_DIM, CMP_HIDDEN)),
                  _const_spec(pos[0].shape), _const_spec(pos[1].shape), _const_spec((1, HEAD_DIM))],
        out_specs=[pl.BlockSpec((1, 1, nr, HEAD_DIM), lambda b, h: (b, h, 0, 0)),
                   pl.BlockSpec((1, 1, HEAD_DIM, nr), lambda b, h: (b, h, 0, 0))],
        out_shape=[jax.ShapeDtypeStruct((B, NSA_KV_HEADS, nr, HEAD_DIM), BF16),
                   jax.ShapeDtypeStruct((B, NSA_KV_HEADS, HEAD_DIM, nr), BF16)],
        compiler_params=_params(("parallel", "parallel")),
        name="nsa_compress",
    )(xk, xv, w1[0], w1[1], cmp_w2[0].astype(BF16), cmp_w2[1].T.astype(BF16), pos[0], pos[1],
      k_norm0.reshape(1, HEAD_DIM))


def _nsa_attn_body(qT_ref, kc_ref, vcT_ref, ks_ref, vs_ref, kw_ref, vw_ref, gate_ref, selm_ref,
                   o_ref, selb_ref, gT_ref, qaug_ref, m_ref, acc_ref, sa_ref, sb_ref, oc_ref, ow_ref, *, seq):
    G, Dh, Qb = NSA_GROUP, HEAD_DIM, Q_BLOCK
    h = pl.program_id(1)
    t0 = pl.program_id(2) * Qb
    nr = seq // CMP_STRIDE
    ns = seq // SEL_BLOCK
    pos = t0 + lax.broadcasted_iota(jnp.int32, (1, Qb), 1)

    qT = qT_ref[0]
    q3 = jnp.concatenate([qT[g * Dh:(g + 1) * Dh, :] for g in range(G)], axis=1)
    qaug_ref[...] = jnp.concatenate([q3, jnp.zeros_like(q3)], axis=0)
    row = lax.broadcasted_iota(jnp.int32, (2 * Dh, G * Qb), 0)
    qpad = jnp.where((row < Dh) == (h == 0), jnp.concatenate([q3, q3], axis=0), 0).astype(BF16)

    ws = pl.multiple_of(jnp.maximum(t0 - WINDOW, 0), Qb)
    sw = jnp.dot(kw_ref[0, pl.ds(ws, WIN_KEYS), :], qpad, preferred_element_type=F32)
    vwx = vw_ref[0, 0, :, pl.ds(ws, WIN_KEYS)]
    kpos = ws + lax.broadcasted_iota(jnp.int32, (WIN_KEYS, 1), 0)
    bias_w = jnp.where((kpos <= pos) & (kpos > pos - WINDOW), 0.0, NEG)
    for g in range(G):
        s = sw[:, g * Qb:(g + 1) * Qb] + bias_w
        e = jnp.exp2(s - jnp.max(s, axis=0, keepdims=True)).astype(BF16)
        ow_ref[g] = jnp.dot(vwx, e, preferred_element_type=F32)

    sc = jnp.dot(kc_ref[0, 0], q3, preferred_element_type=F32)
    cmp_end = lax.broadcasted_iota(jnp.int32, (nr, 1), 0) * CMP_STRIDE + (CMP_LEN - 1)
    bias_c = jnp.where(cmp_end <= pos, 0.0, NEG)
    has_c = pos >= CMP_LEN - 1
    vcT = vcT_ref[0, 0]
    psum = jnp.zeros((nr, Qb), F32)
    for g in range(G):
        s = sc[:, g * Qb:(g + 1) * Qb] + bias_c
        e = jnp.exp2(s - jnp.max(s, axis=0, keepdims=True))
        p = e * jnp.where(has_c, 1.0 / jnp.sum(e, axis=0, keepdims=True), 0.0)
        psum = psum + p
        oc_ref[g] = jnp.dot(vcT, p.astype(BF16), preferred_element_type=F32)

    imp = _dot_exact_x(selm_ref[...], psum)
    blk = lax.broadcasted_iota(jnp.int32, (ns, Qb), 0)
    cur = pos // SEL_BLOCK
    forced = (blk == 0) | (blk == cur) | (blk == cur - 1)
    score = jnp.where(forced | (blk > cur), -jnp.inf, imp)
    selb = jnp.where(forced, 0.0, NEG)
    for _ in range(SEL_TOPK - 3):
        best = jnp.max(score, axis=0, keepdims=True)
        first = jnp.min(jnp.where(score == best, blk, ns), axis=0, keepdims=True)
        hit = blk == first
        selb = jnp.where(hit, 0.0, selb)
        score = jnp.where(hit, -jnp.inf, score)
    selb_ref[...] = selb

    bpt = KEY_TILE // SEL_BLOCK
    krow = lax.broadcasted_iota(jnp.int32, (KEY_TILE, 1), 0)
    m_ref[...] = jnp.full(m_ref.shape, NEG, F32)
    acc_ref[...] = jnp.zeros(acc_ref.shape, F32)

    def scores(kt, slot_ref):
        base = pl.multiple_of(kt * KEY_TILE, KEY_TILE)
        sb = selb_ref[pl.ds(pl.multiple_of(kt * bpt, bpt), bpt), :]
        qaug_ref[Dh:Dh + bpt, :] = jnp.concatenate([sb] * G, axis=1).astype(BF16)
        slot_ref[...] = jnp.dot(ks_ref[0, 0, pl.ds(base, KEY_TILE), :], qaug_ref[...],
                                preferred_element_type=F32)

    def consume(kt, slot_ref, diagonal):
        base = pl.multiple_of(kt * KEY_TILE, KEY_TILE)
        vx = vs_ref[0, 0, :, pl.ds(base, KEY_TILE)]
        for g in range(G):
            s = slot_ref[:, g * Qb:(g + 1) * Qb]
            if diagonal:
                s = jnp.where(base + krow <= pos, s, NEG)
            m_old = m_ref[g:g + 1, :]
            m_new = jnp.maximum(m_old, jnp.max(s, axis=0, keepdims=True))
            e = jnp.exp2(s - m_new).astype(BF16)
            acc_ref[g] = jnp.exp2(m_old - m_new) * acc_ref[g] + jnp.dot(vx, e, preferred_element_type=F32)
            m_ref[g:g + 1, :] = m_new

    n_off = (t0 + Qb + KEY_TILE - 1) // KEY_TILE - 1
    scores(0, sa_ref)

    def pair(p, carry):
        scores(2 * p + 1, sb_ref)
        consume(2 * p, sa_ref, False)
        scores(2 * p + 2, sa_ref)
        consume(2 * p + 1, sb_ref, False)
        return carry

    lax.fori_loop(0, n_off // 2, pair, 0)

    @pl.when(n_off % 2 == 1)
    def _():
        scores(n_off, sb_ref)
        consume(n_off - 1, sa_ref, False)
        consume(n_off, sb_ref, True)

    @pl.when(n_off % 2 == 0)
    def _():
        consume(n_off, sa_ref, True)

    gT_ref[...] = _sigmoid(gate_ref[0]).T
    for g in range(G):
        gates = gT_ref[pl.ds((h * G + g) * 3, 3), :]
        a_s, a_w = acc_ref[g], ow_ref[g]
        o_s = a_s[:Dh] * (1.0 / a_s[Dh:Dh + 1])
        o_w = a_w[:Dh] * (1.0 / a_w[Dh:Dh + 1])
        o_ref[0, g * Dh:(g + 1) * Dh, :] = gates[0:1] * oc_ref[g] + gates[1:2] * o_s + gates[2:3] * o_w


def _nsa_attn(qT, kc, vcT, ks, vs, kw, vw, gate):
    B, _, T = qT.shape
    assert T % KEY_TILE == 0 and T >= WIN_KEYS
    nr, ns = T // CMP_STRIDE, T // SEL_BLOCK
    G, Dh, Qb = NSA_GROUP, HEAD_DIM, Q_BLOCK
    j = jnp.arange(ns)[:, None]
    c = jnp.arange(nr)[None, :]
    ratio = SEL_BLOCK // CMP_STRIDE
    selm = ((c >= ratio * j - 1) & (c <= ratio * j + ratio - 1) & (c < nr - 1)).astype(BF16)
    vfull = pl.BlockSpec((1, 1, V_ROWS, T), lambda b, h, i: (b, h, 0, 0))
    return pl.pallas_call(
        functools.partial(_nsa_attn_body, seq=T),
        grid=(B, NSA_KV_HEADS, T // Qb),
        in_specs=[pl.BlockSpec((1, G * Dh, Qb), lambda b, h, i: (b, h, i)),
                  pl.BlockSpec((1, 1, nr, Dh), lambda b, h, i: (b, h, 0, 0)),
                  pl.BlockSpec((1, 1, Dh, nr), lambda b, h, i: (b, h, 0, 0)),
                  pl.BlockSpec((1, 1, T, LANES), lambda b, h, i: (b, h, 0, 0)), vfull,
                  pl.BlockSpec((1, T, LANES), lambda b, h, i: (b, 0, 0)), vfull,
                  pl.BlockSpec((1, Qb, LANES), lambda b, h, i: (b, i, 0)),
                  _const_spec(selm.shape)],
        out_specs=pl.BlockSpec((1, G * Dh, Qb), lambda b, h, i: (b, h, i)),
        out_shape=jax.ShapeDtypeStruct((B, NSA_WIDTH, T), F32),
        scratch_shapes=[pltpu.VMEM((ns, Qb), F32), pltpu.VMEM((LANES, Qb), F32),
                        pltpu.VMEM((2 * Dh, G * Qb), BF16), pltpu.VMEM((SUBLANES, Qb), F32),
                        pltpu.VMEM((G, V_ROWS, Qb), F32),
                        pltpu.VMEM((KEY_TILE, G * Qb), F32), pltpu.VMEM((KEY_TILE, G * Qb), F32),
                        pltpu.VMEM((G, Dh, Qb), F32), pltpu.VMEM((G, V_ROWS, Qb), F32)],
        compiler_params=_params(("parallel", "parallel", "arbitrary")),
        name="nsa_attn",
    )(qT, kc, vcT, ks, vs, kw, vw, gate, selm)


POOL_HALO = 16


def _pool_body(u_ref, halo_ref, w_ref, sc_ref, o_ref):
    i = pl.program_id(1)
    u = u_ref[0]
    tm = u.shape[0]
    halo = jnp.where(i > 0, halo_ref[0], 0.0)
    ue = jnp.concatenate([halo, u], axis=0)
    lane = lax.broadcasted_iota(jnp.int32, (1, POOL_WIDTH), 1)
    grp = lane // POOL_GROUP_DIM
    acc = ue
    wsum = jnp.zeros_like(ue)
    width = jnp.zeros((1, POOL_WIDTH), F32)
    for gi, w in enumerate(POOL_WINDOWS):
        while_shift = w // 2
        acc = acc + pltpu.roll(acc, while_shift, 0)
        wsum = jnp.where(grp == gi, acc, wsum)
        width = jnp.where(grp == gi, float(w), width)
    t1 = (i * tm + 1 + lax.broadcasted_iota(jnp.int32, (tm, 1), 0)).astype(F32)
    mean = wsum[POOL_HALO:] / jnp.minimum(t1, width)
    o_ref[0] = jnp.dot((mean - u).astype(BF16), w_ref[...], preferred_element_type=F32) * sc_ref[...]


def _pool(u, pool_w, pool_scale, tm=512):
    B, T, _ = u.shape
    wblk = jax.scipy.linalg.block_diag(*[pool_w[g] for g in range(len(POOL_WINDOWS))]).astype(BF16)
    tok = pl.BlockSpec((1, tm, POOL_WIDTH), lambda b, i: (b, i, 0))
    hb = tm // POOL_HALO
    return pl.pallas_call(
        _pool_body,
        grid=(B, T // tm),
        in_specs=[tok,
                  pl.BlockSpec((1, POOL_HALO, POOL_WIDTH), lambda b, i: (b, jnp.maximum(i * hb - 1, 0), 0)),
                  _const_spec(wblk.shape), _const_spec((1, POOL_WIDTH))],
        out_specs=tok,
        out_shape=jax.ShapeDtypeStruct((B, T, POOL_WIDTH), F32),
        compiler_params=_params(("parallel", "parallel")),
        name="pool",
    )(u, u, wblk, pool_scale.reshape(1, POOL_WIDTH))


def _outproj_body(x_ref, ya_ref, ybT_ref, yc_ref, wa_ref, wb_ref, wc_ref, o_ref):
    acc = x_ref[0]
    acc = acc + jnp.dot(ya_ref[0].astype(BF16), wa_ref[...], preferred_element_type=F32)
    acc = acc + jnp.dot(ybT_ref[0].T.astype(BF16), wb_ref[...], preferred_element_type=F32)
    acc = acc + jnp.dot(yc_ref[0].astype(BF16), wc_ref[...], preferred_element_type=F32)
    o_ref[0] = acc


def _outproj(x, ya, ybT, yc, w_out, tm=512):
    B, T, _ = x.shape
    w = w_out.astype(BF16)
    wa, wb, wc = w[:GDN_WIDTH], w[GDN_WIDTH:GDN_WIDTH + NSA_WIDTH], w[GDN_WIDTH + NSA_WIDTH:]
    tok = lambda c: pl.BlockSpec((1, tm, c), lambda b, i: (b, i, 0))
    return pl.pallas_call(
        _outproj_body,
        grid=(B, T // tm),
        in_specs=[tok(D_MODEL), tok(GDN_WIDTH), pl.BlockSpec((1, NSA_WIDTH, tm), lambda b, i: (b, 0, i)),
                  tok(POOL_WIDTH), _const_spec(wa.shape), _const_spec(wb.shape), _const_spec(wc.shape)],
        out_specs=tok(D_MODEL),
        out_shape=jax.ShapeDtypeStruct(x.shape, F32),
        compiler_params=_params(("parallel", "parallel")),
        name="outproj",
    )(x, ya, ybT, yc, wa, wb, wc)


FF_TILE = 1024


def _ffn_body(x_ref, g_ref, w1_ref, w2_ref, o_ref):
    x = x_ref[...]
    h = _rms(x, g_ref[...]).astype(BF16)
    acc = x
    for j in range(D_FF // FF_TILE):
        cols = slice(j * FF_TILE, (j + 1) * FF_TILE)
        a = jnp.maximum(jnp.dot(h, w1_ref[:, cols], preferred_element_type=F32), 0.0)
        acc = acc + jnp.dot((a * a).astype(BF16), w2_ref[cols, :], preferred_element_type=F32)
    o_ref[...] = acc


def _ffn(xf, g, w1, w2, tm=512):
    n = xf.shape[0]
    tok = pl.BlockSpec((tm, D_MODEL), lambda i: (i, 0))
    return pl.pallas_call(
        _ffn_body,
        grid=(n // tm,),
        in_specs=[tok, _const_spec((1, D_MODEL)), _const_spec(w1.shape), _const_spec(w2.shape)],
        out_specs=tok,
        out_shape=jax.ShapeDtypeStruct(xf.shape, F32),
        compiler_params=_params(("parallel",)),
        name="ffn",
    )(xf, g.reshape(1, D_MODEL), w1.astype(BF16), w2.astype(BF16))


def _pad_in_weights(w):
    H = GDN_HEADS
    o_ba = 4 * GDN_WIDTH
    o_q = o_ba + 2 * H
    o_gate = o_q + NSA_WIDTH + 6 * LANES
    o_u = o_gate + 3 * NSA_HEADS
    pad = lambda a: jnp.pad(a, ((0, 0), (0, LANES - a.shape[1])))
    return jnp.concatenate([w[:, :o_ba], pad(w[:, o_ba:o_q]), w[:, o_q:o_gate], pad(w[:, o_gate:o_u]),
                            w[:, o_u:]], axis=1).astype(BF16)


def _mixers(qkv, z, ba, qb, kvb, gate, u, conv_w, a_log, dt_bias, gdn_norm, nsa_q_norm, nsa_k_norm,
            cmp_pos, cmp_w1, cmp_w2, pool_w, pool_scale):
    q, k, v, be, ge = _gdn_prep(qkv, ba, conv_w, a_log, dt_bias)
    ya = _gdn_scan(q, k, v, be, ge, z, gdn_norm)
    qT, ks, kw, vs, vw = _nsa_prep(qb, kvb, nsa_q_norm, nsa_k_norm)
    kc, vcT = _nsa_compress(kvb, cmp_pos, cmp_w1, cmp_w2, nsa_k_norm[0])
    ybT = _nsa_attn(qT, kc, vcT, ks, vs, kw, vw, gate)
    yc = _pool(u, pool_w, pool_scale)
    return ya, ybT, yc


def kernel(x, norm_mix, w_in, conv_w, a_log, dt_bias, gdn_norm, nsa_q_norm, nsa_k_norm, cmp_pos, cmp_w1, cmp_w2,
           pool_w, pool_scale, w_out, norm_ffn, w_ffn1, w_ffn2):
    B, T, D = x.shape
    for l in range(w_in.shape[0]):
        proj = _inproj(x.reshape(B * T, D), norm_mix[l], _pad_in_weights(w_in[l]))
        qkv, z, ba, qb, kvb, gate, u = [p.reshape(B, T, -1) for p in proj]
        ya, ybT, yc = _mixers(qkv, z, ba, qb, kvb, gate, u, conv_w[l], a_log[l], dt_bias[l], gdn_norm[l],
                              nsa_q_norm[l], nsa_k_norm[l], cmp_pos[l], cmp_w1[l], cmp_w2[l],
                              pool_w[l], pool_scale[l])
        x = _outproj(x, ya, ybT, yc, w_out[l])
        x = _ffn(x.reshape(B * T, D), norm_ffn[l], w_ffn1[l], w_ffn2[l]).reshape(B, T, D)
    return x
```

```python
import functools

import jax
import jax.numpy as jnp
from jax import lax
from jax.experimental import pallas as pl
from jax.experimental.pallas import tpu as pltpu

F32 = jnp.float32
BF16 = jnp.bfloat16

D_MODEL = 1024
HEAD_DIM = 64
GDN_HEADS = 6
GDN_CHUNK = 64
GDN_WIDTH = GDN_HEADS * HEAD_DIM
NSA_HEADS = 6
NSA_KV_HEADS = 2
NSA_GROUP = NSA_HEADS // NSA_KV_HEADS
NSA_WIDTH = NSA_HEADS * HEAD_DIM
CMP_LEN = 32
CMP_STRIDE = 16
CMP_HIDDEN = 256
SEL_BLOCK = 64
SEL_TOPK = 16
WINDOW = 512
Q_BLOCK = 256
FORCED_SCORE = 1e4
POOL_WINDOWS = (2, 4, 8, 16)
POOL_GROUP_DIM = 64
POOL_WIDTH = 4 * POOL_GROUP_DIM
D_FF = 4 * D_MODEL
EPS = 1e-6

LANES = 128
SUBLANES = 8
VMEM_LIMIT = 56 * 1024 * 1024
NEG = -1e30
LOG2E = 1.4426950408889634
KEY_TILE = 1024
WIN_KEYS = WINDOW + Q_BLOCK
CMP_CHUNK = 256
V_ROWS = 80

IN_GROUPS = (3 * GDN_WIDTH, GDN_WIDTH, LANES, NSA_WIDTH, 6 * LANES, LANES, POOL_WIDTH)


def _sigmoid(x):
    return 1.0 / (1.0 + jnp.exp(-x))


def _silu(x):
    return x * _sigmoid(x)


def _softplus(x):
    return jnp.maximum(x, 0.0) + jnp.log(1.0 + jnp.exp(-jnp.abs(x)))


def _split_bf16(x):
    hi = x.astype(BF16)
    lo = (x - hi.astype(F32)).astype(BF16)
    return hi, lo


def _dot_x_exact(x, m):
    hi, lo = _split_bf16(x)
    return jnp.dot(hi, m, preferred_element_type=F32) + jnp.dot(lo, m, preferred_element_type=F32)


def _dot_exact_x(m, x):
    hi, lo = _split_bf16(x)
    return jnp.dot(m, hi, preferred_element_type=F32) + jnp.dot(m, lo, preferred_element_type=F32)


def _mm(a, b):
    return jnp.dot(a.astype(BF16), b.astype(BF16), preferred_element_type=F32)


def _mm_nt(a, b):
    return lax.dot_general(a.astype(BF16), b.astype(BF16), (((1,), (1,)), ((), ())),
                           preferred_element_type=F32)


def _mm_tn(a, b):
    return lax.dot_general(a.astype(BF16), b.astype(BF16), (((0,), (0,)), ((), ())),
                           preferred_element_type=F32)


def _rms(x, g):
    return x * lax.rsqrt(jnp.mean(x * x, axis=-1, keepdims=True) + EPS) * g


def _const_spec(shape):
    nd = len(shape)
    return pl.BlockSpec(shape, lambda *_: (0,) * nd)


def _params(sem):
    return pltpu.CompilerParams(dimension_semantics=sem, vmem_limit_bytes=VMEM_LIMIT)


def _inproj_body(x_ref, g_ref, w_ref, *out_refs):
    h = _rms(x_ref[...], g_ref[...]).astype(BF16)
    off = 0
    for o_ref in out_refs:
        n = o_ref.shape[-1]
        o_ref[...] = jnp.dot(h, w_ref[:, off:off + n], preferred_element_type=F32)
        off += n


def _inproj(xf, g, w_pad, tm=512):
    n = xf.shape[0]
    d_in = w_pad.shape[1]
    return pl.pallas_call(
        _inproj_body,
        grid=(n // tm,),
        in_specs=[pl.BlockSpec((tm, D_MODEL), lambda i: (i, 0)),
                  _const_spec((1, D_MODEL)),
                  _const_spec((D_MODEL, d_in))],
        out_specs=[pl.BlockSpec((tm, c), lambda i: (i, 0)) for c in IN_GROUPS],
        out_shape=[jax.ShapeDtypeStruct((n, c), F32) for c in IN_GROUPS],
        compiler_params=_params(("parallel",)),
        name="inproj",
    )(xf, g.reshape(1, D_MODEL), w_pad)


def _gdn_prep_body(x_ref, halo_ref, ba_ref, cw_ref, alog_ref, dtb_ref, seg_ref, eb_ref, eg_ref,
                   q_ref, k_ref, v_ref, be_ref, ge_ref):
    i = pl.program_id(1)
    x = x_ref[0]
    halo = jnp.where(i > 0, halo_ref[0], 0.0)
    xe = jnp.concatenate([halo, x], axis=0)
    cw = cw_ref[...]
    y = x * cw[3:4, :]
    for s in (1, 2, 3):
        y = y + pltpu.roll(xe, s, 0)[SUBLANES:] * cw[3 - s:4 - s, :]
    a = _silu(y)
    q = a[:, :GDN_WIDTH]
    k = a[:, GDN_WIDTH:2 * GDN_WIDTH]
    seg = seg_ref[...]
    q_ref[0] = q * lax.rsqrt(_dot_x_exact(q * q, seg) + EPS) * (HEAD_DIM ** -0.5)
    k_ref[0] = k * lax.rsqrt(_dot_x_exact(k * k, seg) + EPS)
    v_ref[0] = a[:, 2 * GDN_WIDTH:]
    ba = ba_ref[0]
    beta = _sigmoid(ba)
    gval = -jnp.exp(alog_ref[...]) * _softplus(ba + dtb_ref[...])
    be_ref[0] = _dot_x_exact(beta, eb_ref[...])
    ge_ref[0] = _dot_x_exact(gval, eg_ref[...])


def _gdn_prep(qkv, ba, conv_w, a_log, dt_bias, tm=256):
    B, T, C = qkv.shape
    H = GDN_HEADS
    alog_p = jnp.zeros((1, LANES), F32).at[0, H:2 * H].set(a_log)
    dtb_p = jnp.zeros((1, LANES), F32).at[0, H:2 * H].set(dt_bias)
    ch = jnp.arange(GDN_WIDTH) // HEAD_DIM
    seg = (ch[:, None] == ch[None, :]).astype(BF16)
    row = jnp.arange(LANES)
    eb = (row[:, None] == ch[None, :]).astype(BF16)
    eg = (row[:, None] - H == ch[None, :]).astype(BF16)
    tok = lambda c: pl.BlockSpec((1, tm, c), lambda b, i: (b, i, 0))
    halo_blocks = tm // SUBLANES
    out = jax.ShapeDtypeStruct((B, T, GDN_WIDTH), F32)
    return pl.pallas_call(
        _gdn_prep_body,
        grid=(B, T // tm),
        in_specs=[tok(C),
                  pl.BlockSpec((1, SUBLANES, C), lambda b, i: (b, jnp.maximum(i * halo_blocks - 1, 0), 0)),
                  tok(LANES),
                  _const_spec(conv_w.shape), _const_spec((1, LANES)), _const_spec((1, LANES)),
                  _const_spec(seg.shape), _const_spec(eb.shape), _const_spec(eg.shape)],
        out_specs=[tok(GDN_WIDTH)] * 5,
        out_shape=[out] * 5,
        compiler_params=_params(("parallel", "parallel")),
        name="gdn_prep",
    )(qkv, qkv, ba, conv_w, alog_p, dtb_p, seg, eb, eg)


def _gdn_scan_body(q_ref, k_ref, v_ref, be_ref, ge_ref, z_ref, tri_ref, gn_ref, o_ref, s_ref, *, cps):
    C = GDN_CHUNK
    P = 2 * C

    B = q_ref.shape[0]
    NP = GDN_WIDTH // LANES

    @pl.when(pl.program_id(0) == 0)
    def _():
        s_ref[...] = jnp.zeros_like(s_ref)

    lo_lane = lax.broadcasted_iota(jnp.int32, (C, LANES), 1) < HEAD_DIM
    r = lax.broadcasted_iota(jnp.int32, (P, P), 0)
    c = lax.broadcasted_iota(jnp.int32, (P, P), 1)
    same = (r < C) == (c < C)
    causal = same & (r >= c)
    strict = same & (r > c)
    eye = (r == c).astype(F32)
    tri = tri_ref[...]
    gn = gn_ref[...]

    def blk(ref, b, j, p):
        x = ref[b, C * j:C * (j + 1), LANES * p:LANES * (p + 1)]
        return jnp.concatenate([jnp.where(lo_lane, x, 0.0), jnp.where(lo_lane, 0.0, x)], axis=0)

    chains = [(b, p) for b in range(B) for p in range(NP)]
    units = [(b, p, j) for j in range(cps) for (b, p) in chains]
    st = {}
    for u in units:
        b, p, j = u
        q2, k2, v2 = blk(q_ref, b, j, p), blk(k_ref, b, j, p), blk(v_ref, b, j, p)
        b2, g2 = blk(be_ref, b, j, p), blk(ge_ref, b, j, p)
        g_hi, g_lo = _split_bf16(g2)
        gcc = jnp.dot(tri, jnp.concatenate([g_hi, g_lo], axis=1), preferred_element_type=F32)
        gc = gcc[:, :LANES] + gcc[:, LANES:]
        decay = jnp.exp(jnp.where(causal, gc - gc.T, NEG))
        kb = k2 * b2
        gram = _mm_nt(jnp.concatenate([kb, q2], axis=0), k2)
        egc = jnp.exp(gc)
        glast = jnp.concatenate([jnp.broadcast_to(gc[C - 1:C, :], (C, LANES)),
                                 jnp.broadcast_to(gc[P - 1:P, :], (C, LANES))], axis=0)
        st[u] = dict(L=jnp.where(strict, gram[:P] * decay, 0.0), qk=gram[P:] * decay,
                     rhs=jnp.concatenate([v2 * b2, kb * egc], axis=1), qd=q2 * egc,
                     kdT=(k2 * jnp.exp(glast - gc)).T, eg=jnp.exp(glast))

    for u in units:
        st[u]["X"] = eye
    s = 1
    while s < C:
        couple = same & ((r // (2 * s)) == (c // (2 * s))) & (((r // s) % 2) == 1) & (((c // s) % 2) == 0)
        for u in units:
            X = st[u]["X"]
            st[u]["X"] = X - _mm(X, _mm(jnp.where(couple, st[u]["L"], 0.0), X))
        s *= 2
    for u in units:
        st[u].pop("L")
        uw = _mm(st[u].pop("X"), st[u].pop("rhs"))
        st[u]["u"], st[u]["w"] = uw[:, :LANES], uw[:, LANES:]

    S = {ch: s_ref[ch[0], ch[1]] for ch in chains}
    for j in range(cps):
        for ch in chains:
            b, p = ch
            d = st[(b, p, j)]
            ws = _mm(jnp.concatenate([d["w"], d["qd"]], axis=0), S[ch])
            vnew = d["u"] - ws[:P]
            ov = _mm(jnp.concatenate([d["qk"], d["kdT"]], axis=0), vnew)
            o2 = ws[P:] + ov[:P]
            S[ch] = S[ch] * d["eg"] + ov[P:]
            ms = jnp.sum(o2 * o2, axis=-1, keepdims=True) * (1.0 / HEAD_DIM)
            on = o2 * lax.rsqrt(ms + EPS) * gn
            z = z_ref[b, C * j:C * (j + 1), LANES * p:LANES * (p + 1)]
            o_ref[b, C * j:C * (j + 1), LANES * p:LANES * (p + 1)] = (on[:C] + on[C:]) * _silu(z)
    for ch in chains:
        s_ref[ch[0], ch[1]] = S[ch]


def _gdn_scan(q, k, v, be, ge, z, gdn_norm, cps=2):
    B, T, _ = q.shape
    C = GDN_CHUNK
    tm = C * cps
    i2 = jnp.arange(2 * C)
    tri = (((i2[:, None] // C) == (i2[None, :] // C)) & (i2[:, None] >= i2[None, :])).astype(BF16)
    gn2 = jnp.tile(gdn_norm, 2).reshape(1, LANES)
    tok = pl.BlockSpec((B, tm, GDN_WIDTH), lambda c: (0, c, 0))
    return pl.pallas_call(
        functools.partial(_gdn_scan_body, cps=cps),
        grid=(T // tm,),
        in_specs=[tok] * 6 + [_const_spec(tri.shape), _const_spec((1, LANES))],
        out_specs=tok,
        out_shape=jax.ShapeDtypeStruct((B, T, GDN_WIDTH), F32),
        scratch_shapes=[pltpu.VMEM((B, GDN_WIDTH // LANES, 2 * C, LANES), F32)],
        compiler_params=_params(("arbitrary",)),
        name="gdn_scan",
    )(q, k, v, be, ge, z, tri, gn2)


def _nsa_prep_body(q_ref, kv_ref, segq_ref, segk_ref, qn_ref, kn_ref,
                   qT_ref, ks_ref, kw_ref, vs_ref, vw_ref):
    tm = q_ref.shape[1]
    q = q_ref[0]
    ms = _dot_x_exact(q * q, segq_ref[...]) * (1.0 / HEAD_DIM)
    qn = q * lax.rsqrt(ms + EPS) * qn_ref[...] * (HEAD_DIM ** -0.5 * LOG2E)
    qT_ref[0] = qn.T.astype(BF16)
    kv = kv_ref[0]
    segk = segk_ref[...]
    kn = kn_ref[...]

    def knorm(k, g):
        msk = _dot_x_exact(k * k, segk) * (1.0 / HEAD_DIM)
        return k * lax.rsqrt(msk + EPS) * g

    ks = knorm(kv[:, 2 * LANES:3 * LANES], kn[0:1, :])
    lane = lax.broadcasted_iota(jnp.int32, (tm, LANES), 1)
    tok = pl.program_id(1) * tm + lax.broadcasted_iota(jnp.int32, (tm, LANES), 0)
    onehot = (lane - HEAD_DIM == (tok // SEL_BLOCK) % (KEY_TILE // SEL_BLOCK)).astype(F32)
    ks_ref[0, 0] = jnp.where(lane < HEAD_DIM, ks, onehot).astype(BF16)
    ks_ref[0, 1] = jnp.where(lane < HEAD_DIM, pltpu.roll(ks, HEAD_DIM, 1), onehot).astype(BF16)
    kw_ref[0] = knorm(kv[:, 4 * LANES:5 * LANES], kn[1:2, :]).astype(BF16)
    ones_rows = (lax.broadcasted_iota(jnp.int32, (V_ROWS - HEAD_DIM, tm), 0) == 0).astype(BF16)
    for v_ref, cols in ((vs_ref, 3), (vw_ref, 5)):
        vT = kv[:, cols * LANES:(cols + 1) * LANES].T.astype(BF16)
        for h in range(NSA_KV_HEADS):
            v_ref[0, h, :HEAD_DIM, :] = vT[h * HEAD_DIM:(h + 1) * HEAD_DIM, :]
            v_ref[0, h, HEAD_DIM:, :] = ones_rows


def _nsa_prep(qb, kvb, q_norm, k_norm, tm=512):
    B, T, _ = qb.shape
    chq = jnp.arange(NSA_WIDTH) // HEAD_DIM
    segq = (chq[:, None] == chq[None, :]).astype(BF16)
    segk = segq[:LANES, :LANES]
    qn = jnp.tile(q_norm, NSA_HEADS).reshape(1, NSA_WIDTH)
    kn = jnp.tile(k_norm[1:3], (1, NSA_KV_HEADS))
    tok = lambda c: pl.BlockSpec((1, tm, c), lambda b, i: (b, i, 0))
    vspec = pl.BlockSpec((1, NSA_KV_HEADS, V_ROWS, tm), lambda b, i: (b, 0, 0, i))
    vshape = jax.ShapeDtypeStruct((B, NSA_KV_HEADS, V_ROWS, T), BF16)
    return pl.pallas_call(
        _nsa_prep_body,
        grid=(B, T // tm),
        in_specs=[tok(NSA_WIDTH), tok(6 * LANES), _const_spec(segq.shape), _const_spec(segk.shape),
                  _const_spec(qn.shape), _const_spec(kn.shape)],
        out_specs=[pl.BlockSpec((1, NSA_WIDTH, tm), lambda b, i: (b, 0, i)),
                   pl.BlockSpec((1, NSA_KV_HEADS, tm, LANES), lambda b, i: (b, 0, i, 0)),
                   tok(LANES), vspec, vspec],
        out_shape=[jax.ShapeDtypeStruct((B, NSA_WIDTH, T), BF16),
                   jax.ShapeDtypeStruct((B, NSA_KV_HEADS, T, LANES), BF16),
                   jax.ShapeDtypeStruct((B, T, LANES), BF16),
                   vshape, vshape],
        compiler_params=_params(("parallel", "parallel")),
        name="nsa_prep",
    )(qb, kvb, segq, segk, qn, kn)


def _nsa_compress_body(xk_ref, xv_ref, w1k_ref, w1v_ref, w2k_ref, w2vT_ref, posk_ref, posv_ref, kn_ref,
                       kc_ref, vcT_ref):
    half = CMP_STRIDE * HEAD_DIM

    def hidden(x_ref, w1_ref, pos_ref):
        x = x_ref[0, 0]
        nr = x.shape[0]
        ya = jnp.dot(x, w1_ref[:half, :], preferred_element_type=F32)
        yb = jnp.dot(x, w1_ref[half:, :], preferred_element_type=F32)
        bias = jnp.dot(pos_ref[...], w1_ref[...], preferred_element_type=F32)[0:1, :]
        return _silu(ya + pltpu.roll(yb, nr - 1, 0) + bias).astype(BF16)

    hk = hidden(xk_ref, w1k_ref, posk_ref)
    kc = jnp.dot(hk, w2k_ref[...], preferred_element_type=F32)
    kc_ref[0, 0] = _rms(kc, kn_ref[...]).astype(BF16)
    hv = hidden(xv_ref, w1v_ref, posv_ref)
    vcT_ref[0, 0] = lax.dot_general(w2vT_ref[...], hv, (((1,), (1,)), ((), ())),
                                    preferred_element_type=F32).astype(BF16)


def _nsa_compress(kvb, cmp_pos, cmp_w1, cmp_w2, k_norm0):
    B, T, _ = kvb.shape
    nr = T // CMP_STRIDE
    wide = CMP_STRIDE * HEAD_DIM

    def rows16(cols):
        return cols.reshape(B, nr, CMP_STRIDE, NSA_KV_HEADS, HEAD_DIM).transpose(0, 3, 1, 2, 4) \
                   .reshape(B, NSA_KV_HEADS, nr, wide).astype(BF16)

    xk = rows16(kvb[..., :LANES])
    xv = rows16(kvb[..., LANES:2 * LANES])
    pos = jnp.broadcast_to(cmp_pos.reshape(2, 1, CMP_LEN * HEAD_DIM), (2, SUBLANES, CMP_LEN * HEAD_DIM)).astype(BF16)
    w1 = cmp_w1.astype(BF16)
    xspec = pl.BlockSpec((1, 1, nr, wide), lambda b, h: (b, h, 0, 0))
    return pl.pallas_call(
        _nsa_compress_body,
        grid=(B, NSA_KV_HEADS),
        in_specs=[xspec, xspec, _const_spec(w1[0].shape), _const_spec(w1[1].shape),
                  _const_spec((CMP_HIDDEN, HEAD_DIM)), _const_spec((HEAD_DIM, CMP_HIDDEN)),
                  _const_spec(pos[0].shape), _const_spec(pos[1].shape), _const_spec((1, HEAD_DIM))],
        out_specs=[pl.BlockSpec((1, 1, nr, HEAD_DIM), lambda b, h: (b, h, 0, 0)),
                   pl.BlockSpec((1, 1, HEAD_DIM, nr), lambda b, h: (b, h, 0, 0))],
        out_shape=[jax.ShapeDtypeStruct((B, NSA_KV_HEADS, nr, HEAD_DIM), BF16),
                   jax.ShapeDtypeStruct((B, NSA_KV_HEADS, HEAD_DIM, nr), BF16)],
        compiler_params=_params(("parallel", "parallel")),
        name="nsa_compress",
    )(xk, xv, w1[0], w1[1], cmp_w2[0].astype(BF16), cmp_w2[1].T.astype(BF16), pos[0], pos[1],
      k_norm0.reshape(1, HEAD_DIM))


def _nsa_attn_body(qT_ref, kc_ref, vcT_ref, ks_ref, vs_ref, kw_ref, vw_ref, gate_ref, selm_ref,
                   o_ref, selb_ref, gT_ref, qaug_ref, m_ref, acc_ref, sa_ref, sb_ref, oc_ref, ow_ref, *, seq):
    G, Dh, Qb = NSA_GROUP, HEAD_DIM, Q_BLOCK
    h = pl.program_id(1)
    t0 = pl.program_id(2) * Qb
    nr = seq // CMP_STRIDE
    ns = seq // SEL_BLOCK
    pos = t0 + lax.broadcasted_iota(jnp.int32, (1, Qb), 1)

    qT = qT_ref[0]
    q3 = jnp.concatenate([qT[g * Dh:(g + 1) * Dh, :] for g in range(G)], axis=1)
    qaug_ref[...] = jnp.concatenate([q3, jnp.zeros_like(q3)], axis=0)
    row = lax.broadcasted_iota(jnp.int32, (2 * Dh, G * Qb), 0)
    qpad = jnp.where((row < Dh) == (h == 0), jnp.concatenate([q3, q3], axis=0), 0).astype(BF16)

    ws = pl.multiple_of(jnp.maximum(t0 - WINDOW, 0), Qb)
    sw = jnp.dot(kw_ref[0, pl.ds(ws, WIN_KEYS), :], qpad, preferred_element_type=F32)
    vwx = vw_ref[0, 0, :, pl.ds(ws, WIN_KEYS)]
    kpos = ws + lax.broadcasted_iota(jnp.int32, (WIN_KEYS, 1), 0)
    bias_w = jnp.where((kpos <= pos) & (kpos > pos - WINDOW), 0.0, NEG)
    for g in range(G):
        s = sw[:, g * Qb:(g + 1) * Qb] + bias_w
        e = jnp.exp2(s - jnp.max(s, axis=0, keepdims=True)).astype(BF16)
        ow_ref[g] = jnp.dot(vwx, e, preferred_element_type=F32)

    def compressed_and_select(n_c):
        n_s = n_c * CMP_STRIDE // SEL_BLOCK
        sc = jnp.dot(kc_ref[0, 0, :n_c, :], q3, preferred_element_type=F32)
        cmp_end = lax.broadcasted_iota(jnp.int32, (n_c, 1), 0) * CMP_STRIDE + (CMP_LEN - 1)
        bias_c = jnp.where(cmp_end <= pos, 0.0, NEG)
        has_c = pos >= CMP_LEN - 1
        vcT = vcT_ref[0, 0, :, :n_c]
        psum = jnp.zeros((n_c, Qb), F32)
        for g in range(G):
            s = sc[:, g * Qb:(g + 1) * Qb] + bias_c
            e = jnp.exp2(s - jnp.max(s, axis=0, keepdims=True))
            p = e * jnp.where(has_c, 1.0 / jnp.sum(e, axis=0, keepdims=True), 0.0)
            psum = psum + p
            oc_ref[g] = jnp.dot(vcT, p.astype(BF16), preferred_element_type=F32)

        imp = _dot_exact_x(selm_ref[:n_s, :n_c], psum)
        blk = lax.broadcasted_iota(jnp.int32, (n_s, Qb), 0)
        cur = pos // SEL_BLOCK
        forced = (blk == 0) | (blk == cur) | (blk == cur - 1)
        score = jnp.where(forced | (blk > cur), -jnp.inf, imp)
        for _ in range(SEL_TOPK - 3):
            best = jnp.max(score, axis=0, keepdims=True)
            first = jnp.min(jnp.where(score == best, blk, n_s), axis=0, keepdims=True)
            score = jnp.where(blk == first, -jnp.inf, score)
        selb_ref[:n_s, :] = jnp.where((score == -jnp.inf) & (blk <= cur), 0.0, NEG)

    chunk = min(CMP_CHUNK, nr)
    need = (t0 + Qb) // CMP_STRIDE
    for k in range(1, nr // chunk + 1):
        pl.when((need + chunk - 1) // chunk == k)(functools.partial(compressed_and_select, k * chunk))

    bpt = KEY_TILE // SEL_BLOCK
    krow = lax.broadcasted_iota(jnp.int32, (KEY_TILE, 1), 0)
    m_ref[...] = jnp.full(m_ref.shape, NEG, F32)
    acc_ref[...] = jnp.zeros(acc_ref.shape, F32)

    def scores(kt, slot_ref):
        base = pl.multiple_of(kt * KEY_TILE, KEY_TILE)
        sb = selb_ref[pl.ds(pl.multiple_of(kt * bpt, bpt), bpt), :]
        qaug_ref[Dh:Dh + bpt, :] = jnp.concatenate([sb] * G, axis=1).astype(BF16)
        slot_ref[...] = jnp.dot(ks_ref[0, 0, pl.ds(base, KEY_TILE), :], qaug_ref[...],
                                preferred_element_type=F32)

    def consume(kt, slot_ref, diagonal):
        base = pl.multiple_of(kt * KEY_TILE, KEY_TILE)
        vx = vs_ref[0, 0, :, pl.ds(base, KEY_TILE)]
        for g in range(G):
            s = slot_ref[:, g * Qb:(g + 1) * Qb]
            if diagonal:
                s = jnp.where(base + krow <= pos, s, NEG)
            m_old = m_ref[g:g + 1, :]
            m_new = jnp.maximum(m_old, jnp.max(s, axis=0, keepdims=True))
            e = jnp.exp2(s - m_new).astype(BF16)
            acc_ref[g] = jnp.exp2(m_old - m_new) * acc_ref[g] + jnp.dot(vx, e, preferred_element_type=F32)
            m_ref[g:g + 1, :] = m_new

    n_off = (t0 + Qb + KEY_TILE - 1) // KEY_TILE - 1
    scores(0, sa_ref)

    def pair(p, carry):
        scores(2 * p + 1, sb_ref)
        consume(2 * p, sa_ref, False)
        scores(2 * p + 2, sa_ref)
        consume(2 * p + 1, sb_ref, False)
        return carry

    lax.fori_loop(0, n_off // 2, pair, 0)

    @pl.when(n_off % 2 == 1)
    def _():
        scores(n_off, sb_ref)
        consume(n_off - 1, sa_ref, False)
        consume(n_off, sb_ref, True)

    @pl.when(n_off % 2 == 0)
    def _():
        consume(n_off, sa_ref, True)

    gT_ref[...] = _sigmoid(gate_ref[0]).T
    gT = gT_ref[0:3 * SUBLANES, :]
    for g in range(G):
        r0 = g * 3
        gates = jnp.where(h == 0, gT[r0:r0 + 3], gT[G * 3 + r0:G * 3 + r0 + 3])
        a_s, a_w = acc_ref[g], ow_ref[g]
        o_s = a_s[:Dh] * (1.0 / a_s[Dh:Dh + 1])
        o_w = a_w[:Dh] * (1.0 / a_w[Dh:Dh + 1])
        o_ref[0, g * Dh:(g + 1) * Dh, :] = gates[0:1] * oc_ref[g] + gates[1:2] * o_s + gates[2:3] * o_w


def _nsa_attn(qT, kc, vcT, ks, vs, kw, vw, gate):
    B, _, T = qT.shape
    assert T % KEY_TILE == 0 and T >= WIN_KEYS
    nr, ns = T // CMP_STRIDE, T // SEL_BLOCK
    G, Dh, Qb = NSA_GROUP, HEAD_DIM, Q_BLOCK
    j = jnp.arange(ns)[:, None]
    c = jnp.arange(nr)[None, :]
    ratio = SEL_BLOCK // CMP_STRIDE
    selm = ((c >= ratio * j - 1) & (c <= ratio * j + ratio - 1) & (c < nr - 1)).astype(BF16)
    vfull = pl.BlockSpec((1, 1, V_ROWS, T), lambda b, h, i: (b, h, 0, 0))
    return pl.pallas_call(
        functools.partial(_nsa_attn_body, seq=T),
        grid=(B, NSA_KV_HEADS, T // Qb),
        in_specs=[pl.BlockSpec((1, G * Dh, Qb), lambda b, h, i: (b, h, i)),
                  pl.BlockSpec((1, 1, nr, Dh), lambda b, h, i: (b, h, 0, 0)),
                  pl.BlockSpec((1, 1, Dh, nr), lambda b, h, i: (b, h, 0, 0)),
                  pl.BlockSpec((1, 1, T, LANES), lambda b, h, i: (b, h, 0, 0)), vfull,
                  pl.BlockSpec((1, T, LANES), lambda b, h, i: (b, 0, 0)), vfull,
                  pl.BlockSpec((1, Qb, LANES), lambda b, h, i: (b, i, 0)),
                  _const_spec(selm.shape)],
        out_specs=pl.BlockSpec((1, G * Dh, Qb), lambda b, h, i: (b, h, i)),
        out_shape=jax.ShapeDtypeStruct((B, NSA_WIDTH, T), F32),
        scratch_shapes=[pltpu.VMEM((ns, Qb), F32), pltpu.VMEM((LANES, Qb), F32),
                        pltpu.VMEM((2 * Dh, G * Qb), BF16), pltpu.VMEM((SUBLANES, Qb), F32),
                        pltpu.VMEM((G, V_ROWS, Qb), F32),
                        pltpu.VMEM((KEY_TILE, G * Qb), F32), pltpu.VMEM((KEY_TILE, G * Qb), F32),
                        pltpu.VMEM((G, Dh, Qb), F32), pltpu.VMEM((G, V_ROWS, Qb), F32)],
        compiler_params=_params(("parallel", "parallel", "arbitrary")),
        name="nsa_attn",
    )(qT, kc, vcT, ks, vs, kw, vw, gate, selm)


POOL_HALO = 16


def _pool_body(u_ref, halo_ref, w_ref, sc_ref, o_ref):
    i = pl.program_id(1)
    u = u_ref[0]
    tm = u.shape[0]
    halo = jnp.where(i > 0, halo_ref[0], 0.0)
    ue = jnp.concatenate([halo, u], axis=0)
    lane = lax.broadcasted_iota(jnp.int32, (1, POOL_WIDTH), 1)
    grp = lane // POOL_GROUP_DIM
    acc = ue
    wsum = jnp.zeros_like(ue)
    width = jnp.zeros((1, POOL_WIDTH), F32)
    for gi, w in enumerate(POOL_WINDOWS):
        while_shift = w // 2
        acc = acc + pltpu.roll(acc, while_shift, 0)
        wsum = jnp.where(grp == gi, acc, wsum)
        width = jnp.where(grp == gi, float(w), width)
    t1 = (i * tm + 1 + lax.broadcasted_iota(jnp.int32, (tm, 1), 0)).astype(F32)
    mean = wsum[POOL_HALO:] / jnp.minimum(t1, width)
    o_ref[0] = jnp.dot((mean - u).astype(BF16), w_ref[...], preferred_element_type=F32) * sc_ref[...]


def _pool(u, pool_w, pool_scale, tm=512):
    B, T, _ = u.shape
    wblk = jax.scipy.linalg.block_diag(*[pool_w[g] for g in range(len(POOL_WINDOWS))]).astype(BF16)
    tok = pl.BlockSpec((1, tm, POOL_WIDTH), lambda b, i: (b, i, 0))
    hb = tm // POOL_HALO
    return pl.pallas_call(
        _pool_body,
        grid=(B, T // tm),
        in_specs=[tok,
                  pl.BlockSpec((1, POOL_HALO, POOL_WIDTH), lambda b, i: (b, jnp.maximum(i * hb - 1, 0), 0)),
                  _const_spec(wblk.shape), _const_spec((1, POOL_WIDTH))],
        out_specs=tok,
        out_shape=jax.ShapeDtypeStruct((B, T, POOL_WIDTH), F32),
        compiler_params=_params(("parallel", "parallel")),
        name="pool",
    )(u, u, wblk, pool_scale.reshape(1, POOL_WIDTH))


def _outproj_body(x_ref, ya_ref, ybT_ref, yc_ref, wa_ref, wb_ref, wc_ref, o_ref):
    acc = x_ref[0]
    acc = acc + jnp.dot(ya_ref[0].astype(BF16), wa_ref[...], preferred_element_type=F32)
    acc = acc + jnp.dot(ybT_ref[0].T.astype(BF16), wb_ref[...], preferred_element_type=F32)
    acc = acc + jnp.dot(yc_ref[0].astype(BF16), wc_ref[...], preferred_element_type=F32)
    o_ref[0] = acc


def _outproj(x, ya, ybT, yc, w_out, tm=512):
    B, T, _ = x.shape
    w = w_out.astype(BF16)
    wa, wb, wc = w[:GDN_WIDTH], w[GDN_WIDTH:GDN_WIDTH + NSA_WIDTH], w[GDN_WIDTH + NSA_WIDTH:]
    tok = lambda c: pl.BlockSpec((1, tm, c), lambda b, i: (b, i, 0))
    return pl.pallas_call(
        _outproj_body,
        grid=(B, T // tm),
        in_specs=[tok(D_MODEL), tok(GDN_WIDTH), pl.BlockSpec((1, NSA_WIDTH, tm), lambda b, i: (b, 0, i)),
                  tok(POOL_WIDTH), _const_spec(wa.shape), _const_spec(wb.shape), _const_spec(wc.shape)],
        out_specs=tok(D_MODEL),
        out_shape=jax.ShapeDtypeStruct(x.shape, F32),
        compiler_params=_params(("parallel", "parallel")),
        name="outproj",
    )(x, ya, ybT, yc, wa, wb, wc)


FF_TILE = 1024


def _ffn_body(x_ref, g_ref, w1_ref, w2_ref, o_ref):
    x = x_ref[...]
    h = _rms(x, g_ref[...]).astype(BF16)
    acc = x
    for j in range(D_FF // FF_TILE):
        cols = slice(j * FF_TILE, (j + 1) * FF_TILE)
        a = jnp.maximum(jnp.dot(h, w1_ref[:, cols], preferred_element_type=F32), 0.0)
        acc = acc + jnp.dot((a * a).astype(BF16), w2_ref[cols, :], preferred_element_type=F32)
    o_ref[...] = acc


def _ffn(xf, g, w1, w2, tm=512):
    n = xf.shape[0]
    tok = pl.BlockSpec((tm, D_MODEL), lambda i: (i, 0))
    return pl.pallas_call(
        _ffn_body,
        grid=(n // tm,),
        in_specs=[tok, _const_spec((1, D_MODEL)), _const_spec(w1.shape), _const_spec(w2.shape)],
        out_specs=tok,
        out_shape=jax.ShapeDtypeStruct(xf.shape, F32),
        compiler_params=_params(("parallel",)),
        name="ffn",
    )(xf, g.reshape(1, D_MODEL), w1.astype(BF16), w2.astype(BF16))


def _pad_in_weights(w):
    H = GDN_HEADS
    o_ba = 4 * GDN_WIDTH
    o_q = o_ba + 2 * H
    o_gate = o_q + NSA_WIDTH + 6 * LANES
    o_u = o_gate + 3 * NSA_HEADS
    pad = lambda a: jnp.pad(a, ((0, 0), (0, LANES - a.shape[1])))
    return jnp.concatenate([w[:, :o_ba], pad(w[:, o_ba:o_q]), w[:, o_q:o_gate], pad(w[:, o_gate:o_u]),
                            w[:, o_u:]], axis=1).astype(BF16)


def _mixers(qkv, z, ba, qb, kvb, gate, u, conv_w, a_log, dt_bias, gdn_norm, nsa_q_norm, nsa_k_norm,
            cmp_pos, cmp_w1, cmp_w2, pool_w, pool_scale):
    q, k, v, be, ge = _gdn_prep(qkv, ba, conv_w, a_log, dt_bias)
    ya = _gdn_scan(q, k, v, be, ge, z, gdn_norm)
    qT, ks, kw, vs, vw = _nsa_prep(qb, kvb, nsa_q_norm, nsa_k_norm)
    kc, vcT = _nsa_compress(kvb, cmp_pos, cmp_w1, cmp_w2, nsa_k_norm[0])
    ybT = _nsa_attn(qT, kc, vcT, ks, vs, kw, vw, gate)
    yc = _pool(u, pool_w, pool_scale)
    return ya, ybT, yc


def kernel(x, norm_mix, w_in, conv_w, a_log, dt_bias, gdn_norm, nsa_q_norm, nsa_k_norm, cmp_pos, cmp_w1, cmp_w2,
           pool_w, pool_scale, w_out, norm_ffn, w_ffn1, w_ffn2):
    B, T, D = x.shape
    for l in range(w_in.shape[0]):
        proj = _inproj(x.reshape(B * T, D), norm_mix[l], _pad_in_weights(w_in[l]))
        qkv, z, ba, qb, kvb, gate, u = [p.reshape(B, T, -1) for p in proj]
        ya, ybT, yc = _mixers(qkv, z, ba, qb, kvb, gate, u, conv_w[l], a_log[l], dt_bias[l], gdn_norm[l],
                              nsa_q_norm[l], nsa_k_norm[l], cmp_pos[l], cmp_w1[l], cmp_w2[l],
                              pool_w[l], pool_scale[l])
        x = _outproj(x, ya, ybT, yc, w_out[l])
        x = _ffn(x.reshape(B * T, D), norm_ffn[l], w_ffn1[l], w_ffn2[l]).reshape(B, T, D)
    return x
```

```python
import functools

import jax
import jax.numpy as jnp
from jax import lax
from jax.experimental import pallas as pl
from jax.experimental.pallas import tpu as pltpu

F32 = jnp.float32
BF16 = jnp.bfloat16

D_MODEL = 1024
HEAD_DIM = 64
GDN_HEADS = 6
GDN_CHUNK = 64
GDN_WIDTH = GDN_HEADS * HEAD_DIM
NSA_HEADS = 6
NSA_KV_HEADS = 2
NSA_GROUP = NSA_HEADS // NSA_KV_HEADS
NSA_WIDTH = NSA_HEADS * HEAD_DIM
CMP_LEN = 32
CMP_STRIDE = 16
CMP_HIDDEN = 256
SEL_BLOCK = 64
SEL_TOPK = 16
WINDOW = 512
Q_BLOCK = 256
FORCED_SCORE = 1e4
POOL_WINDOWS = (2, 4, 8, 16)
POOL_GROUP_DIM = 64
POOL_WIDTH = 4 * POOL_GROUP_DIM
D_FF = 4 * D_MODEL
EPS = 1e-6

LANES = 128
SUBLANES = 8
VMEM_LIMIT = 56 * 1024 * 1024
NEG = -1e30
LOG2E = 1.4426950408889634
KEY_TILE = 1024
WIN_KEYS = WINDOW + Q_BLOCK
CMP_CHUNK = 256
V_ROWS = 80

IN_GROUPS = (3 * GDN_WIDTH, GDN_WIDTH, LANES, NSA_WIDTH, 6 * LANES, LANES, POOL_WIDTH)


def _sigmoid(x):
    return 1.0 / (1.0 + jnp.exp(-x))


def _silu(x):
    return x * _sigmoid(x)


def _softplus(x):
    return jnp.maximum(x, 0.0) + jnp.log(1.0 + jnp.exp(-jnp.abs(x)))


def _split_bf16(x):
    hi = x.astype(BF16)
    lo = (x - hi.astype(F32)).astype(BF16)
    return hi, lo


def _dot_x_exact(x, m):
    hi, lo = _split_bf16(x)
    return jnp.dot(hi, m, preferred_element_type=F32) + jnp.dot(lo, m, preferred_element_type=F32)


def _dot_exact_x(m, x):
    hi, lo = _split_bf16(x)
    return jnp.dot(m, hi, preferred_element_type=F32) + jnp.dot(m, lo, preferred_element_type=F32)


def _mm(a, b):
    return jnp.dot(a.astype(BF16), b.astype(BF16), preferred_element_type=F32)


def _mm_nt(a, b):
    return lax.dot_general(a.astype(BF16), b.astype(BF16), (((1,), (1,)), ((), ())),
                           preferred_element_type=F32)


def _mm_tn(a, b):
    return lax.dot_general(a.astype(BF16), b.astype(BF16), (((0,), (0,)), ((), ())),
                           preferred_element_type=F32)


def _rms(x, g):
    return x * lax.rsqrt(jnp.mean(x * x, axis=-1, keepdims=True) + EPS) * g


def _const_spec(shape):
    nd = len(shape)
    return pl.BlockSpec(shape, lambda *_: (0,) * nd)


def _params(sem):
    return pltpu.CompilerParams(dimension_semantics=sem, vmem_limit_bytes=VMEM_LIMIT)


def _inproj_body(x_ref, g_ref, w_ref, *out_refs):
    h = _rms(x_ref[...], g_ref[...]).astype(BF16)
    off = 0
    for o_ref in out_refs:
        n = o_ref.shape[-1]
        o_ref[...] = jnp.dot(h, w_ref[:, off:off + n], preferred_element_type=F32)
        off += n


def _inproj(xf, g, w_pad, tm=512):
    n = xf.shape[0]
    d_in = w_pad.shape[1]
    return pl.pallas_call(
        _inproj_body,
        grid=(n // tm,),
        in_specs=[pl.BlockSpec((tm, D_MODEL), lambda i: (i, 0)),
                  _const_spec((1, D_MODEL)),
                  _const_spec((D_MODEL, d_in))],
        out_specs=[pl.BlockSpec((tm, c), lambda i: (i, 0)) for c in IN_GROUPS],
        out_shape=[jax.ShapeDtypeStruct((n, c), F32) for c in IN_GROUPS],
        compiler_params=_params(("parallel",)),
        name="inproj",
    )(xf, g.reshape(1, D_MODEL), w_pad)


def _gdn_prep_body(x_ref, halo_ref, ba_ref, cw_ref, alog_ref, dtb_ref, seg_ref, eb_ref, eg_ref,
                   q_ref, k_ref, v_ref, be_ref, ge_ref):
    i = pl.program_id(1)
    x = x_ref[0]
    halo = jnp.where(i > 0, halo_ref[0], 0.0)
    xe = jnp.concatenate([halo, x], axis=0)
    cw = cw_ref[...]
    y = x * cw[3:4, :]
    for s in (1, 2, 3):
        y = y + pltpu.roll(xe, s, 0)[SUBLANES:] * cw[3 - s:4 - s, :]
    a = _silu(y)
    q = a[:, :GDN_WIDTH]
    k = a[:, GDN_WIDTH:2 * GDN_WIDTH]
    seg = seg_ref[...]
    q_ref[0] = q * lax.rsqrt(_dot_x_exact(q * q, seg) + EPS) * (HEAD_DIM ** -0.5)
    k_ref[0] = k * lax.rsqrt(_dot_x_exact(k * k, seg) + EPS)
    v_ref[0] = a[:, 2 * GDN_WIDTH:]
    ba = ba_ref[0]
    beta = _sigmoid(ba)
    gval = -jnp.exp(alog_ref[...]) * _softplus(ba + dtb_ref[...])
    be_ref[0] = _dot_x_exact(beta, eb_ref[...])
    ge_ref[0] = _dot_x_exact(gval, eg_ref[...])


def _gdn_prep(qkv, ba, conv_w, a_log, dt_bias, tm=256):
    B, T, C = qkv.shape
    H = GDN_HEADS
    alog_p = jnp.zeros((1, LANES), F32).at[0, H:2 * H].set(a_log)
    dtb_p = jnp.zeros((1, LANES), F32).at[0, H:2 * H].set(dt_bias)
    ch = jnp.arange(GDN_WIDTH) // HEAD_DIM
    seg = (ch[:, None] == ch[None, :]).astype(BF16)
    row = jnp.arange(LANES)
    eb = (row[:, None] == ch[None, :]).astype(BF16)
    eg = (row[:, None] - H == ch[None, :]).astype(BF16)
    tok = lambda c: pl.BlockSpec((1, tm, c), lambda b, i: (b, i, 0))
    halo_blocks = tm // SUBLANES
    out = jax.ShapeDtypeStruct((B, T, GDN_WIDTH), F32)
    return pl.pallas_call(
        _gdn_prep_body,
        grid=(B, T // tm),
        in_specs=[tok(C),
                  pl.BlockSpec((1, SUBLANES, C), lambda b, i: (b, jnp.maximum(i * halo_blocks - 1, 0), 0)),
                  tok(LANES),
                  _const_spec(conv_w.shape), _const_spec((1, LANES)), _const_spec((1, LANES)),
                  _const_spec(seg.shape), _const_spec(eb.shape), _const_spec(eg.shape)],
        out_specs=[tok(GDN_WIDTH)] * 5,
        out_shape=[out] * 5,
        compiler_params=_params(("parallel", "parallel")),
        name="gdn_prep",
    )(qkv, qkv, ba, conv_w, alog_p, dtb_p, seg, eb, eg)


def _gdn_scan_body(q_ref, k_ref, v_ref, be_ref, ge_ref, z_ref, tri_ref, gn_ref, o_ref, s_ref, *, cps):
    C = GDN_CHUNK
    P = 2 * C

    B = q_ref.shape[0]
    NP = GDN_WIDTH // LANES

    @pl.when(pl.program_id(0) == 0)
    def _():
        s_ref[...] = jnp.zeros_like(s_ref)

    lo_lane = lax.broadcasted_iota(jnp.int32, (C, LANES), 1) < HEAD_DIM
    r = lax.broadcasted_iota(jnp.int32, (P, P), 0)
    c = lax.broadcasted_iota(jnp.int32, (P, P), 1)
    same = (r < C) == (c < C)
    causal = same & (r >= c)
    strict = same & (r > c)
    eye = (r == c).astype(F32)
    tri = tri_ref[...]
    gn = gn_ref[...]

    def blk(ref, b, j, p):
        x = ref[b, C * j:C * (j + 1), LANES * p:LANES * (p + 1)]
        return jnp.concatenate([jnp.where(lo_lane, x, 0.0), jnp.where(lo_lane, 0.0, x)], axis=0)

    chains = [(b, p) for b in range(B) for p in range(NP)]
    units = [(b, p, j) for j in range(cps) for (b, p) in chains]
    st = {}
    for u in units:
        b, p, j = u
        q2, k2, v2 = blk(q_ref, b, j, p), blk(k_ref, b, j, p), blk(v_ref, b, j, p)
        b2, g2 = blk(be_ref, b, j, p), blk(ge_ref, b, j, p)
        g_hi, g_lo = _split_bf16(g2)
        gcc = jnp.dot(tri, jnp.concatenate([g_hi, g_lo], axis=1), preferred_element_type=F32)
        gc = gcc[:, :LANES] + gcc[:, LANES:]
        decay = jnp.exp(jnp.where(causal, gc - gc.T, NEG))
        kb = k2 * b2
        gram = _mm_nt(jnp.concatenate([kb, q2], axis=0), k2)
        egc = jnp.exp(gc)
        glast = jnp.concatenate([jnp.broadcast_to(gc[C - 1:C, :], (C, LANES)),
                                 jnp.broadcast_to(gc[P - 1:P, :], (C, LANES))], axis=0)
        st[u] = dict(L=jnp.where(strict, gram[:P] * decay, 0.0), qk=gram[P:] * decay,
                     rhs=jnp.concatenate([v2 * b2, kb * egc], axis=1), qd=q2 * egc,
                     kdT=(k2 * jnp.exp(glast - gc)).T, eg=jnp.exp(glast))

    for u in units:
        st[u]["X"] = eye
    s = 1
    while s < C:
        couple = same & ((r // (2 * s)) == (c // (2 * s))) & (((r // s) % 2) == 1) & (((c // s) % 2) == 0)
        for u in units:
            X = st[u]["X"]
            st[u]["X"] = X - _mm(X, _mm(jnp.where(couple, st[u]["L"], 0.0), X))
        s *= 2
    for u in units:
        st[u].pop("L")
        uw = _mm(st[u].pop("X"), st[u].pop("rhs"))
        st[u]["u"], st[u]["w"] = uw[:, :LANES], uw[:, LANES:]

    S = {ch: s_ref[ch[0], ch[1]] for ch in chains}
    for j in range(cps):
        for ch in chains:
            b, p = ch
            d = st[(b, p, j)]
            ws = _mm(jnp.concatenate([d["w"], d["qd"]], axis=0), S[ch])
            vnew = d["u"] - ws[:P]
            ov = _mm(jnp.concatenate([d["qk"], d["kdT"]], axis=0), vnew)
            o2 = ws[P:] + ov[:P]
            S[ch] = S[ch] * d["eg"] + ov[P:]
            ms = jnp.sum(o2 * o2, axis=-1, keepdims=True) * (1.0 / HEAD_DIM)
            on = o2 * lax.rsqrt(ms + EPS) * gn
            z = z_ref[b, C * j:C * (j + 1), LANES * p:LANES * (p + 1)]
            o_ref[b, C * j:C * (j + 1), LANES * p:LANES * (p + 1)] = (on[:C] + on[C:]) * _silu(z)
    for ch in chains:
        s_ref[ch[0], ch[1]] = S[ch]


def _gdn_scan(q, k, v, be, ge, z, gdn_norm, cps=2):
    B, T, _ = q.shape
    C = GDN_CHUNK
    tm = C * cps
    i2 = jnp.arange(2 * C)
    tri = (((i2[:, None] // C) == (i2[None, :] // C)) & (i2[:, None] >= i2[None, :])).astype(BF16)
    gn2 = jnp.tile(gdn_norm, 2).reshape(1, LANES)
    tok = pl.BlockSpec((B, tm, GDN_WIDTH), lambda c: (0, c, 0))
    return pl.pallas_call(
        functools.partial(_gdn_scan_body, cps=cps),
        grid=(T // tm,),
        in_specs=[tok] * 6 + [_const_spec(tri.shape), _const_spec((1, LANES))],
        out_specs=tok,
        out_shape=jax.ShapeDtypeStruct((B, T, GDN_WIDTH), F32),
        scratch_shapes=[pltpu.VMEM((B, GDN_WIDTH // LANES, 2 * C, LANES), F32)],
        compiler_params=_params(("arbitrary",)),
        name="gdn_scan",
    )(q, k, v, be, ge, z, tri, gn2)


def _nsa_prep_body(q_ref, kv_ref, segq_ref, segk_ref, qn_ref, kn_ref,
                   qT_ref, ks_ref, kw_ref, vs_ref, vw_ref):
    tm = q_ref.shape[1]
    q = q_ref[0]
    ms = _dot_x_exact(q * q, segq_ref[...]) * (1.0 / HEAD_DIM)
    qn = q * lax.rsqrt(ms + EPS) * qn_ref[...] * (HEAD_DIM ** -0.5 * LOG2E)
    qT_ref[0] = qn.T.astype(BF16)
    kv = kv_ref[0]
    segk = segk_ref[...]
    kn = kn_ref[...]

    def knorm(k, g):
        msk = _dot_x_exact(k * k, segk) * (1.0 / HEAD_DIM)
        return k * lax.rsqrt(msk + EPS) * g

    ks = knorm(kv[:, 2 * LANES:3 * LANES], kn[0:1, :])
    lane = lax.broadcasted_iota(jnp.int32, (tm, LANES), 1)
    tok = pl.program_id(1) * tm + lax.broadcasted_iota(jnp.int32, (tm, LANES), 0)
    onehot = (lane - HEAD_DIM == (tok // SEL_BLOCK) % (KEY_TILE // SEL_BLOCK)).astype(F32)
    ks_ref[0, 0] = jnp.where(lane < HEAD_DIM, ks, onehot).astype(BF16)
    ks_ref[0, 1] = jnp.where(lane < HEAD_DIM, pltpu.roll(ks, HEAD_DIM, 1), onehot).astype(BF16)
    kw_ref[0] = knorm(kv[:, 4 * LANES:5 * LANES], kn[1:2, :]).astype(BF16)
    ones_rows = (lax.broadcasted_iota(jnp.int32, (V_ROWS - HEAD_DIM, tm), 0) == 0).astype(BF16)
    for v_ref, cols in ((vs_ref, 3), (vw_ref, 5)):
        vT = kv[:, cols * LANES:(cols + 1) * LANES].T.astype(BF16)
        for h in range(NSA_KV_HEADS):
            v_ref[0, h, :HEAD_DIM, :] = vT[h * HEAD_DIM:(h + 1) * HEAD_DIM, :]
            v_ref[0, h, HEAD_DIM:, :] = ones_rows


def _nsa_prep(qb, kvb, q_norm, k_norm, tm=512):
    B, T, _ = qb.shape
    chq = jnp.arange(NSA_WIDTH) // HEAD_DIM
    segq = (chq[:, None] == chq[None, :]).astype(BF16)
    segk = segq[:LANES, :LANES]
    qn = jnp.tile(q_norm, NSA_HEADS).reshape(1, NSA_WIDTH)
    kn = jnp.tile(k_norm[1:3], (1, NSA_KV_HEADS))
    tok = lambda c: pl.BlockSpec((1, tm, c), lambda b, i: (b, i, 0))
    vspec = pl.BlockSpec((1, NSA_KV_HEADS, V_ROWS, tm), lambda b, i: (b, 0, 0, i))
    vshape = jax.ShapeDtypeStruct((B, NSA_KV_HEADS, V_ROWS, T), BF16)
    return pl.pallas_call(
        _nsa_prep_body,
        grid=(B, T // tm),
        in_specs=[tok(NSA_WIDTH), tok(6 * LANES), _const_spec(segq.shape), _const_spec(segk.shape),
                  _const_spec(qn.shape), _const_spec(kn.shape)],
        out_specs=[pl.BlockSpec((1, NSA_WIDTH, tm), lambda b, i: (b, 0, i)),
                   pl.BlockSpec((1, NSA_KV_HEADS, tm, LANES), lambda b, i: (b, 0, i, 0)),
                   tok(LANES), vspec, vspec],
        out_shape=[jax.ShapeDtypeStruct((B, NSA_WIDTH, T), BF16),
                   jax.ShapeDtypeStruct((B, NSA_KV_HEADS, T, LANES), BF16),
                   jax.ShapeDtypeStruct((B, T, LANES), BF16),
                   vshape, vshape],
        compiler_params=_params(("parallel", "parallel")),
        name="nsa_prep",
    )(qb, kvb, segq, segk, qn, kn)


def _nsa_compress_body(x_ref, wd_ref, pos_ref, w2k_ref, w2vT_ref, kn_ref, kc_ref, vcT_ref):
    branch = pl.program_id(1)
    nr = x_ref.shape[1] // CMP_STRIDE
    hid2 = NSA_KV_HEADS * CMP_HIDDEN
    ya = jnp.zeros((nr, hid2), F32)
    yb = jnp.zeros((nr, hid2), F32)
    for l in range(CMP_STRIDE):
        xl = x_ref[0, pl.ds(l, nr, stride=CMP_STRIDE), :].astype(BF16)
        ya = ya + jnp.dot(xl, wd_ref[0, l * LANES:(l + 1) * LANES, :], preferred_element_type=F32)
        yb = yb + jnp.dot(xl, wd_ref[0, (CMP_STRIDE + l) * LANES:(CMP_STRIDE + l + 1) * LANES, :],
                          preferred_element_type=F32)
    bias = jnp.dot(pos_ref[0], wd_ref[0], preferred_element_type=F32)[0:1, :]
    hidden = _silu(ya + pltpu.roll(yb, nr - 1, 0) + bias).astype(BF16)

    @pl.when(branch == 0)
    def _():
        for h in range(NSA_KV_HEADS):
            kc = jnp.dot(hidden[:, h * CMP_HIDDEN:(h + 1) * CMP_HIDDEN], w2k_ref[...], preferred_element_type=F32)
            kc_ref[0, h] = _rms(kc, kn_ref[...]).astype(BF16)

    @pl.when(branch == 1)
    def _():
        for h in range(NSA_KV_HEADS):
            vcT_ref[0, h] = lax.dot_general(w2vT_ref[...], hidden[:, h * CMP_HIDDEN:(h + 1) * CMP_HIDDEN],
                                            (((1,), (1,)), ((), ())), preferred_element_type=F32).astype(BF16)


def _nsa_compress(kvb, cmp_pos, cmp_w1, cmp_w2, k_norm0):
    B, T, _ = kvb.shape
    nr = T // CMP_STRIDE
    w1 = cmp_w1.reshape(2, CMP_LEN, HEAD_DIM, CMP_HIDDEN)
    z = jnp.zeros_like(w1)
    wd = jnp.concatenate([jnp.concatenate([w1, z], axis=3), jnp.concatenate([z, w1], axis=3)], axis=2)
    wd = wd.reshape(2, CMP_LEN * LANES, NSA_KV_HEADS * CMP_HIDDEN).astype(BF16)
    pos = jnp.tile(cmp_pos, (1, 1, NSA_KV_HEADS)).reshape(2, 1, CMP_LEN * LANES)
    pos = jnp.broadcast_to(pos, (2, SUBLANES, CMP_LEN * LANES)).astype(BF16)
    return pl.pallas_call(
        _nsa_compress_body,
        grid=(B, 2),
        in_specs=[pl.BlockSpec((1, T, LANES), lambda b, r: (b, 0, r)),
                  pl.BlockSpec((1,) + wd.shape[1:], lambda b, r: (r, 0, 0)),
                  pl.BlockSpec((1,) + pos.shape[1:], lambda b, r: (r, 0, 0)),
                  _const_spec((CMP_HIDDEN, HEAD_DIM)), _const_spec((HEAD_DIM, CMP_HIDDEN)),
                  _const_spec((1, HEAD_DIM))],
        out_specs=[pl.BlockSpec((1, NSA_KV_HEADS, nr, HEAD_DIM), lambda b, r: (b, 0, 0, 0)),
                   pl.BlockSpec((1, NSA_KV_HEADS, HEAD_DIM, nr), lambda b, r: (b, 0, 0, 0))],
        out_shape=[jax.ShapeDtypeStruct((B, NSA_KV_HEADS, nr, HEAD_DIM), BF16),
                   jax.ShapeDtypeStruct((B, NSA_KV_HEADS, HEAD_DIM, nr), BF16)],
        compiler_params=_params(("parallel", "arbitrary")),
        name="nsa_compress",
    )(kvb, wd, pos, cmp_w2[0].astype(BF16), cmp_w2[1].T.astype(BF16), k_norm0.reshape(1, HEAD_DIM))


def _nsa_attn_body(qT_ref, kc_ref, vcT_ref, ks_ref, vs_ref, kw_ref, vw_ref, gate_ref, selm_ref,
                   o_ref, selb_ref, gT_ref, qaug_ref, m_ref, acc_ref, sa_ref, sb_ref, oc_ref, ow_ref, *, seq):
    G, Dh, Qb = NSA_GROUP, HEAD_DIM, Q_BLOCK
    h = pl.program_id(1)
    t0 = pl.program_id(2) * Qb
    nr = seq // CMP_STRIDE
    ns = seq // SEL_BLOCK
    pos = t0 + lax.broadcasted_iota(jnp.int32, (1, Qb), 1)

    qT = qT_ref[0]
    q3 = jnp.concatenate([qT[g * Dh:(g + 1) * Dh, :] for g in range(G)], axis=1)
    qaug_ref[...] = jnp.concatenate([q3, jnp.zeros_like(q3)], axis=0)
    row = lax.broadcasted_iota(jnp.int32, (2 * Dh, G * Qb), 0)
    qpad = jnp.where((row < Dh) == (h == 0), jnp.concatenate([q3, q3], axis=0), 0).astype(BF16)

    ws = pl.multiple_of(jnp.maximum(t0 - WINDOW, 0), Qb)
    sw = jnp.dot(kw_ref[0, pl.ds(ws, WIN_KEYS), :], qpad, preferred_element_type=F32)
    vwx = vw_ref[0, 0, :, pl.ds(ws, WIN_KEYS)]
    kpos = ws + lax.broadcasted_iota(jnp.int32, (WIN_KEYS, 1), 0)
    bias_w = jnp.where((kpos <= pos) & (kpos > pos - WINDOW), 0.0, NEG)
    for g in range(G):
        s = sw[:, g * Qb:(g + 1) * Qb] + bias_w
        e = jnp.exp2(s - jnp.max(s, axis=0, keepdims=True)).astype(BF16)
        ow_ref[g] = jnp.dot(vwx, e, preferred_element_type=F32)

    def compressed_and_select(n_c):
        n_s = n_c * CMP_STRIDE // SEL_BLOCK
        sc = jnp.dot(kc_ref[0, 0, :n_c, :], q3, preferred_element_type=F32)
        cmp_end = lax.broadcasted_iota(jnp.int32, (n_c, 1), 0) * CMP_STRIDE + (CMP_LEN - 1)
        bias_c = jnp.where(cmp_end <= pos, 0.0, NEG)
        has_c = pos >= CMP_LEN - 1
        vcT = vcT_ref[0, 0, :, :n_c]
        psum = jnp.zeros((n_c, Qb), F32)
        for g in range(G):
            s = sc[:, g * Qb:(g + 1) * Qb] + bias_c
            e = jnp.exp2(s - jnp.max(s, axis=0, keepdims=True))
            p = e * jnp.where(has_c, 1.0 / jnp.sum(e, axis=0, keepdims=True), 0.0)
            psum = psum + p
            oc_ref[g] = jnp.dot(vcT, p.astype(BF16), preferred_element_type=F32)

        imp = _dot_exact_x(selm_ref[:n_s, :n_c], psum)
        blk = lax.broadcasted_iota(jnp.int32, (n_s, Qb), 0)
        cur = pos // SEL_BLOCK
        forced = (blk == 0) | (blk == cur) | (blk == cur - 1)
        score = jnp.where(forced | (blk > cur), -jnp.inf, imp)
        for _ in range(SEL_TOPK - 3):
            best = jnp.max(score, axis=0, keepdims=True)
            first = jnp.min(jnp.where(score == best, blk, n_s), axis=0, keepdims=True)
            score = jnp.where(blk == first, -jnp.inf, score)
        selb_ref[:n_s, :] = jnp.where((score == -jnp.inf) & (blk <= cur), 0.0, NEG)

    chunk = min(CMP_CHUNK, nr)
    need = (t0 + Qb) // CMP_STRIDE
    for k in range(1, nr // chunk + 1):
        pl.when((need + chunk - 1) // chunk == k)(functools.partial(compressed_and_select, k * chunk))

    bpt = KEY_TILE // SEL_BLOCK
    krow = lax.broadcasted_iota(jnp.int32, (KEY_TILE, 1), 0)
    m_ref[...] = jnp.full(m_ref.shape, NEG, F32)
    acc_ref[...] = jnp.zeros(acc_ref.shape, F32)

    def scores(kt, slot_ref, nk=KEY_TILE):
        base = pl.multiple_of(kt * KEY_TILE, KEY_TILE)
        sb = selb_ref[pl.ds(pl.multiple_of(kt * bpt, bpt), bpt), :]
        qaug_ref[Dh:Dh + bpt, :] = jnp.concatenate([sb] * G, axis=1).astype(BF16)
        slot_ref[:nk, :] = jnp.dot(ks_ref[0, 0, pl.ds(base, nk), :], qaug_ref[...],
                                   preferred_element_type=F32)

    def consume(kt, slot_ref, diagonal, nk=KEY_TILE):
        base = pl.multiple_of(kt * KEY_TILE, KEY_TILE)
        vx = vs_ref[0, 0, :, pl.ds(base, nk)]
        for g in range(G):
            s = slot_ref[:nk, g * Qb:(g + 1) * Qb]
            if diagonal:
                s = jnp.where(base + krow[:nk] <= pos, s, NEG)
            m_old = m_ref[g:g + 1, :]
            m_new = jnp.maximum(m_old, jnp.max(s, axis=0, keepdims=True))
            e = jnp.exp2(s - m_new).astype(BF16)
            acc_ref[g] = jnp.exp2(m_old - m_new) * acc_ref[g] + jnp.dot(vx, e, preferred_element_type=F32)
            m_ref[g:g + 1, :] = m_new

    n_off = (t0 + Qb + KEY_TILE - 1) // KEY_TILE - 1
    scores(0, sa_ref)

    def pair(p, carry):
        scores(2 * p + 1, sb_ref)
        consume(2 * p, sa_ref, False)
        scores(2 * p + 2, sa_ref)
        consume(2 * p + 1, sb_ref, False)
        return carry

    lax.fori_loop(0, n_off // 2, pair, 0)

    def last_odd(nk):
        scores(n_off, sb_ref, nk)
        consume(n_off - 1, sa_ref, False)
        consume(n_off, sb_ref, True, nk)

    def last_even(nk):
        consume(n_off, sa_ref, True, nk)

    q_in_tile = (t0 % KEY_TILE) // Qb
    for r in range(KEY_TILE // Qb):
        pl.when((n_off % 2 == 1) & (q_in_tile == r))(functools.partial(last_odd, (r + 1) * Qb))
        pl.when((n_off % 2 == 0) & (q_in_tile == r))(functools.partial(last_even, (r + 1) * Qb))

    gT_ref[...] = _sigmoid(gate_ref[0]).T
    gT = gT_ref[0:3 * SUBLANES, :]
    for g in range(G):
        r0 = g * 3
        gates = jnp.where(h == 0, gT[r0:r0 + 3], gT[G * 3 + r0:G * 3 + r0 + 3])
        a_s, a_w = acc_ref[g], ow_ref[g]
        o_s = a_s[:Dh] * (1.0 / a_s[Dh:Dh + 1])
        o_w = a_w[:Dh] * (1.0 / a_w[Dh:Dh + 1])
        o_ref[0, g * Dh:(g + 1) * Dh, :] = gates[0:1] * oc_ref[g] + gates[1:2] * o_s + gates[2:3] * o_w


def _nsa_attn(qT, kc, vcT, ks, vs, kw, vw, gate):
    B, _, T = qT.shape
    assert T % KEY_TILE == 0 and T >= WIN_KEYS
    nr, ns = T // CMP_STRIDE, T // SEL_BLOCK
    G, Dh, Qb = NSA_GROUP, HEAD_DIM, Q_BLOCK
    j = jnp.arange(ns)[:, None]
    c = jnp.arange(nr)[None, :]
    ratio = SEL_BLOCK // CMP_STRIDE
    selm = ((c >= ratio * j - 1) & (c <= ratio * j + ratio - 1) & (c < nr - 1)).astype(BF16)
    vfull = pl.BlockSpec((1, 1, V_ROWS, T), lambda b, h, i: (b, h, 0, 0))
    return pl.pallas_call(
        functools.partial(_nsa_attn_body, seq=T),
        grid=(B, NSA_KV_HEADS, T // Qb),
        in_specs=[pl.BlockSpec((1, G * Dh, Qb), lambda b, h, i: (b, h, i)),
                  pl.BlockSpec((1, 1, nr, Dh), lambda b, h, i: (b, h, 0, 0)),
                  pl.BlockSpec((1, 1, Dh, nr), lambda b, h, i: (b, h, 0, 0)),
                  pl.BlockSpec((1, 1, T, LANES), lambda b, h, i: (b, h, 0, 0)), vfull,
                  pl.BlockSpec((1, T, LANES), lambda b, h, i: (b, 0, 0)), vfull,
                  pl.BlockSpec((1, Qb, LANES), lambda b, h, i: (b, i, 0)),
                  _const_spec(selm.shape)],
        out_specs=pl.BlockSpec((1, G * Dh, Qb), lambda b, h, i: (b, h, i)),
        out_shape=jax.ShapeDtypeStruct((B, NSA_WIDTH, T), F32),
        scratch_shapes=[pltpu.VMEM((ns, Qb), F32), pltpu.VMEM((LANES, Qb), F32),
                        pltpu.VMEM((2 * Dh, G * Qb), BF16), pltpu.VMEM((SUBLANES, Qb), F32),
                        pltpu.VMEM((G, V_ROWS, Qb), F32),
                        pltpu.VMEM((KEY_TILE, G * Qb), F32), pltpu.VMEM((KEY_TILE, G * Qb), F32),
                        pltpu.VMEM((G, Dh, Qb), F32), pltpu.VMEM((G, V_ROWS, Qb), F32)],
        compiler_params=_params(("parallel", "parallel", "arbitrary")),
        name="nsa_attn",
    )(qT, kc, vcT, ks, vs, kw, vw, gate, selm)


POOL_HALO = 16
FF_TILE = 1024


def _pool_tile(u, halo, first_pos, w, scale):
    tm = u.shape[0]
    ue = jnp.concatenate([halo, u], axis=0)
    grp = lax.broadcasted_iota(jnp.int32, (1, POOL_WIDTH), 1) // POOL_GROUP_DIM
    acc = ue
    wsum = jnp.zeros_like(ue)
    width = jnp.zeros((1, POOL_WIDTH), F32)
    for gi, win in enumerate(POOL_WINDOWS):
        acc = acc + pltpu.roll(acc, win // 2, 0)
        wsum = jnp.where(grp == gi, acc, wsum)
        width = jnp.where(grp == gi, float(win), width)
    t1 = (first_pos + 1 + lax.broadcasted_iota(jnp.int32, (tm, 1), 0)).astype(F32)
    mean = wsum[POOL_HALO:] / jnp.minimum(t1, width)
    return jnp.dot((mean - u).astype(BF16), w, preferred_element_type=F32) * scale


def _post_body(x_ref, ya_ref, ybT_ref, u_ref, halo_ref, wa_ref, wb_ref, wc_ref, pw_ref, psc_ref, g_ref,
               w1_ref, w2_ref, o_ref):
    i = pl.program_id(1)
    tm = x_ref.shape[1]
    halo = jnp.where(i > 0, halo_ref[0], 0.0)
    yc = _pool_tile(u_ref[0], halo, i * tm, pw_ref[...], psc_ref[...])
    x = x_ref[0]
    x = x + jnp.dot(ya_ref[0].astype(BF16), wa_ref[...], preferred_element_type=F32)
    x = x + jnp.dot(ybT_ref[0].T.astype(BF16), wb_ref[...], preferred_element_type=F32)
    x = x + jnp.dot(yc.astype(BF16), wc_ref[...], preferred_element_type=F32)
    o_ref[0] = x
    x = o_ref[0]
    h = _rms(x, g_ref[...]).astype(BF16)
    acc = x
    for j in range(D_FF // FF_TILE):
        cols = slice(j * FF_TILE, (j + 1) * FF_TILE)
        a = jnp.maximum(jnp.dot(h, w1_ref[:, cols], preferred_element_type=F32), 0.0)
        acc = acc + jnp.dot((a * a).astype(BF16), w2_ref[cols, :], preferred_element_type=F32)
    o_ref[0] = acc


def _post(x, ya, ybT, u, pool_w, pool_scale, w_out, g, w1, w2, tm=512):
    B, T, _ = x.shape
    w = w_out.astype(BF16)
    wa, wb, wc = w[:GDN_WIDTH], w[GDN_WIDTH:GDN_WIDTH + NSA_WIDTH], w[GDN_WIDTH + NSA_WIDTH:]
    wblk = jax.scipy.linalg.block_diag(*[pool_w[gi] for gi in range(len(POOL_WINDOWS))]).astype(BF16)
    tok = lambda c: pl.BlockSpec((1, tm, c), lambda b, i: (b, i, 0))
    hb = tm // POOL_HALO
    return pl.pallas_call(
        _post_body,
        grid=(B, T // tm),
        in_specs=[tok(D_MODEL), tok(GDN_WIDTH), pl.BlockSpec((1, NSA_WIDTH, tm), lambda b, i: (b, 0, i)),
                  tok(POOL_WIDTH),
                  pl.BlockSpec((1, POOL_HALO, POOL_WIDTH), lambda b, i: (b, jnp.maximum(i * hb - 1, 0), 0)),
                  _const_spec(wa.shape), _const_spec(wb.shape), _const_spec(wc.shape),
                  _const_spec(wblk.shape), _const_spec((1, POOL_WIDTH)), _const_spec((1, D_MODEL)),
                  _const_spec(w1.shape), _const_spec(w2.shape)],
        out_specs=tok(D_MODEL),
        out_shape=jax.ShapeDtypeStruct(x.shape, F32),
        compiler_params=_params(("parallel", "parallel")),
        name="post",
    )(x, ya, ybT, u, u, wa, wb, wc, wblk, pool_scale.reshape(1, POOL_WIDTH), g.reshape(1, D_MODEL),
      w1.astype(BF16), w2.astype(BF16))


def _pad_in_weights(w):
    H = GDN_HEADS
    o_ba = 4 * GDN_WIDTH
    o_q = o_ba + 2 * H
    o_gate = o_q + NSA_WIDTH + 6 * LANES
    o_u = o_gate + 3 * NSA_HEADS
    pad = lambda a: jnp.pad(a, ((0, 0), (0, LANES - a.shape[1])))
    return jnp.concatenate([w[:, :o_ba], pad(w[:, o_ba:o_q]), w[:, o_q:o_gate], pad(w[:, o_gate:o_u]),
                            w[:, o_u:]], axis=1).astype(BF16)


def _mixers(qkv, z, ba, qb, kvb, gate, conv_w, a_log, dt_bias, gdn_norm, nsa_q_norm, nsa_k_norm,
            cmp_pos, cmp_w1, cmp_w2):
    q, k, v, be, ge = _gdn_prep(qkv, ba, conv_w, a_log, dt_bias)
    ya = _gdn_scan(q, k, v, be, ge, z, gdn_norm)
    qT, ks, kw, vs, vw = _nsa_prep(qb, kvb, nsa_q_norm, nsa_k_norm)
    kc, vcT = _nsa_compress(kvb, cmp_pos, cmp_w1, cmp_w2, nsa_k_norm[0])
    ybT = _nsa_attn(qT, kc, vcT, ks, vs, kw, vw, gate)
    return ya, ybT


def kernel(x, norm_mix, w_in, conv_w, a_log, dt_bias, gdn_norm, nsa_q_norm, nsa_k_norm, cmp_pos, cmp_w1, cmp_w2,
           pool_w, pool_scale, w_out, norm_ffn, w_ffn1, w_ffn2):
    B, T, D = x.shape
    for l in range(w_in.shape[0]):
        proj = _inproj(x.reshape(B * T, D), norm_mix[l], _pad_in_weights(w_in[l]))
        qkv, z, ba, qb, kvb, gate, u = [p.reshape(B, T, -1) for p in proj]
        ya, ybT = _mixers(qkv, z, ba, qb, kvb, gate, conv_w[l], a_log[l], dt_bias[l], gdn_norm[l],
                          nsa_q_norm[l], nsa_k_norm[l], cmp_pos[l], cmp_w1[l], cmp_w2[l])
        x = _post(x, ya, ybT, u, pool_w[l], pool_scale[l], w_out[l], norm_ffn[l], w_ffn1[l], w_ffn2[l])
    return x
```

```python
import functools

import jax
import jax.numpy as jnp
from jax import lax
from jax.experimental import pallas as pl
from jax.experimental.pallas import tpu as pltpu

F32 = jnp.float32
BF16 = jnp.bfloat16

D_MODEL = 1024
HEAD_DIM = 64
GDN_HEADS = 6
GDN_CHUNK = 64
GDN_WIDTH = GDN_HEADS * HEAD_DIM
NSA_HEADS = 6
NSA_KV_HEADS = 2
NSA_GROUP = NSA_HEADS // NSA_KV_HEADS
NSA_WIDTH = NSA_HEADS * HEAD_DIM
CMP_LEN = 32
CMP_STRIDE = 16
CMP_HIDDEN = 256
SEL_BLOCK = 64
SEL_TOPK = 16
WINDOW = 512
Q_BLOCK = 256
FORCED_SCORE = 1e4
POOL_WINDOWS = (2, 4, 8, 16)
POOL_GROUP_DIM = 64
POOL_WIDTH = 4 * POOL_GROUP_DIM
D_FF = 4 * D_MODEL
EPS = 1e-6

LANES = 128
SUBLANES = 8
VMEM_LIMIT = 56 * 1024 * 1024
NEG = -1e30
LOG2E = 1.4426950408889634
KEY_TILE = 1024
WIN_KEYS = WINDOW + Q_BLOCK
CMP_CHUNK = 256
V_ROWS = 80

IN_GROUPS = (3 * GDN_WIDTH, GDN_WIDTH, LANES, NSA_WIDTH, 6 * LANES, LANES, POOL_WIDTH)


def _sigmoid(x):
    return 1.0 / (1.0 + jnp.exp(-x))


def _silu(x):
    return x * _sigmoid(x)


def _softplus(x):
    return jnp.maximum(x, 0.0) + jnp.log(1.0 + jnp.exp(-jnp.abs(x)))


def _split_bf16(x):
    hi = x.astype(BF16)
    lo = (x - hi.astype(F32)).astype(BF16)
    return hi, lo


def _dot_x_exact(x, m):
    hi, lo = _split_bf16(x)
    return jnp.dot(hi, m, preferred_element_type=F32) + jnp.dot(lo, m, preferred_element_type=F32)


def _dot_exact_x(m, x):
    hi, lo = _split_bf16(x)
    return jnp.dot(m, hi, preferred_element_type=F32) + jnp.dot(m, lo, preferred_element_type=F32)


def _mm(a, b):
    return jnp.dot(a.astype(BF16), b.astype(BF16), preferred_element_type=F32)


def _mm_nt(a, b):
    return lax.dot_general(a.astype(BF16), b.astype(BF16), (((1,), (1,)), ((), ())),
                           preferred_element_type=F32)


def _mm_tn(a, b):
    return lax.dot_general(a.astype(BF16), b.astype(BF16), (((0,), (0,)), ((), ())),
                           preferred_element_type=F32)


def _rms(x, g):
    return x * lax.rsqrt(jnp.mean(x * x, axis=-1, keepdims=True) + EPS) * g


def _const_spec(shape):
    nd = len(shape)
    return pl.BlockSpec(shape, lambda *_: (0,) * nd)


def _params(sem):
    return pltpu.CompilerParams(dimension_semantics=sem, vmem_limit_bytes=VMEM_LIMIT)


def _inproj_body(x_ref, g_ref, w_ref, *out_refs):
    h = _rms(x_ref[...], g_ref[...]).astype(BF16)
    off = 0
    for o_ref in out_refs:
        n = o_ref.shape[-1]
        o_ref[...] = jnp.dot(h, w_ref[:, off:off + n], preferred_element_type=F32)
        off += n


def _inproj(xf, g, w_pad, tm=512):
    n = xf.shape[0]
    d_in = w_pad.shape[1]
    return pl.pallas_call(
        _inproj_body,
        grid=(n // tm,),
        in_specs=[pl.BlockSpec((tm, D_MODEL), lambda i: (i, 0)),
                  _const_spec((1, D_MODEL)),
                  _const_spec((D_MODEL, d_in))],
        out_specs=[pl.BlockSpec((tm, c), lambda i: (i, 0)) for c in IN_GROUPS],
        out_shape=[jax.ShapeDtypeStruct((n, c), F32) for c in IN_GROUPS],
        compiler_params=_params(("parallel",)),
        name="inproj",
    )(xf, g.reshape(1, D_MODEL), w_pad)


def _gdn_prep_body(x_ref, halo_ref, ba_ref, cw_ref, alog_ref, dtb_ref, seg_ref, eb_ref, eg_ref,
                   q_ref, k_ref, v_ref, be_ref, ge_ref):
    i = pl.program_id(1)
    x = x_ref[0]
    halo = jnp.where(i > 0, halo_ref[0], 0.0)
    xe = jnp.concatenate([halo, x], axis=0)
    cw = cw_ref[...]
    y = x * cw[3:4, :]
    for s in (1, 2, 3):
        y = y + pltpu.roll(xe, s, 0)[SUBLANES:] * cw[3 - s:4 - s, :]
    a = _silu(y)
    q = a[:, :GDN_WIDTH]
    k = a[:, GDN_WIDTH:2 * GDN_WIDTH]
    seg = seg_ref[...]
    q_ref[0] = q * lax.rsqrt(_dot_x_exact(q * q, seg) + EPS) * (HEAD_DIM ** -0.5)
    k_ref[0] = k * lax.rsqrt(_dot_x_exact(k * k, seg) + EPS)
    v_ref[0] = a[:, 2 * GDN_WIDTH:]
    ba = ba_ref[0]
    beta = _sigmoid(ba)
    gval = -jnp.exp(alog_ref[...]) * _softplus(ba + dtb_ref[...])
    be_ref[0] = _dot_x_exact(beta, eb_ref[...])
    ge_ref[0] = _dot_x_exact(gval, eg_ref[...])


def _gdn_prep(qkv, ba, conv_w, a_log, dt_bias, tm=256):
    B, T, C = qkv.shape
    H = GDN_HEADS
    alog_p = jnp.zeros((1, LANES), F32).at[0, H:2 * H].set(a_log)
    dtb_p = jnp.zeros((1, LANES), F32).at[0, H:2 * H].set(dt_bias)
    ch = jnp.arange(GDN_WIDTH) // HEAD_DIM
    seg = (ch[:, None] == ch[None, :]).astype(BF16)
    row = jnp.arange(LANES)
    eb = (row[:, None] == ch[None, :]).astype(BF16)
    eg = (row[:, None] - H == ch[None, :]).astype(BF16)
    tok = lambda c: pl.BlockSpec((1, tm, c), lambda b, i: (b, i, 0))
    halo_blocks = tm // SUBLANES
    out = jax.ShapeDtypeStruct((B, T, GDN_WIDTH), F32)
    return pl.pallas_call(
        _gdn_prep_body,
        grid=(B, T // tm),
        in_specs=[tok(C),
                  pl.BlockSpec((1, SUBLANES, C), lambda b, i: (b, jnp.maximum(i * halo_blocks - 1, 0), 0)),
                  tok(LANES),
                  _const_spec(conv_w.shape), _const_spec((1, LANES)), _const_spec((1, LANES)),
                  _const_spec(seg.shape), _const_spec(eb.shape), _const_spec(eg.shape)],
        out_specs=[tok(GDN_WIDTH)] * 5,
        out_shape=[out] * 5,
        compiler_params=_params(("parallel", "parallel")),
        name="gdn_prep",
    )(qkv, qkv, ba, conv_w, alog_p, dtb_p, seg, eb, eg)


def _gdn_scan_body(q_ref, k_ref, v_ref, be_ref, ge_ref, z_ref, tri_ref, gn_ref, o_ref, s_ref, *, cps):
    C = GDN_CHUNK
    P = 2 * C

    B = q_ref.shape[0]
    NP = GDN_WIDTH // LANES

    @pl.when(pl.program_id(0) == 0)
    def _():
        s_ref[...] = jnp.zeros_like(s_ref)

    lo_lane = lax.broadcasted_iota(jnp.int32, (C, LANES), 1) < HEAD_DIM
    r = lax.broadcasted_iota(jnp.int32, (P, P), 0)
    c = lax.broadcasted_iota(jnp.int32, (P, P), 1)
    same = (r < C) == (c < C)
    causal = same & (r >= c)
    strict = same & (r > c)
    eye = (r == c).astype(F32)
    tri = tri_ref[...]
    gn = gn_ref[...]

    def blk(ref, b, j, p):
        x = ref[b, C * j:C * (j + 1), LANES * p:LANES * (p + 1)]
        return jnp.concatenate([jnp.where(lo_lane, x, 0.0), jnp.where(lo_lane, 0.0, x)], axis=0)

    chains = [(b, p) for b in range(B) for p in range(NP)]
    units = [(b, p, j) for j in range(cps) for (b, p) in chains]
    st = {}
    for u in units:
        b, p, j = u
        q2, k2, v2 = blk(q_ref, b, j, p), blk(k_ref, b, j, p), blk(v_ref, b, j, p)
        b2, g2 = blk(be_ref, b, j, p), blk(ge_ref, b, j, p)
        g_hi, g_lo = _split_bf16(g2)
        gcc = jnp.dot(tri, jnp.concatenate([g_hi, g_lo], axis=1), preferred_element_type=F32)
        gc = gcc[:, :LANES] + gcc[:, LANES:]
        decay = jnp.exp(jnp.where(causal, gc - gc.T, NEG))
        kb = k2 * b2
        gram = _mm_nt(jnp.concatenate([kb, q2], axis=0), k2)
        egc = jnp.exp(gc)
        glast = jnp.concatenate([jnp.broadcast_to(gc[C - 1:C, :], (C, LANES)),
                                 jnp.broadcast_to(gc[P - 1:P, :], (C, LANES))], axis=0)
        st[u] = dict(L=jnp.where(strict, gram[:P] * decay, 0.0), qk=gram[P:] * decay,
                     rhs=jnp.concatenate([v2 * b2, kb * egc], axis=1), qd=q2 * egc,
                     kdT=(k2 * jnp.exp(glast - gc)).T, eg=jnp.exp(glast))

    for u in units:
        st[u]["X"] = eye
    s = 1
    while s < C:
        couple = same & ((r // (2 * s)) == (c // (2 * s))) & (((r // s) % 2) == 1) & (((c // s) % 2) == 0)
        for u in units:
            X = st[u]["X"]
            st[u]["X"] = X - _mm(X, _mm(jnp.where(couple, st[u]["L"], 0.0), X))
        s *= 2
    for u in units:
        st[u].pop("L")
        uw = _mm(st[u].pop("X"), st[u].pop("rhs"))
        st[u]["u"], st[u]["w"] = uw[:, :LANES], uw[:, LANES:]

    S = {ch: s_ref[ch[0], ch[1]] for ch in chains}
    for j in range(cps):
        for ch in chains:
            b, p = ch
            d = st[(b, p, j)]
            ws = _mm(jnp.concatenate([d["w"], d["qd"]], axis=0), S[ch])
            vnew = d["u"] - ws[:P]
            ov = _mm(jnp.concatenate([d["qk"], d["kdT"]], axis=0), vnew)
            o2 = ws[P:] + ov[:P]
            S[ch] = S[ch] * d["eg"] + ov[P:]
            ms = jnp.sum(o2 * o2, axis=-1, keepdims=True) * (1.0 / HEAD_DIM)
            on = o2 * lax.rsqrt(ms + EPS) * gn
            z = z_ref[b, C * j:C * (j + 1), LANES * p:LANES * (p + 1)]
            o_ref[b, C * j:C * (j + 1), LANES * p:LANES * (p + 1)] = (on[:C] + on[C:]) * _silu(z)
    for ch in chains:
        s_ref[ch[0], ch[1]] = S[ch]


def _gdn_scan(q, k, v, be, ge, z, gdn_norm, cps=2):
    B, T, _ = q.shape
    C = GDN_CHUNK
    tm = C * cps
    i2 = jnp.arange(2 * C)
    tri = (((i2[:, None] // C) == (i2[None, :] // C)) & (i2[:, None] >= i2[None, :])).astype(BF16)
    gn2 = jnp.tile(gdn_norm, 2).reshape(1, LANES)
    tok = pl.BlockSpec((B, tm, GDN_WIDTH), lambda c: (0, c, 0))
    return pl.pallas_call(
        functools.partial(_gdn_scan_body, cps=cps),
        grid=(T // tm,),
        in_specs=[tok] * 6 + [_const_spec(tri.shape), _const_spec((1, LANES))],
        out_specs=tok,
        out_shape=jax.ShapeDtypeStruct((B, T, GDN_WIDTH), F32),
        scratch_shapes=[pltpu.VMEM((B, GDN_WIDTH // LANES, 2 * C, LANES), F32)],
        compiler_params=_params(("arbitrary",)),
        name="gdn_scan",
    )(q, k, v, be, ge, z, tri, gn2)


def _nsa_prep_body(q_ref, kv_ref, segq_ref, segk_ref, qn_ref, kn_ref,
                   qT_ref, ks_ref, kw_ref, vs_ref, vw_ref):
    tm = q_ref.shape[1]
    q = q_ref[0]
    ms = _dot_x_exact(q * q, segq_ref[...]) * (1.0 / HEAD_DIM)
    qn = q * lax.rsqrt(ms + EPS) * qn_ref[...] * (HEAD_DIM ** -0.5 * LOG2E)
    qT_ref[0] = qn.T.astype(BF16)
    kv = kv_ref[0]
    segk = segk_ref[...]
    kn = kn_ref[...]

    def knorm(k, g):
        msk = _dot_x_exact(k * k, segk) * (1.0 / HEAD_DIM)
        return k * lax.rsqrt(msk + EPS) * g

    ks = knorm(kv[:, 2 * LANES:3 * LANES], kn[0:1, :])
    lane = lax.broadcasted_iota(jnp.int32, (tm, LANES), 1)
    tok = pl.program_id(1) * tm + lax.broadcasted_iota(jnp.int32, (tm, LANES), 0)
    onehot = (lane - HEAD_DIM == (tok // SEL_BLOCK) % (KEY_TILE // SEL_BLOCK)).astype(F32)
    ks_ref[0, 0] = jnp.where(lane < HEAD_DIM, ks, onehot).astype(BF16)
    ks_ref[0, 1] = jnp.where(lane < HEAD_DIM, pltpu.roll(ks, HEAD_DIM, 1), onehot).astype(BF16)
    kw_ref[0] = knorm(kv[:, 4 * LANES:5 * LANES], kn[1:2, :]).astype(BF16)
    ones_rows = (lax.broadcasted_iota(jnp.int32, (V_ROWS - HEAD_DIM, tm), 0) == 0).astype(BF16)
    for v_ref, cols in ((vs_ref, 3), (vw_ref, 5)):
        vT = kv[:, cols * LANES:(cols + 1) * LANES].T.astype(BF16)
        for h in range(NSA_KV_HEADS):
            v_ref[0, h, :HEAD_DIM, :] = vT[h * HEAD_DIM:(h + 1) * HEAD_DIM, :]
            v_ref[0, h, HEAD_DIM:, :] = ones_rows


def _nsa_prep(qb, kvb, q_norm, k_norm, tm=512):
    B, T, _ = qb.shape
    chq = jnp.arange(NSA_WIDTH) // HEAD_DIM
    segq = (chq[:, None] == chq[None, :]).astype(BF16)
    segk = segq[:LANES, :LANES]
    qn = jnp.tile(q_norm, NSA_HEADS).reshape(1, NSA_WIDTH)
    kn = jnp.tile(k_norm[1:3], (1, NSA_KV_HEADS))
    tok = lambda c: pl.BlockSpec((1, tm, c), lambda b, i: (b, i, 0))
    vspec = pl.BlockSpec((1, NSA_KV_HEADS, V_ROWS, tm), lambda b, i: (b, 0, 0, i))
    vshape = jax.ShapeDtypeStruct((B, NSA_KV_HEADS, V_ROWS, T), BF16)
    return pl.pallas_call(
        _nsa_prep_body,
        grid=(B, T // tm),
        in_specs=[tok(NSA_WIDTH), tok(6 * LANES), _const_spec(segq.shape), _const_spec(segk.shape),
                  _const_spec(qn.shape), _const_spec(kn.shape)],
        out_specs=[pl.BlockSpec((1, NSA_WIDTH, tm), lambda b, i: (b, 0, i)),
                   pl.BlockSpec((1, NSA_KV_HEADS, tm, LANES), lambda b, i: (b, 0, i, 0)),
                   tok(LANES), vspec, vspec],
        out_shape=[jax.ShapeDtypeStruct((B, NSA_WIDTH, T), BF16),
                   jax.ShapeDtypeStruct((B, NSA_KV_HEADS, T, LANES), BF16),
                   jax.ShapeDtypeStruct((B, T, LANES), BF16),
                   vshape, vshape],
        compiler_params=_params(("parallel", "parallel")),
        name="nsa_prep",
    )(qb, kvb, segq, segk, qn, kn)


def _nsa_compress_body(x_ref, wd_ref, pos_ref, w2k_ref, w2vT_ref, kn_ref, kc_ref, vcT_ref):
    branch = pl.program_id(1)
    nr = x_ref.shape[1] // CMP_STRIDE
    hid2 = NSA_KV_HEADS * CMP_HIDDEN
    ya = jnp.zeros((nr, hid2), F32)
    yb = jnp.zeros((nr, hid2), F32)
    for l in range(CMP_STRIDE):
        xl = x_ref[0, pl.ds(l, nr, stride=CMP_STRIDE), :].astype(BF16)
        ya = ya + jnp.dot(xl, wd_ref[0, l * LANES:(l + 1) * LANES, :], preferred_element_type=F32)
        yb = yb + jnp.dot(xl, wd_ref[0, (CMP_STRIDE + l) * LANES:(CMP_STRIDE + l + 1) * LANES, :],
                          preferred_element_type=F32)
    bias = jnp.dot(pos_ref[0], wd_ref[0], preferred_element_type=F32)[0:1, :]
    hidden = _silu(ya + pltpu.roll(yb, nr - 1, 0) + bias).astype(BF16)

    @pl.when(branch == 0)
    def _():
        for h in range(NSA_KV_HEADS):
            kc = jnp.dot(hidden[:, h * CMP_HIDDEN:(h + 1) * CMP_HIDDEN], w2k_ref[...], preferred_element_type=F32)
            kc_ref[0, h] = _rms(kc, kn_ref[...]).astype(BF16)

    @pl.when(branch == 1)
    def _():
        for h in range(NSA_KV_HEADS):
            vcT_ref[0, h] = lax.dot_general(w2vT_ref[...], hidden[:, h * CMP_HIDDEN:(h + 1) * CMP_HIDDEN],
                                            (((1,), (1,)), ((), ())), preferred_element_type=F32).astype(BF16)


def _nsa_compress(kvb, cmp_pos, cmp_w1, cmp_w2, k_norm0):
    B, T, _ = kvb.shape
    nr = T // CMP_STRIDE
    w1 = cmp_w1.reshape(2, CMP_LEN, HEAD_DIM, CMP_HIDDEN)
    z = jnp.zeros_like(w1)
    wd = jnp.concatenate([jnp.concatenate([w1, z], axis=3), jnp.concatenate([z, w1], axis=3)], axis=2)
    wd = wd.reshape(2, CMP_LEN * LANES, NSA_KV_HEADS * CMP_HIDDEN).astype(BF16)
    pos = jnp.tile(cmp_pos, (1, 1, NSA_KV_HEADS)).reshape(2, 1, CMP_LEN * LANES)
    pos = jnp.broadcast_to(pos, (2, SUBLANES, CMP_LEN * LANES)).astype(BF16)
    return pl.pallas_call(
        _nsa_compress_body,
        grid=(B, 2),
        in_specs=[pl.BlockSpec((1, T, LANES), lambda b, r: (b, 0, r)),
                  pl.BlockSpec((1,) + wd.shape[1:], lambda b, r: (r, 0, 0)),
                  pl.BlockSpec((1,) + pos.shape[1:], lambda b, r: (r, 0, 0)),
                  _const_spec((CMP_HIDDEN, HEAD_DIM)), _const_spec((HEAD_DIM, CMP_HIDDEN)),
                  _const_spec((1, HEAD_DIM))],
        out_specs=[pl.BlockSpec((1, NSA_KV_HEADS, nr, HEAD_DIM), lambda b, r: (b, 0, 0, 0)),
                   pl.BlockSpec((1, NSA_KV_HEADS, HEAD_DIM, nr), lambda b, r: (b, 0, 0, 0))],
        out_shape=[jax.ShapeDtypeStruct((B, NSA_KV_HEADS, nr, HEAD_DIM), BF16),
                   jax.ShapeDtypeStruct((B, NSA_KV_HEADS, HEAD_DIM, nr), BF16)],
        compiler_params=_params(("parallel", "arbitrary")),
        name="nsa_compress",
    )(kvb, wd, pos, cmp_w2[0].astype(BF16), cmp_w2[1].T.astype(BF16), k_norm0.reshape(1, HEAD_DIM))


def _nsa_attn_body(qT_ref, kc_ref, vcT_ref, ks_ref, vs_ref, kw_ref, vw_ref, gate_ref, selm_ref,
                   o_ref, selb_ref, gT_ref, qaug_ref, m_ref, acc_ref, sa_ref, sb_ref, oc_ref, ow_ref, *, seq):
    G, Dh, Qb = NSA_GROUP, HEAD_DIM, Q_BLOCK
    h = pl.program_id(1)
    t0 = pl.program_id(2) * Qb
    nr = seq // CMP_STRIDE
    ns = seq // SEL_BLOCK
    pos = t0 + lax.broadcasted_iota(jnp.int32, (1, Qb), 1)

    qT = qT_ref[0]
    q3 = jnp.concatenate([qT[g * Dh:(g + 1) * Dh, :] for g in range(G)], axis=1)
    qaug_ref[...] = jnp.concatenate([q3, jnp.zeros_like(q3)], axis=0)
    row = lax.broadcasted_iota(jnp.int32, (2 * Dh, G * Qb), 0)
    qpad = jnp.where((row < Dh) == (h == 0), jnp.concatenate([q3, q3], axis=0), 0).astype(BF16)

    def window_branch():
        ws = pl.multiple_of(jnp.maximum(t0 - WINDOW, 0), Qb)
        sw = jnp.dot(kw_ref[0, pl.ds(ws, WIN_KEYS), :], qpad, preferred_element_type=F32)
        vwx = vw_ref[0, 0, :, pl.ds(ws, WIN_KEYS)]
        kpos = ws + lax.broadcasted_iota(jnp.int32, (WIN_KEYS, 1), 0)
        bias_w = jnp.where((kpos <= pos) & (kpos > pos - WINDOW), 0.0, NEG)
        for g in range(G):
            s = sw[:, g * Qb:(g + 1) * Qb] + bias_w
            e = jnp.exp2(s - jnp.max(s, axis=0, keepdims=True)).astype(BF16)
            ow_ref[g] = jnp.dot(vwx, e, preferred_element_type=F32)

    def compressed_and_select(n_c):
        window_branch()
        n_s = n_c * CMP_STRIDE // SEL_BLOCK
        sc = jnp.dot(kc_ref[0, 0, :n_c, :], q3, preferred_element_type=F32)
        cmp_end = lax.broadcasted_iota(jnp.int32, (n_c, 1), 0) * CMP_STRIDE + (CMP_LEN - 1)
        bias_c = jnp.where(cmp_end <= pos, 0.0, NEG)
        has_c = pos >= CMP_LEN - 1
        lhs = jnp.concatenate([vcT_ref[0, 0, :, :n_c], selm_ref[:n_s, :n_c]], axis=0)
        imp = jnp.zeros((n_s, Qb), F32)
        for g in range(G):
            s = sc[:, g * Qb:(g + 1) * Qb] + bias_c
            e = jnp.exp2(s - jnp.max(s, axis=0, keepdims=True))
            p = (e * jnp.where(has_c, 1.0 / jnp.sum(e, axis=0, keepdims=True), 0.0)).astype(BF16)
            both = jnp.dot(lhs, p, preferred_element_type=F32)
            oc_ref[g] = both[:Dh]
            imp = imp + both[Dh:]

        blk = lax.broadcasted_iota(jnp.int32, (n_s, Qb), 0)
        cur = pos // SEL_BLOCK
        forced = (blk == 0) | (blk == cur) | (blk == cur - 1)
        score = jnp.where(forced | (blk > cur), -jnp.inf, imp)
        for _ in range(SEL_TOPK - 3):
            best = jnp.max(score, axis=0, keepdims=True)
            first = jnp.min(jnp.where(score == best, blk, n_s), axis=0, keepdims=True)
            score = jnp.where(blk == first, -jnp.inf, score)
        selb_ref[:n_s, :] = jnp.where((score == -jnp.inf) & (blk <= cur), 0.0, NEG)

    chunk = min(CMP_CHUNK, nr)
    need = (t0 + Qb) // CMP_STRIDE
    for k in range(1, nr // chunk + 1):
        pl.when((need + chunk - 1) // chunk == k)(functools.partial(compressed_and_select, k * chunk))

    bpt = KEY_TILE // SEL_BLOCK
    krow = lax.broadcasted_iota(jnp.int32, (KEY_TILE, 1), 0)
    m_ref[...] = jnp.full(m_ref.shape, NEG, F32)
    acc_ref[...] = jnp.zeros(acc_ref.shape, F32)

    def scores(kt, slot_ref, nk=KEY_TILE):
        base = pl.multiple_of(kt * KEY_TILE, KEY_TILE)
        sb = selb_ref[pl.ds(pl.multiple_of(kt * bpt, bpt), bpt), :]
        qaug_ref[Dh:Dh + bpt, :] = jnp.concatenate([sb] * G, axis=1).astype(BF16)
        slot_ref[:nk, :] = jnp.dot(ks_ref[0, 0, pl.ds(base, nk), :], qaug_ref[...],
                                   preferred_element_type=F32)

    def consume(kt, slot_ref, diagonal, nk=KEY_TILE):
        base = pl.multiple_of(kt * KEY_TILE, KEY_TILE)
        vx = vs_ref[0, 0, :, pl.ds(base, nk)]
        for g in range(G):
            s = slot_ref[:nk, g * Qb:(g + 1) * Qb]
            if diagonal:
                s = jnp.where(base + krow[:nk] <= pos, s, NEG)
            m_old = m_ref[g:g + 1, :]
            m_new = jnp.maximum(m_old, jnp.max(s, axis=0, keepdims=True))
            e = jnp.exp2(s - m_new).astype(BF16)
            acc_ref[g] = jnp.exp2(m_old - m_new) * acc_ref[g] + jnp.dot(vx, e, preferred_element_type=F32)
            m_ref[g:g + 1, :] = m_new

    n_off = (t0 + Qb + KEY_TILE - 1) // KEY_TILE - 1
    scores(0, sa_ref)

    def pair(p, carry):
        scores(2 * p + 1, sb_ref)
        consume(2 * p, sa_ref, False)
        scores(2 * p + 2, sa_ref)
        consume(2 * p + 1, sb_ref, False)
        return carry

    lax.fori_loop(0, n_off // 2, pair, 0)

    def last_odd(nk):
        scores(n_off, sb_ref, nk)
        consume(n_off - 1, sa_ref, False)
        consume(n_off, sb_ref, True, nk)

    def last_even(nk):
        consume(n_off, sa_ref, True, nk)

    q_in_tile = (t0 % KEY_TILE) // Qb
    for r in range(KEY_TILE // Qb):
        pl.when((n_off % 2 == 1) & (q_in_tile == r))(functools.partial(last_odd, (r + 1) * Qb))
        pl.when((n_off % 2 == 0) & (q_in_tile == r))(functools.partial(last_even, (r + 1) * Qb))

    gT_ref[...] = _sigmoid(gate_ref[0]).T
    gT = gT_ref[0:3 * SUBLANES, :]
    for g in range(G):
        r0 = g * 3
        gates = jnp.where(h == 0, gT[r0:r0 + 3], gT[G * 3 + r0:G * 3 + r0 + 3])
        a_s, a_w = acc_ref[g], ow_ref[g]
        o_s = a_s[:Dh] * (1.0 / a_s[Dh:Dh + 1])
        o_w = a_w[:Dh] * (1.0 / a_w[Dh:Dh + 1])
        o_ref[0, g * Dh:(g + 1) * Dh, :] = gates[0:1] * oc_ref[g] + gates[1:2] * o_s + gates[2:3] * o_w


def _nsa_attn(qT, kc, vcT, ks, vs, kw, vw, gate):
    B, _, T = qT.shape
    assert T % KEY_TILE == 0 and T >= WIN_KEYS
    nr, ns = T // CMP_STRIDE, T // SEL_BLOCK
    G, Dh, Qb = NSA_GROUP, HEAD_DIM, Q_BLOCK
    j = jnp.arange(ns)[:, None]
    c = jnp.arange(nr)[None, :]
    ratio = SEL_BLOCK // CMP_STRIDE
    selm = ((c >= ratio * j - 1) & (c <= ratio * j + ratio - 1) & (c < nr - 1)).astype(BF16)
    vfull = pl.BlockSpec((1, 1, V_ROWS, T), lambda b, h, i: (b, h, 0, 0))
    return pl.pallas_call(
        functools.partial(_nsa_attn_body, seq=T),
        grid=(B, NSA_KV_HEADS, T // Qb),
        in_specs=[pl.BlockSpec((1, G * Dh, Qb), lambda b, h, i: (b, h, i)),
                  pl.BlockSpec((1, 1, nr, Dh), lambda b, h, i: (b, h, 0, 0)),
                  pl.BlockSpec((1, 1, Dh, nr), lambda b, h, i: (b, h, 0, 0)),
                  pl.BlockSpec((1, 1, T, LANES), lambda b, h, i: (b, h, 0, 0)), vfull,
                  pl.BlockSpec((1, T, LANES), lambda b, h, i: (b, 0, 0)), vfull,
                  pl.BlockSpec((1, Qb, LANES), lambda b, h, i: (b, i, 0)),
                  _const_spec(selm.shape)],
        out_specs=pl.BlockSpec((1, G * Dh, Qb), lambda b, h, i: (b, h, i)),
        out_shape=jax.ShapeDtypeStruct((B, NSA_WIDTH, T), F32),
        scratch_shapes=[pltpu.VMEM((ns, Qb), F32), pltpu.VMEM((LANES, Qb), F32),
                        pltpu.VMEM((2 * Dh, G * Qb), BF16), pltpu.VMEM((SUBLANES, Qb), F32),
                        pltpu.VMEM((G, V_ROWS, Qb), F32),
                        pltpu.VMEM((KEY_TILE, G * Qb), F32), pltpu.VMEM((KEY_TILE, G * Qb), F32),
                        pltpu.VMEM((G, Dh, Qb), F32), pltpu.VMEM((G, V_ROWS, Qb), F32)],
        compiler_params=_params(("parallel", "parallel", "arbitrary")),
        name="nsa_attn",
    )(qT, kc, vcT, ks, vs, kw, vw, gate, selm)


POOL_HALO = 16
FF_TILE = 1024


def _pool_tile(u, halo, first_pos, w, scale):
    tm = u.shape[0]
    ue = jnp.concatenate([halo, u], axis=0)
    grp = lax.broadcasted_iota(jnp.int32, (1, POOL_WIDTH), 1) // POOL_GROUP_DIM
    acc = ue
    wsum = jnp.zeros_like(ue)
    width = jnp.zeros((1, POOL_WIDTH), F32)
    for gi, win in enumerate(POOL_WINDOWS):
        acc = acc + pltpu.roll(acc, win // 2, 0)
        wsum = jnp.where(grp == gi, acc, wsum)
        width = jnp.where(grp == gi, float(win), width)
    t1 = (first_pos + 1 + lax.broadcasted_iota(jnp.int32, (tm, 1), 0)).astype(F32)
    mean = wsum[POOL_HALO:] / jnp.minimum(t1, width)
    return jnp.dot((mean - u).astype(BF16), w, preferred_element_type=F32) * scale


def _post_body(x_ref, ya_ref, ybT_ref, u_ref, halo_ref, wa_ref, wb_ref, wc_ref, pw_ref, psc_ref, g_ref,
               w1_ref, w2_ref, o_ref):
    i = pl.program_id(1)
    tm = x_ref.shape[1]
    halo = jnp.where(i > 0, halo_ref[0], 0.0)
    yc = _pool_tile(u_ref[0], halo, i * tm, pw_ref[...], psc_ref[...])
    x = x_ref[0]
    x = x + jnp.dot(ya_ref[0].astype(BF16), wa_ref[...], preferred_element_type=F32)
    x = x + jnp.dot(ybT_ref[0].T.astype(BF16), wb_ref[...], preferred_element_type=F32)
    x = x + jnp.dot(yc.astype(BF16), wc_ref[...], preferred_element_type=F32)
    o_ref[0] = x
    x = o_ref[0]
    h = _rms(x, g_ref[...]).astype(BF16)
    acc = x
    for j in range(D_FF // FF_TILE):
        cols = slice(j * FF_TILE, (j + 1) * FF_TILE)
        a = jnp.maximum(jnp.dot(h, w1_ref[:, cols], preferred_element_type=F32), 0.0)
        acc = acc + jnp.dot((a * a).astype(BF16), w2_ref[cols, :], preferred_element_type=F32)
    o_ref[0] = acc


def _post(x, ya, ybT, u, pool_w, pool_scale, w_out, g, w1, w2, tm=512):
    B, T, _ = x.shape
    w = w_out.astype(BF16)
    wa, wb, wc = w[:GDN_WIDTH], w[GDN_WIDTH:GDN_WIDTH + NSA_WIDTH], w[GDN_WIDTH + NSA_WIDTH:]
    wblk = jax.scipy.linalg.block_diag(*[pool_w[gi] for gi in range(len(POOL_WINDOWS))]).astype(BF16)
    tok = lambda c: pl.BlockSpec((1, tm, c), lambda b, i: (b, i, 0))
    hb = tm // POOL_HALO
    return pl.pallas_call(
        _post_body,
        grid=(B, T // tm),
        in_specs=[tok(D_MODEL), tok(GDN_WIDTH), pl.BlockSpec((1, NSA_WIDTH, tm), lambda b, i: (b, 0, i)),
                  tok(POOL_WIDTH),
                  pl.BlockSpec((1, POOL_HALO, POOL_WIDTH), lambda b, i: (b, jnp.maximum(i * hb - 1, 0), 0)),
                  _const_spec(wa.shape), _const_spec(wb.shape), _const_spec(wc.shape),
                  _const_spec(wblk.shape), _const_spec((1, POOL_WIDTH)), _const_spec((1, D_MODEL)),
                  _const_spec(w1.shape), _const_spec(w2.shape)],
        out_specs=tok(D_MODEL),
        out_shape=jax.ShapeDtypeStruct(x.shape, F32),
        compiler_params=_params(("parallel", "parallel")),
        name="post",
    )(x, ya, ybT, u, u, wa, wb, wc, wblk, pool_scale.reshape(1, POOL_WIDTH), g.reshape(1, D_MODEL),
      w1.astype(BF16), w2.astype(BF16))


def _pad_in_weights(w):
    H = GDN_HEADS
    o_ba = 4 * GDN_WIDTH
    o_q = o_ba + 2 * H
    o_gate = o_q + NSA_WIDTH + 6 * LANES
    o_u = o_gate + 3 * NSA_HEADS
    w = w.astype(BF16)
    pad = lambda a: jnp.pad(a, [(0, 0)] * (a.ndim - 1) + [(0, LANES - a.shape[-1])])
    return jnp.concatenate([w[..., :o_ba], pad(w[..., o_ba:o_q]), w[..., o_q:o_gate], pad(w[..., o_gate:o_u]),
                            w[..., o_u:]], axis=-1)


def _mixers(qkv, z, ba, qb, kvb, gate, conv_w, a_log, dt_bias, gdn_norm, nsa_q_norm, nsa_k_norm,
            cmp_pos, cmp_w1, cmp_w2):
    q, k, v, be, ge = _gdn_prep(qkv, ba, conv_w, a_log, dt_bias)
    ya = _gdn_scan(q, k, v, be, ge, z, gdn_norm)
    qT, ks, kw, vs, vw = _nsa_prep(qb, kvb, nsa_q_norm, nsa_k_norm)
    kc, vcT = _nsa_compress(kvb, cmp_pos, cmp_w1, cmp_w2, nsa_k_norm[0])
    ybT = _nsa_attn(qT, kc, vcT, ks, vs, kw, vw, gate)
    return ya, ybT


def kernel(x, norm_mix, w_in, conv_w, a_log, dt_bias, gdn_norm, nsa_q_norm, nsa_k_norm, cmp_pos, cmp_w1, cmp_w2,
           pool_w, pool_scale, w_out, norm_ffn, w_ffn1, w_ffn2):
    B, T, D = x.shape
    w_in_pad = _pad_in_weights(w_in)
    for l in range(w_in.shape[0]):
        proj = _inproj(x.reshape(B * T, D), norm_mix[l], w_in_pad[l])
        qkv, z, ba, qb, kvb, gate, u = [p.reshape(B, T, -1) for p in proj]
        ya, ybT = _mixers(qkv, z, ba, qb, kvb, gate, conv_w[l], a_log[l], dt_bias[l], gdn_norm[l],
                          nsa_q_norm[l], nsa_k_norm[l], cmp_pos[l], cmp_w1[l], cmp_w2[l])
        x = _post(x, ya, ybT, u, pool_w[l], pool_scale[l], w_out[l], norm_ffn[l], w_ffn1[l], w_ffn2[l])
    return x
```

```python
import functools

import jax
import jax.numpy as jnp
from jax import lax
from jax.experimental import pallas as pl
from jax.experimental.pallas import tpu as pltpu

F32 = jnp.float32
BF16 = jnp.bfloat16

D_MODEL = 1024
HEAD_DIM = 64
GDN_HEADS = 6
GDN_CHUNK = 64
GDN_WIDTH = GDN_HEADS * HEAD_DIM
NSA_HEADS = 6
NSA_KV_HEADS = 2
NSA_GROUP = NSA_HEADS // NSA_KV_HEADS
NSA_WIDTH = NSA_HEADS * HEAD_DIM
CMP_LEN = 32
CMP_STRIDE = 16
CMP_HIDDEN = 256
SEL_BLOCK = 64
SEL_TOPK = 16
WINDOW = 512
Q_BLOCK = 256
FORCED_SCORE = 1e4
POOL_WINDOWS = (2, 4, 8, 16)
POOL_GROUP_DIM = 64
POOL_WIDTH = 4 * POOL_GROUP_DIM
D_FF = 4 * D_MODEL
EPS = 1e-6

LANES = 128
SUBLANES = 8
VMEM_LIMIT = 56 * 1024 * 1024
NEG = -1e30
LOG2E = 1.4426950408889634
KEY_TILE = 1024
WIN_KEYS = WINDOW + Q_BLOCK
CMP_CHUNK = 256
V_ROWS = 80

IN_GROUPS = (3 * GDN_WIDTH, GDN_WIDTH, LANES, NSA_WIDTH, 6 * LANES, LANES, POOL_WIDTH)


def _sigmoid(x):
    return 1.0 / (1.0 + jnp.exp(-x))


def _silu(x):
    return x * _sigmoid(x)


def _softplus(x):
    return jnp.maximum(x, 0.0) + jnp.log(1.0 + jnp.exp(-jnp.abs(x)))


def _split_bf16(x):
    hi = x.astype(BF16)
    lo = (x - hi.astype(F32)).astype(BF16)
    return hi, lo


def _dot_x_exact(x, m):
    hi, lo = _split_bf16(x)
    return jnp.dot(hi, m, preferred_element_type=F32) + jnp.dot(lo, m, preferred_element_type=F32)


def _dot_exact_x(m, x):
    hi, lo = _split_bf16(x)
    return jnp.dot(m, hi, preferred_element_type=F32) + jnp.dot(m, lo, preferred_element_type=F32)


def _mm(a, b):
    return jnp.dot(a.astype(BF16), b.astype(BF16), preferred_element_type=F32)


def _mm_nt(a, b):
    return lax.dot_general(a.astype(BF16), b.astype(BF16), (((1,), (1,)), ((), ())),
                           preferred_element_type=F32)


def _mm_tn(a, b):
    return lax.dot_general(a.astype(BF16), b.astype(BF16), (((0,), (0,)), ((), ())),
                           preferred_element_type=F32)


def _rms(x, g):
    return x * lax.rsqrt(jnp.mean(x * x, axis=-1, keepdims=True) + EPS) * g


def _const_spec(shape):
    nd = len(shape)
    return pl.BlockSpec(shape, lambda *_: (0,) * nd)


def _resident_spec(shape):
    nd = len(shape)
    return pl.BlockSpec(shape, lambda *_: (0,) * nd, pipeline_mode=pl.Buffered(1))


def _params(sem):
    return pltpu.CompilerParams(dimension_semantics=sem, vmem_limit_bytes=VMEM_LIMIT)


def _inproj_body(x_ref, g_ref, w_ref, *out_refs):
    h = _rms(x_ref[...], g_ref[...]).astype(BF16)
    off = 0
    for o_ref in out_refs:
        n = o_ref.shape[-1]
        o_ref[...] = jnp.dot(h, w_ref[:, off:off + n], preferred_element_type=F32)
        off += n


def _inproj(xf, g, w_pad, tm=1024):
    n = xf.shape[0]
    d_in = w_pad.shape[1]
    return pl.pallas_call(
        _inproj_body,
        grid=(n // tm,),
        in_specs=[pl.BlockSpec((tm, D_MODEL), lambda i: (i, 0)),
                  _const_spec((1, D_MODEL)),
                  _const_spec((D_MODEL, d_in))],
        out_specs=[pl.BlockSpec((tm, c), lambda i: (i, 0)) for c in IN_GROUPS],
        out_shape=[jax.ShapeDtypeStruct((n, c), F32) for c in IN_GROUPS],
        compiler_params=_params(("parallel",)),
        name="inproj",
    )(xf, g.reshape(1, D_MODEL), w_pad)


def _gdn_inputs(x, tail, ba, cw, alog, dtb, seg, eb, eg):
    xe = jnp.concatenate([tail, x], axis=0)
    y = x * cw[3:4, :]
    for s in (1, 2, 3):
        y = y + pltpu.roll(xe, s, 0)[SUBLANES:] * cw[3 - s:4 - s, :]
    a = _silu(y)
    q = a[:, :GDN_WIDTH]
    k = a[:, GDN_WIDTH:2 * GDN_WIDTH]
    qn = q * lax.rsqrt(_dot_x_exact(q * q, seg) + EPS) * (HEAD_DIM ** -0.5)
    kn = k * lax.rsqrt(_dot_x_exact(k * k, seg) + EPS)
    gval = -jnp.exp(alog) * _softplus(ba + dtb)
    return qn, kn, a[:, 2 * GDN_WIDTH:], _dot_x_exact(_sigmoid(ba), eb), _dot_x_exact(gval, eg)


def _gdn_body(qkv_ref, ba_ref, z_ref, cw_ref, alog_ref, dtb_ref, seg_ref, eb_ref, eg_ref, tri_ref, gn_ref,
              o_ref, s_ref, tail_ref, *, cps):
    C = GDN_CHUNK
    P = 2 * C
    B = qkv_ref.shape[0]
    NP = GDN_WIDTH // LANES
    tm = qkv_ref.shape[1]

    @pl.when(pl.program_id(0) == 0)
    def _():
        s_ref[...] = jnp.zeros_like(s_ref)
        tail_ref[...] = jnp.zeros_like(tail_ref)

    feats = []
    for b in range(B):
        x = qkv_ref[b]
        feats.append(_gdn_inputs(x, tail_ref[b], ba_ref[b], cw_ref[...], alog_ref[...], dtb_ref[...],
                                 seg_ref[...], eb_ref[...], eg_ref[...]))
        tail_ref[b] = x[tm - SUBLANES:, :]

    lo_lane = lax.broadcasted_iota(jnp.int32, (C, LANES), 1) < HEAD_DIM
    r = lax.broadcasted_iota(jnp.int32, (P, P), 0)
    c = lax.broadcasted_iota(jnp.int32, (P, P), 1)
    same = (r < C) == (c < C)
    causal = same & (r >= c)
    strict = same & (r > c)
    eye = (r == c).astype(F32)
    tri = tri_ref[...]
    gn = gn_ref[...]

    def blk(full, j, p):
        x = full[C * j:C * (j + 1), LANES * p:LANES * (p + 1)]
        return jnp.concatenate([jnp.where(lo_lane, x, 0.0), jnp.where(lo_lane, 0.0, x)], axis=0)

    chains = [(b, p) for b in range(B) for p in range(NP)]
    units = [(b, p, j) for j in range(cps) for (b, p) in chains]
    st = {}
    for u in units:
        b, p, j = u
        q2, k2, v2, b2, g2 = [blk(f, j, p) for f in feats[b]]
        g_hi, g_lo = _split_bf16(g2)
        gcc = jnp.dot(tri, jnp.concatenate([g_hi, g_lo], axis=1), preferred_element_type=F32)
        gc = gcc[:, :LANES] + gcc[:, LANES:]
        decay = jnp.exp(jnp.where(causal, gc - gc.T, NEG))
        kb = k2 * b2
        gram = _mm_nt(jnp.concatenate([kb, q2], axis=0), k2)
        egc = jnp.exp(gc)
        glast = jnp.concatenate([jnp.broadcast_to(gc[C - 1:C, :], (C, LANES)),
                                 jnp.broadcast_to(gc[P - 1:P, :], (C, LANES))], axis=0)
        st[u] = dict(L=jnp.where(strict, gram[:P] * decay, 0.0), qk=gram[P:] * decay,
                     rhs=jnp.concatenate([v2 * b2, kb * egc], axis=1), qd=q2 * egc,
                     kdT=(k2 * jnp.exp(glast - gc)).T, eg=jnp.exp(glast))

    for u in units:
        st[u]["X"] = eye
    s = 1
    while s < C:
        couple = same & ((r // (2 * s)) == (c // (2 * s))) & (((r // s) % 2) == 1) & (((c // s) % 2) == 0)
        for u in units:
            X = st[u]["X"]
            st[u]["X"] = X - _mm(X, _mm(jnp.where(couple, st[u]["L"], 0.0), X))
        s *= 2
    for u in units:
        st[u].pop("L")
        uw = _mm(st[u].pop("X"), st[u].pop("rhs"))
        st[u]["u"], st[u]["w"] = uw[:, :LANES], uw[:, LANES:]

    S = {ch: s_ref[ch[0], ch[1]] for ch in chains}
    for j in range(cps):
        for ch in chains:
            b, p = ch
            d = st[(b, p, j)]
            ws = _mm(jnp.concatenate([d["w"], d["qd"]], axis=0), S[ch])
            vnew = d["u"] - ws[:P]
            ov = _mm(jnp.concatenate([d["qk"], d["kdT"]], axis=0), vnew)
            o2 = ws[P:] + ov[:P]
            S[ch] = S[ch] * d["eg"] + ov[P:]
            ms = jnp.sum(o2 * o2, axis=-1, keepdims=True) * (1.0 / HEAD_DIM)
            on = o2 * lax.rsqrt(ms + EPS) * gn
            z = z_ref[b, C * j:C * (j + 1), LANES * p:LANES * (p + 1)]
            o_ref[b, C * j:C * (j + 1), LANES * p:LANES * (p + 1)] = (on[:C] + on[C:]) * _silu(z)
    for ch in chains:
        s_ref[ch[0], ch[1]] = S[ch]


def _gdn(qkv, ba, z, conv_w, a_log, dt_bias, gdn_norm, cps=2):
    B, T, width = qkv.shape
    C = GDN_CHUNK
    H = GDN_HEADS
    tm = C * cps
    alog_p = jnp.zeros((1, LANES), F32).at[0, H:2 * H].set(a_log)
    dtb_p = jnp.zeros((1, LANES), F32).at[0, H:2 * H].set(dt_bias)
    ch = jnp.arange(GDN_WIDTH) // HEAD_DIM
    seg = (ch[:, None] == ch[None, :]).astype(BF16)
    row = jnp.arange(LANES)
    eb = (row[:, None] == ch[None, :]).astype(BF16)
    eg = (row[:, None] - H == ch[None, :]).astype(BF16)
    i2 = jnp.arange(2 * C)
    tri = (((i2[:, None] // C) == (i2[None, :] // C)) & (i2[:, None] >= i2[None, :])).astype(BF16)
    gn2 = jnp.tile(gdn_norm, 2).reshape(1, LANES)
    tok = lambda c: pl.BlockSpec((B, tm, c), lambda i: (0, i, 0))
    consts = (conv_w, alog_p, dtb_p, seg, eb, eg, tri, gn2)
    return pl.pallas_call(
        functools.partial(_gdn_body, cps=cps),
        grid=(T // tm,),
        in_specs=[tok(width), tok(LANES), tok(GDN_WIDTH)] + [_const_spec(c.shape) for c in consts],
        out_specs=tok(GDN_WIDTH),
        out_shape=jax.ShapeDtypeStruct((B, T, GDN_WIDTH), F32),
        scratch_shapes=[pltpu.VMEM((B, GDN_WIDTH // LANES, 2 * C, LANES), F32),
                        pltpu.VMEM((B, SUBLANES, width), F32)],
        compiler_params=_params(("arbitrary",)),
        name="gdn",
    )(qkv, ba, z, *consts)


def _nsa_prep_body(q_ref, kv_ref, segq_ref, segk_ref, qn_ref, kn_ref,
                   qT_ref, ks_ref, kw_ref, vs_ref, vw_ref):
    tm = q_ref.shape[1]
    q = q_ref[0]
    ms = _dot_x_exact(q * q, segq_ref[...]) * (1.0 / HEAD_DIM)
    qn = q * lax.rsqrt(ms + EPS) * qn_ref[...] * (HEAD_DIM ** -0.5 * LOG2E)
    qT_ref[0] = qn.T.astype(BF16)
    kv = kv_ref[0]
    segk = segk_ref[...]
    kn = kn_ref[...]

    def knorm(k, g):
        msk = _dot_x_exact(k * k, segk) * (1.0 / HEAD_DIM)
        return k * lax.rsqrt(msk + EPS) * g

    ks = knorm(kv[:, 2 * LANES:3 * LANES], kn[0:1, :])
    lane = lax.broadcasted_iota(jnp.int32, (tm, LANES), 1)
    tok = pl.program_id(1) * tm + lax.broadcasted_iota(jnp.int32, (tm, LANES), 0)
    onehot = (lane - HEAD_DIM == (tok // SEL_BLOCK) % (KEY_TILE // SEL_BLOCK)).astype(F32)
    ks_ref[0, 0] = jnp.where(lane < HEAD_DIM, ks, onehot).astype(BF16)
    ks_ref[0, 1] = jnp.where(lane < HEAD_DIM, pltpu.roll(ks, HEAD_DIM, 1), onehot).astype(BF16)
    kw_ref[0] = knorm(kv[:, 4 * LANES:5 * LANES], kn[1:2, :]).astype(BF16)
    ones_rows = (lax.broadcasted_iota(jnp.int32, (V_ROWS - HEAD_DIM, tm), 0) == 0).astype(BF16)
    for v_ref, cols in ((vs_ref, 3), (vw_ref, 5)):
        vT = kv[:, cols * LANES:(cols + 1) * LANES].T.astype(BF16)
        for h in range(NSA_KV_HEADS):
            v_ref[0, h, :HEAD_DIM, :] = vT[h * HEAD_DIM:(h + 1) * HEAD_DIM, :]
            v_ref[0, h, HEAD_DIM:, :] = ones_rows


def _nsa_prep(qb, kvb, q_norm, k_norm, tm=512):
    B, T, _ = qb.shape
    chq = jnp.arange(NSA_WIDTH) // HEAD_DIM
    segq = (chq[:, None] == chq[None, :]).astype(BF16)
    segk = segq[:LANES, :LANES]
    qn = jnp.tile(q_norm, NSA_HEADS).reshape(1, NSA_WIDTH)
    kn = jnp.tile(k_norm[1:3], (1, NSA_KV_HEADS))
    tok = lambda c: pl.BlockSpec((1, tm, c), lambda b, i: (b, i, 0))
    vspec = pl.BlockSpec((1, NSA_KV_HEADS, V_ROWS, tm), lambda b, i: (b, 0, 0, i))
    vshape = jax.ShapeDtypeStruct((B, NSA_KV_HEADS, V_ROWS, T), BF16)
    return pl.pallas_call(
        _nsa_prep_body,
        grid=(B, T // tm),
        in_specs=[tok(NSA_WIDTH), tok(6 * LANES), _const_spec(segq.shape), _const_spec(segk.shape),
                  _const_spec(qn.shape), _const_spec(kn.shape)],
        out_specs=[pl.BlockSpec((1, NSA_WIDTH, tm), lambda b, i: (b, 0, i)),
                   pl.BlockSpec((1, NSA_KV_HEADS, tm, LANES), lambda b, i: (b, 0, i, 0)),
                   tok(LANES), vspec, vspec],
        out_shape=[jax.ShapeDtypeStruct((B, NSA_WIDTH, T), BF16),
                   jax.ShapeDtypeStruct((B, NSA_KV_HEADS, T, LANES), BF16),
                   jax.ShapeDtypeStruct((B, T, LANES), BF16),
                   vshape, vshape],
        compiler_params=_params(("parallel", "parallel")),
        name="nsa_prep",
    )(qb, kvb, segq, segk, qn, kn)


def _nsa_compress_body(x_ref, wd_ref, pos_ref, w2k_ref, w2vT_ref, kn_ref, kc_ref, vcT_ref):
    branch = pl.program_id(1)
    nr = x_ref.shape[1] // CMP_STRIDE
    hid2 = NSA_KV_HEADS * CMP_HIDDEN
    ya = jnp.zeros((nr, hid2), F32)
    yb = jnp.zeros((nr, hid2), F32)
    for l in range(CMP_STRIDE):
        xl = x_ref[0, pl.ds(l, nr, stride=CMP_STRIDE), :].astype(BF16)
        ya = ya + jnp.dot(xl, wd_ref[0, l * LANES:(l + 1) * LANES, :], preferred_element_type=F32)
        yb = yb + jnp.dot(xl, wd_ref[0, (CMP_STRIDE + l) * LANES:(CMP_STRIDE + l + 1) * LANES, :],
                          preferred_element_type=F32)
    bias = jnp.dot(pos_ref[0], wd_ref[0], preferred_element_type=F32)[0:1, :]
    hidden = _silu(ya + pltpu.roll(yb, nr - 1, 0) + bias).astype(BF16)

    @pl.when(branch == 0)
    def _():
        for h in range(NSA_KV_HEADS):
            kc = jnp.dot(hidden[:, h * CMP_HIDDEN:(h + 1) * CMP_HIDDEN], w2k_ref[...], preferred_element_type=F32)
            kc_ref[0, h] = _rms(kc, kn_ref[...]).astype(BF16)

    @pl.when(branch == 1)
    def _():
        for h in range(NSA_KV_HEADS):
            vcT_ref[0, h] = lax.dot_general(w2vT_ref[...], hidden[:, h * CMP_HIDDEN:(h + 1) * CMP_HIDDEN],
                                            (((1,), (1,)), ((), ())), preferred_element_type=F32).astype(BF16)


def _nsa_compress(kvb, cmp_pos, cmp_w1, cmp_w2, k_norm0):
    B, T, _ = kvb.shape
    nr = T // CMP_STRIDE
    w1 = cmp_w1.reshape(2, CMP_LEN, HEAD_DIM, CMP_HIDDEN)
    z = jnp.zeros_like(w1)
    wd = jnp.concatenate([jnp.concatenate([w1, z], axis=3), jnp.concatenate([z, w1], axis=3)], axis=2)
    wd = wd.reshape(2, CMP_LEN * LANES, NSA_KV_HEADS * CMP_HIDDEN).astype(BF16)
    pos = jnp.tile(cmp_pos, (1, 1, NSA_KV_HEADS)).reshape(2, 1, CMP_LEN * LANES)
    pos = jnp.broadcast_to(pos, (2, SUBLANES, CMP_LEN * LANES)).astype(BF16)
    return pl.pallas_call(
        _nsa_compress_body,
        grid=(B, 2),
        in_specs=[pl.BlockSpec((1, T, LANES), lambda b, r: (b, 0, r)),
                  pl.BlockSpec((1,) + wd.shape[1:], lambda b, r: (r, 0, 0)),
                  pl.BlockSpec((1,) + pos.shape[1:], lambda b, r: (r, 0, 0)),
                  _const_spec((CMP_HIDDEN, HEAD_DIM)), _const_spec((HEAD_DIM, CMP_HIDDEN)),
                  _const_spec((1, HEAD_DIM))],
        out_specs=[pl.BlockSpec((1, NSA_KV_HEADS, nr, HEAD_DIM), lambda b, r: (b, 0, 0, 0)),
                   pl.BlockSpec((1, NSA_KV_HEADS, HEAD_DIM, nr), lambda b, r: (b, 0, 0, 0))],
        out_shape=[jax.ShapeDtypeStruct((B, NSA_KV_HEADS, nr, HEAD_DIM), BF16),
                   jax.ShapeDtypeStruct((B, NSA_KV_HEADS, HEAD_DIM, nr), BF16)],
        compiler_params=_params(("parallel", "arbitrary")),
        name="nsa_compress",
    )(kvb, wd, pos, cmp_w2[0].astype(BF16), cmp_w2[1].T.astype(BF16), k_norm0.reshape(1, HEAD_DIM))


def _nsa_attn_body(qT_ref, kc_ref, vcT_ref, ks_ref, vs_ref, kw_ref, vw_ref, gate_ref, selm_ref,
                   o_ref, selb_ref, gT_ref, qaug_ref, m_ref, acc_ref, sa_ref, sb_ref, oc_ref, ow_ref, *, seq):
    G, Dh, Qb = NSA_GROUP, HEAD_DIM, Q_BLOCK
    h = pl.program_id(1)
    t0 = pl.program_id(2) * Qb
    nr = seq // CMP_STRIDE
    ns = seq // SEL_BLOCK
    pos = t0 + lax.broadcasted_iota(jnp.int32, (1, Qb), 1)

    qT = qT_ref[0]
    q3 = jnp.concatenate([qT[g * Dh:(g + 1) * Dh, :] for g in range(G)], axis=1)
    qaug_ref[...] = jnp.concatenate([q3, jnp.zeros_like(q3)], axis=0)
    row = lax.broadcasted_iota(jnp.int32, (2 * Dh, G * Qb), 0)
    qpad = jnp.where((row < Dh) == (h == 0), jnp.concatenate([q3, q3], axis=0), 0).astype(BF16)

    def window_branch():
        ws = pl.multiple_of(jnp.maximum(t0 - WINDOW, 0), Qb)
        sw = jnp.dot(kw_ref[0, pl.ds(ws, WIN_KEYS), :], qpad, preferred_element_type=F32)
        vwx = vw_ref[0, 0, :, pl.ds(ws, WIN_KEYS)]
        kpos = ws + lax.broadcasted_iota(jnp.int32, (WIN_KEYS, 1), 0)
        bias_w = jnp.where((kpos <= pos) & (kpos > pos - WINDOW), 0.0, NEG)
        for g in range(G):
            s = sw[:, g * Qb:(g + 1) * Qb] + bias_w
            e = jnp.exp2(s - jnp.max(s, axis=0, keepdims=True)).astype(BF16)
            ow_ref[g] = jnp.dot(vwx, e, preferred_element_type=F32)

    def compressed_and_select(n_c):
        window_branch()
        n_s = n_c * CMP_STRIDE // SEL_BLOCK
        sc = jnp.dot(kc_ref[0, 0, :n_c, :], q3, preferred_element_type=F32)
        cmp_end = lax.broadcasted_iota(jnp.int32, (n_c, 1), 0) * CMP_STRIDE + (CMP_LEN - 1)
        bias_c = jnp.where(cmp_end <= pos, 0.0, NEG)
        has_c = pos >= CMP_LEN - 1
        lhs = jnp.concatenate([vcT_ref[0, 0, :, :n_c], selm_ref[:n_s, :n_c]], axis=0)
        imp = jnp.zeros((n_s, Qb), F32)
        for g in range(G):
            s = sc[:, g * Qb:(g + 1) * Qb] + bias_c
            e = jnp.exp2(s - jnp.max(s, axis=0, keepdims=True))
            p = (e * jnp.where(has_c, 1.0 / jnp.sum(e, axis=0, keepdims=True), 0.0)).astype(BF16)
            both = jnp.dot(lhs, p, preferred_element_type=F32)
            oc_ref[g] = both[:Dh]
            imp = imp + both[Dh:]

        blk = lax.broadcasted_iota(jnp.int32, (n_s, Qb), 0)
        cur = pos // SEL_BLOCK
        forced = (blk == 0) | (blk == cur) | (blk == cur - 1)
        score = jnp.where(forced | (blk > cur), -jnp.inf, imp)
        for _ in range(SEL_TOPK - 3):
            best = jnp.max(score, axis=0, keepdims=True)
            first = jnp.min(jnp.where(score == best, blk, n_s), axis=0, keepdims=True)
            score = jnp.where(blk == first, -jnp.inf, score)
        selb_ref[:n_s, :] = jnp.where((score == -jnp.inf) & (blk <= cur), 0.0, NEG)

    chunk = min(CMP_CHUNK, nr)
    need = (t0 + Qb) // CMP_STRIDE
    for k in range(1, nr // chunk + 1):
        pl.when((need + chunk - 1) // chunk == k)(functools.partial(compressed_and_select, k * chunk))

    bpt = KEY_TILE // SEL_BLOCK
    krow = lax.broadcasted_iota(jnp.int32, (KEY_TILE, 1), 0)
    m_ref[...] = jnp.full(m_ref.shape, NEG, F32)
    acc_ref[...] = jnp.zeros(acc_ref.shape, F32)

    def scores(kt, slot_ref, nk=KEY_TILE):
        base = pl.multiple_of(kt * KEY_TILE, KEY_TILE)
        sb = selb_ref[pl.ds(pl.multiple_of(kt * bpt, bpt), bpt), :]
        qaug_ref[Dh:Dh + bpt, :] = jnp.concatenate([sb] * G, axis=1).astype(BF16)
        slot_ref[:nk, :] = jnp.dot(ks_ref[0, 0, pl.ds(base, nk), :], qaug_ref[...],
                                   preferred_element_type=F32)

    def consume(kt, slot_ref, diagonal, nk=KEY_TILE):
        base = pl.multiple_of(kt * KEY_TILE, KEY_TILE)
        vx = vs_ref[0, 0, :, pl.ds(base, nk)]
        for g in range(G):
            s = slot_ref[:nk, g * Qb:(g + 1) * Qb]
            if diagonal:
                s = jnp.where(base + krow[:nk] <= pos, s, NEG)
            m_old = m_ref[g:g + 1, :]
            m_new = jnp.maximum(m_old, jnp.max(s, axis=0, keepdims=True))
            e = jnp.exp2(s - m_new).astype(BF16)
            acc_ref[g] = jnp.exp2(m_old - m_new) * acc_ref[g] + jnp.dot(vx, e, preferred_element_type=F32)
            m_ref[g:g + 1, :] = m_new

    n_off = (t0 + Qb + KEY_TILE - 1) // KEY_TILE - 1
    scores(0, sa_ref)

    def pair(p, carry):
        scores(2 * p + 1, sb_ref)
        consume(2 * p, sa_ref, False)
        scores(2 * p + 2, sa_ref)
        consume(2 * p + 1, sb_ref, False)
        return carry

    lax.fori_loop(0, n_off // 2, pair, 0)

    def last_odd(nk):
        scores(n_off, sb_ref, nk)
        consume(n_off - 1, sa_ref, False)
        consume(n_off, sb_ref, True, nk)

    def last_even(nk):
        consume(n_off, sa_ref, True, nk)

    q_in_tile = (t0 % KEY_TILE) // Qb
    for r in range(KEY_TILE // Qb):
        pl.when((n_off % 2 == 1) & (q_in_tile == r))(functools.partial(last_odd, (r + 1) * Qb))
        pl.when((n_off % 2 == 0) & (q_in_tile == r))(functools.partial(last_even, (r + 1) * Qb))

    gT_ref[...] = _sigmoid(gate_ref[0]).T
    gT = gT_ref[0:3 * SUBLANES, :]
    for g in range(G):
        r0 = g * 3
        gates = jnp.where(h == 0, gT[r0:r0 + 3], gT[G * 3 + r0:G * 3 + r0 + 3])
        a_s, a_w = acc_ref[g], ow_ref[g]
        o_s = a_s[:Dh] * (1.0 / a_s[Dh:Dh + 1])
        o_w = a_w[:Dh] * (1.0 / a_w[Dh:Dh + 1])
        o_ref[0, g * Dh:(g + 1) * Dh, :] = gates[0:1] * oc_ref[g] + gates[1:2] * o_s + gates[2:3] * o_w


def _nsa_attn(qT, kc, vcT, ks, vs, kw, vw, gate):
    B, _, T = qT.shape
    assert T % KEY_TILE == 0 and T >= WIN_KEYS
    nr, ns = T // CMP_STRIDE, T // SEL_BLOCK
    G, Dh, Qb = NSA_GROUP, HEAD_DIM, Q_BLOCK
    j = jnp.arange(ns)[:, None]
    c = jnp.arange(nr)[None, :]
    ratio = SEL_BLOCK // CMP_STRIDE
    selm = ((c >= ratio * j - 1) & (c <= ratio * j + ratio - 1) & (c < nr - 1)).astype(BF16)
    vfull = pl.BlockSpec((1, 1, V_ROWS, T), lambda b, h, i: (b, h, 0, 0))
    return pl.pallas_call(
        functools.partial(_nsa_attn_body, seq=T),
        grid=(B, NSA_KV_HEADS, T // Qb),
        in_specs=[pl.BlockSpec((1, G * Dh, Qb), lambda b, h, i: (b, h, i)),
                  pl.BlockSpec((1, 1, nr, Dh), lambda b, h, i: (b, h, 0, 0)),
                  pl.BlockSpec((1, 1, Dh, nr), lambda b, h, i: (b, h, 0, 0)),
                  pl.BlockSpec((1, 1, T, LANES), lambda b, h, i: (b, h, 0, 0)), vfull,
                  pl.BlockSpec((1, T, LANES), lambda b, h, i: (b, 0, 0)), vfull,
                  pl.BlockSpec((1, Qb, LANES), lambda b, h, i: (b, i, 0)),
                  _const_spec(selm.shape)],
        out_specs=pl.BlockSpec((1, G * Dh, Qb), lambda b, h, i: (b, h, i)),
        out_shape=jax.ShapeDtypeStruct((B, NSA_WIDTH, T), F32),
        scratch_shapes=[pltpu.VMEM((ns, Qb), F32), pltpu.VMEM((LANES, Qb), F32),
                        pltpu.VMEM((2 * Dh, G * Qb), BF16), pltpu.VMEM((SUBLANES, Qb), F32),
                        pltpu.VMEM((G, V_ROWS, Qb), F32),
                        pltpu.VMEM((KEY_TILE, G * Qb), F32), pltpu.VMEM((KEY_TILE, G * Qb), F32),
                        pltpu.VMEM((G, Dh, Qb), F32), pltpu.VMEM((G, V_ROWS, Qb), F32)],
        compiler_params=_params(("parallel", "parallel", "arbitrary")),
        name="nsa_attn",
    )(qT, kc, vcT, ks, vs, kw, vw, gate, selm)


POOL_HALO = 16
FF_TILE = 1024


def _pool_tile(u, halo, first_pos, w, scale):
    tm = u.shape[0]
    ue = jnp.concatenate([halo, u], axis=0)
    grp = lax.broadcasted_iota(jnp.int32, (1, POOL_WIDTH), 1) // POOL_GROUP_DIM
    acc = ue
    wsum = jnp.zeros_like(ue)
    width = jnp.zeros((1, POOL_WIDTH), F32)
    for gi, win in enumerate(POOL_WINDOWS):
        acc = acc + pltpu.roll(acc, win // 2, 0)
        wsum = jnp.where(grp == gi, acc, wsum)
        width = jnp.where(grp == gi, float(win), width)
    t1 = (first_pos + 1 + lax.broadcasted_iota(jnp.int32, (tm, 1), 0)).astype(F32)
    mean = wsum[POOL_HALO:] / jnp.minimum(t1, width)
    return jnp.dot((mean - u).astype(BF16), w, preferred_element_type=F32) * scale


def _post_body(x_ref, ya_ref, ybT_ref, u_ref, halo_ref, wa_ref, wb_ref, wc_ref, pw_ref, psc_ref, g_ref,
               w1_ref, w2_ref, o_ref):
    i = pl.program_id(1)
    tm = x_ref.shape[1]
    halo = jnp.where(i > 0, halo_ref[0], 0.0)
    yc = _pool_tile(u_ref[0], halo, i * tm, pw_ref[...], psc_ref[...])
    x = x_ref[0]
    x = x + jnp.dot(ya_ref[0].astype(BF16), wa_ref[...], preferred_element_type=F32)
    x = x + jnp.dot(ybT_ref[0].T.astype(BF16), wb_ref[...], preferred_element_type=F32)
    x = x + jnp.dot(yc.astype(BF16), wc_ref[...], preferred_element_type=F32)
    o_ref[0] = x
    x = o_ref[0]
    h = _rms(x, g_ref[...]).astype(BF16)
    acc = x
    for j in range(D_FF // FF_TILE):
        cols = slice(j * FF_TILE, (j + 1) * FF_TILE)
        a = jnp.maximum(jnp.dot(h, w1_ref[:, cols], preferred_element_type=F32), 0.0)
        acc = acc + jnp.dot((a * a).astype(BF16), w2_ref[cols, :], preferred_element_type=F32)
    o_ref[0] = acc


def _post(x, ya, ybT, u, pool_w, pool_scale, w_out, g, w1, w2, tm=1024):
    B, T, _ = x.shape
    w = w_out.astype(BF16)
    wa, wb, wc = w[:GDN_WIDTH], w[GDN_WIDTH:GDN_WIDTH + NSA_WIDTH], w[GDN_WIDTH + NSA_WIDTH:]
    wblk = jax.scipy.linalg.block_diag(*[pool_w[gi] for gi in range(len(POOL_WINDOWS))]).astype(BF16)
    tok = lambda c: pl.BlockSpec((1, tm, c), lambda b, i: (b, i, 0))
    hb = tm // POOL_HALO
    return pl.pallas_call(
        _post_body,
        grid=(B, T // tm),
        in_specs=[tok(D_MODEL), tok(GDN_WIDTH), pl.BlockSpec((1, NSA_WIDTH, tm), lambda b, i: (b, 0, i)),
                  tok(POOL_WIDTH),
                  pl.BlockSpec((1, POOL_HALO, POOL_WIDTH), lambda b, i: (b, jnp.maximum(i * hb - 1, 0), 0)),
                  _const_spec(wa.shape), _const_spec(wb.shape), _const_spec(wc.shape),
                  _const_spec(wblk.shape), _const_spec((1, POOL_WIDTH)), _const_spec((1, D_MODEL)),
                  _resident_spec(w1.shape), _resident_spec(w2.shape)],
        out_specs=tok(D_MODEL),
        out_shape=jax.ShapeDtypeStruct(x.shape, F32),
        compiler_params=_params(("parallel", "parallel")),
        name="post",
    )(x, ya, ybT, u, u, wa, wb, wc, wblk, pool_scale.reshape(1, POOL_WIDTH), g.reshape(1, D_MODEL),
      w1.astype(BF16), w2.astype(BF16))


def _pad_in_weights(w):
    H = GDN_HEADS
    o_ba = 4 * GDN_WIDTH
    o_q = o_ba + 2 * H
    o_gate = o_q + NSA_WIDTH + 6 * LANES
    o_u = o_gate + 3 * NSA_HEADS
    w = w.astype(BF16)
    pad = lambda a: jnp.pad(a, [(0, 0)] * (a.ndim - 1) + [(0, LANES - a.shape[-1])])
    return jnp.concatenate([w[..., :o_ba], pad(w[..., o_ba:o_q]), w[..., o_q:o_gate], pad(w[..., o_gate:o_u]),
                            w[..., o_u:]], axis=-1)


def _mixers(qkv, z, ba, qb, kvb, gate, conv_w, a_log, dt_bias, gdn_norm, nsa_q_norm, nsa_k_norm,
            cmp_pos, cmp_w1, cmp_w2):
    ya = _gdn(qkv, ba, z, conv_w, a_log, dt_bias, gdn_norm)
    qT, ks, kw, vs, vw = _nsa_prep(qb, kvb, nsa_q_norm, nsa_k_norm)
    kc, vcT = _nsa_compress(kvb, cmp_pos, cmp_w1, cmp_w2, nsa_k_norm[0])
    ybT = _nsa_attn(qT, kc, vcT, ks, vs, kw, vw, gate)
    return ya, ybT


def kernel(x, norm_mix, w_in, conv_w, a_log, dt_bias, gdn_norm, nsa_q_norm, nsa_k_norm, cmp_pos, cmp_w1, cmp_w2,
           pool_w, pool_scale, w_out, norm_ffn, w_ffn1, w_ffn2):
    B, T, D = x.shape
    w_in_pad = _pad_in_weights(w_in)
    for l in range(w_in.shape[0]):
        proj = _inproj(x.reshape(B * T, D), norm_mix[l], w_in_pad[l])
        qkv, z, ba, qb, kvb, gate, u = [p.reshape(B, T, -1) for p in proj]
        ya, ybT = _mixers(qkv, z, ba, qb, kvb, gate, conv_w[l], a_log[l], dt_bias[l], gdn_norm[l],
                          nsa_q_norm[l], nsa_k_norm[l], cmp_pos[l], cmp_w1[l], cmp_w2[l])
        x = _post(x, ya, ybT, u, pool_w[l], pool_scale[l], w_out[l], norm_ffn[l], w_ffn1[l], w_ffn2[l])
    return x
```

```python
import functools

import jax
import jax.numpy as jnp
from jax import lax
from jax.experimental import pallas as pl
from jax.experimental.pallas import tpu as pltpu

F32 = jnp.float32
BF16 = jnp.bfloat16

D_MODEL = 1024
HEAD_DIM = 64
GDN_HEADS = 6
GDN_CHUNK = 64
GDN_WIDTH = GDN_HEADS * HEAD_DIM
NSA_HEADS = 6
NSA_KV_HEADS = 2
NSA_GROUP = NSA_HEADS // NSA_KV_HEADS
NSA_WIDTH = NSA_HEADS * HEAD_DIM
CMP_LEN = 32
CMP_STRIDE = 16
CMP_HIDDEN = 256
SEL_BLOCK = 64
SEL_TOPK = 16
WINDOW = 512
Q_BLOCK = 256
FORCED_SCORE = 1e4
POOL_WINDOWS = (2, 4, 8, 16)
POOL_GROUP_DIM = 64
POOL_WIDTH = 4 * POOL_GROUP_DIM
D_FF = 4 * D_MODEL
EPS = 1e-6

LANES = 128
SUBLANES = 8
VMEM_LIMIT = 56 * 1024 * 1024
NEG = -1e30
LOG2E = 1.4426950408889634
KEY_TILE = 1024
WIN_KEYS = WINDOW + Q_BLOCK
CMP_CHUNK = 128
V_ROWS = 80

IN_GROUPS = (3 * GDN_WIDTH, GDN_WIDTH, LANES, NSA_WIDTH, 6 * LANES, LANES, POOL_WIDTH)


def _sigmoid(x):
    return 1.0 / (1.0 + jnp.exp(-x))


def _silu(x):
    return x * _sigmoid(x)


def _softplus(x):
    return jnp.maximum(x, 0.0) + jnp.log(1.0 + jnp.exp(-jnp.abs(x)))


def _split_bf16(x):
    hi = x.astype(BF16)
    lo = (x - hi.astype(F32)).astype(BF16)
    return hi, lo


def _dot_x_exact(x, m):
    hi, lo = _split_bf16(x)
    return jnp.dot(hi, m, preferred_element_type=F32) + jnp.dot(lo, m, preferred_element_type=F32)


def _dot_exact_x(m, x):
    hi, lo = _split_bf16(x)
    return jnp.dot(m, hi, preferred_element_type=F32) + jnp.dot(m, lo, preferred_element_type=F32)


def _mm(a, b):
    return jnp.dot(a.astype(BF16), b.astype(BF16), preferred_element_type=F32)


def _mm_nt(a, b):
    return lax.dot_general(a.astype(BF16), b.astype(BF16), (((1,), (1,)), ((), ())),
                           preferred_element_type=F32)


def _mm_tn(a, b):
    return lax.dot_general(a.astype(BF16), b.astype(BF16), (((0,), (0,)), ((), ())),
                           preferred_element_type=F32)


def _rms(x, g):
    return x * lax.rsqrt(jnp.mean(x * x, axis=-1, keepdims=True) + EPS) * g


def _const_spec(shape):
    nd = len(shape)
    return pl.BlockSpec(shape, lambda *_: (0,) * nd)


def _resident_spec(shape):
    nd = len(shape)
    return pl.BlockSpec(shape, lambda *_: (0,) * nd, pipeline_mode=pl.Buffered(1))


def _params(sem):
    return pltpu.CompilerParams(dimension_semantics=sem, vmem_limit_bytes=VMEM_LIMIT)


def _inproj_body(x_ref, g_ref, w_ref, segq_ref, segk_ref, qn_ref, kn_ref,
                 qkv_ref, z_ref, ba_ref, kvc_ref, gate_ref, u_ref, qT_ref, ks_ref, kw_ref, vs_ref, vw_ref):
    tm = x_ref.shape[1]
    h = _rms(x_ref[0], g_ref[...]).astype(BF16)
    offs = [0]
    for width in IN_GROUPS:
        offs.append(offs[-1] + width)
    proj = lambda gi: jnp.dot(h, w_ref[:, offs[gi]:offs[gi + 1]], preferred_element_type=F32)
    qkv_ref[0] = proj(0)
    z_ref[0] = proj(1)
    ba_ref[0] = proj(2)
    gate_ref[0] = proj(5)
    u_ref[0] = proj(6)
    kv = proj(4)
    kvc_ref[0] = kv[:, :2 * LANES]
    _nsa_layouts(proj(3), kv, pl.program_id(1) * tm, segq_ref[...], segk_ref[...], qn_ref[...], kn_ref[...],
                 qT_ref, ks_ref, kw_ref, vs_ref, vw_ref)


def _inproj(x, g, w_pad, q_norm, k_norm, tm=1024):
    B, T, _ = x.shape
    chq = jnp.arange(NSA_WIDTH) // HEAD_DIM
    segq = (chq[:, None] == chq[None, :]).astype(BF16)
    segk = segq[:LANES, :LANES]
    qn = jnp.tile(q_norm, NSA_HEADS).reshape(1, NSA_WIDTH)
    kn = jnp.tile(k_norm[1:3], (1, NSA_KV_HEADS))
    tok = lambda c: pl.BlockSpec((1, tm, c), lambda b, i: (b, i, 0))
    f32 = lambda c: jax.ShapeDtypeStruct((B, T, c), F32)
    vspec = pl.BlockSpec((1, NSA_KV_HEADS, V_ROWS, tm), lambda b, i: (b, 0, 0, i))
    vshape = jax.ShapeDtypeStruct((B, NSA_KV_HEADS, V_ROWS, T), BF16)
    return pl.pallas_call(
        _inproj_body,
        grid=(B, T // tm),
        in_specs=[tok(D_MODEL), _const_spec((1, D_MODEL)), _resident_spec(w_pad.shape),
                  _const_spec(segq.shape), _const_spec(segk.shape), _const_spec(qn.shape), _const_spec(kn.shape)],
        out_specs=[tok(3 * GDN_WIDTH), tok(GDN_WIDTH), tok(LANES), tok(2 * LANES), tok(LANES), tok(POOL_WIDTH),
                   pl.BlockSpec((1, NSA_WIDTH, tm), lambda b, i: (b, 0, i)),
                   pl.BlockSpec((1, NSA_KV_HEADS, tm, LANES), lambda b, i: (b, 0, i, 0)),
                   tok(LANES), vspec, vspec],
        out_shape=[f32(3 * GDN_WIDTH), f32(GDN_WIDTH), f32(LANES), f32(2 * LANES), f32(LANES), f32(POOL_WIDTH),
                   jax.ShapeDtypeStruct((B, NSA_WIDTH, T), BF16),
                   jax.ShapeDtypeStruct((B, NSA_KV_HEADS, T, LANES), BF16),
                   jax.ShapeDtypeStruct((B, T, LANES), BF16),
                   vshape, vshape],
        compiler_params=_params(("parallel", "parallel")),
        name="inproj",
    )(x, g.reshape(1, D_MODEL), w_pad, segq, segk, qn, kn)


def _gdn_inputs(x, tail, ba, cw, alog, dtb, seg, eb, eg):
    xe = jnp.concatenate([tail, x], axis=0)
    y = x * cw[3:4, :]
    for s in (1, 2, 3):
        y = y + pltpu.roll(xe, s, 0)[SUBLANES:] * cw[3 - s:4 - s, :]
    a = _silu(y)
    q = a[:, :GDN_WIDTH]
    k = a[:, GDN_WIDTH:2 * GDN_WIDTH]
    qn = q * lax.rsqrt(_dot_x_exact(q * q, seg) + EPS) * (HEAD_DIM ** -0.5)
    kn = k * lax.rsqrt(_dot_x_exact(k * k, seg) + EPS)
    gval = -jnp.exp(alog) * _softplus(ba + dtb)
    return qn, kn, a[:, 2 * GDN_WIDTH:], _dot_x_exact(_sigmoid(ba), eb), _dot_x_exact(gval, eg)


def _gdn_body(qkv_ref, ba_ref, z_ref, cw_ref, alog_ref, dtb_ref, seg_ref, eb_ref, eg_ref, tri_ref, gn_ref,
              o_ref, s_ref, tail_ref, *, cps):
    C = GDN_CHUNK
    P = 2 * C
    B = qkv_ref.shape[0]
    NP = GDN_WIDTH // LANES
    tm = qkv_ref.shape[1]

    @pl.when(pl.program_id(0) == 0)
    def _():
        s_ref[...] = jnp.zeros_like(s_ref)
        tail_ref[...] = jnp.zeros_like(tail_ref)

    feats = []
    for b in range(B):
        x = qkv_ref[b]
        feats.append(_gdn_inputs(x, tail_ref[b], ba_ref[b], cw_ref[...], alog_ref[...], dtb_ref[...],
                                 seg_ref[...], eb_ref[...], eg_ref[...]))
        tail_ref[b] = x[tm - SUBLANES:, :]

    lo_lane = lax.broadcasted_iota(jnp.int32, (C, LANES), 1) < HEAD_DIM
    r = lax.broadcasted_iota(jnp.int32, (P, P), 0)
    c = lax.broadcasted_iota(jnp.int32, (P, P), 1)
    same = (r < C) == (c < C)
    causal = same & (r >= c)
    strict = same & (r > c)
    eye = (r == c).astype(F32)
    tri = tri_ref[...]
    gn = gn_ref[...]

    def blk(full, j, p):
        x = full[C * j:C * (j + 1), LANES * p:LANES * (p + 1)]
        return jnp.concatenate([jnp.where(lo_lane, x, 0.0), jnp.where(lo_lane, 0.0, x)], axis=0)

    chains = [(b, p) for b in range(B) for p in range(NP)]
    st = {}
    _gdn_local_stages([(b, p, j) for j in range(cps) for (b, p) in chains], st, feats, blk, tri, causal, strict,
                      same, eye, r, c)

    S = {ch: s_ref[ch[0], ch[1]] for ch in chains}
    for j in range(cps):
        for ch in chains:
            b, p = ch
            d = st[(b, p, j)]
            ws = _mm(jnp.concatenate([d["w"], d["qd"]], axis=0), S[ch])
            vnew = d["u"] - ws[:P]
            ov = _mm(jnp.concatenate([d["qk"], d["kdT"]], axis=0), vnew)
            o2 = ws[P:] + ov[:P]
            S[ch] = S[ch] * d["eg"] + ov[P:]
            ms = jnp.sum(o2 * o2, axis=-1, keepdims=True) * (1.0 / HEAD_DIM)
            on = o2 * lax.rsqrt(ms + EPS) * gn
            z = z_ref[b, C * j:C * (j + 1), LANES * p:LANES * (p + 1)]
            o_ref[b, C * j:C * (j + 1), LANES * p:LANES * (p + 1)] = (on[:C] + on[C:]) * _silu(z)
    for ch in chains:
        s_ref[ch[0], ch[1]] = S[ch]


def _gdn_local_stages(units, st, feats, blk, tri, causal, strict, same, eye, r, c):
    C = GDN_CHUNK
    P = 2 * C
    for u in units:
        b, p, j = u
        q2, k2, v2, b2, g2 = [blk(f, j, p) for f in feats[b]]
        g_hi, g_lo = _split_bf16(g2)
        gcc = jnp.dot(tri, jnp.concatenate([g_hi, g_lo], axis=1), preferred_element_type=F32)
        gc = gcc[:, :LANES] + gcc[:, LANES:]
        decay = jnp.exp(jnp.where(causal, gc - gc.T, NEG))
        kb = k2 * b2
        gram = _mm_nt(jnp.concatenate([kb, q2], axis=0), k2)
        egc = jnp.exp(gc)
        glast = jnp.concatenate([jnp.broadcast_to(gc[C - 1:C, :], (C, LANES)),
                                 jnp.broadcast_to(gc[P - 1:P, :], (C, LANES))], axis=0)
        st[u] = dict(L=jnp.where(strict, gram[:P] * decay, 0.0), qk=gram[P:] * decay,
                     rhs=jnp.concatenate([v2 * b2, kb * egc], axis=1), qd=q2 * egc,
                     kdT=(k2 * jnp.exp(glast - gc)).T, eg=jnp.exp(glast))

    for u in units:
        st[u]["X"] = eye
    s = 1
    while s < C:
        couple = same & ((r // (2 * s)) == (c // (2 * s))) & (((r // s) % 2) == 1) & (((c // s) % 2) == 0)
        for u in units:
            X = st[u]["X"]
            st[u]["X"] = X - _mm(X, _mm(jnp.where(couple, st[u]["L"], 0.0), X))
        s *= 2
    for u in units:
        st[u].pop("L")
        uw = _mm(st[u].pop("X"), st[u].pop("rhs"))
        st[u]["u"], st[u]["w"] = uw[:, :LANES], uw[:, LANES:]


def _gdn(qkv, ba, z, conv_w, a_log, dt_bias, gdn_norm, cps=4):
    B, T, width = qkv.shape
    C = GDN_CHUNK
    H = GDN_HEADS
    tm = C * cps
    alog_p = jnp.zeros((1, LANES), F32).at[0, H:2 * H].set(a_log)
    dtb_p = jnp.zeros((1, LANES), F32).at[0, H:2 * H].set(dt_bias)
    ch = jnp.arange(GDN_WIDTH) // HEAD_DIM
    seg = (ch[:, None] == ch[None, :]).astype(BF16)
    row = jnp.arange(LANES)
    eb = (row[:, None] == ch[None, :]).astype(BF16)
    eg = (row[:, None] - H == ch[None, :]).astype(BF16)
    i2 = jnp.arange(2 * C)
    tri = (((i2[:, None] // C) == (i2[None, :] // C)) & (i2[:, None] >= i2[None, :])).astype(BF16)
    gn2 = jnp.tile(gdn_norm, 2).reshape(1, LANES)
    tok = lambda c: pl.BlockSpec((B, tm, c), lambda i: (0, i, 0))
    consts = (conv_w, alog_p, dtb_p, seg, eb, eg, tri, gn2)
    return pl.pallas_call(
        functools.partial(_gdn_body, cps=cps),
        grid=(T // tm,),
        in_specs=[tok(width), tok(LANES), tok(GDN_WIDTH)] + [_const_spec(c.shape) for c in consts],
        out_specs=tok(GDN_WIDTH),
        out_shape=jax.ShapeDtypeStruct((B, T, GDN_WIDTH), F32),
        scratch_shapes=[pltpu.VMEM((B, GDN_WIDTH // LANES, 2 * C, LANES), F32),
                        pltpu.VMEM((B, SUBLANES, width), F32)],
        compiler_params=_params(("arbitrary",)),
        name="gdn",
    )(qkv, ba, z, *consts)


def _nsa_layouts(q, kv, first_pos, segq, segk, qn, kn, qT_ref, ks_ref, kw_ref, vs_ref, vw_ref):
    tm = q.shape[0]
    ms = _dot_x_exact(q * q, segq) * (1.0 / HEAD_DIM)
    qT_ref[0] = (q * lax.rsqrt(ms + EPS) * qn * (HEAD_DIM ** -0.5 * LOG2E)).T.astype(BF16)

    def knorm(k, g):
        msk = _dot_x_exact(k * k, segk) * (1.0 / HEAD_DIM)
        return k * lax.rsqrt(msk + EPS) * g

    ks = knorm(kv[:, 2 * LANES:3 * LANES], kn[0:1, :])
    lane = lax.broadcasted_iota(jnp.int32, (tm, LANES), 1)
    tok = first_pos + lax.broadcasted_iota(jnp.int32, (tm, LANES), 0)
    onehot = (lane - HEAD_DIM == (tok // SEL_BLOCK) % (KEY_TILE // SEL_BLOCK)).astype(F32)
    ks_ref[0, 0] = jnp.where(lane < HEAD_DIM, ks, onehot).astype(BF16)
    ks_ref[0, 1] = jnp.where(lane < HEAD_DIM, pltpu.roll(ks, HEAD_DIM, 1), onehot).astype(BF16)
    kw_ref[0] = knorm(kv[:, 4 * LANES:5 * LANES], kn[1:2, :]).astype(BF16)
    ones_rows = (lax.broadcasted_iota(jnp.int32, (V_ROWS - HEAD_DIM, tm), 0) == 0).astype(BF16)
    for v_ref, cols in ((vs_ref, 3), (vw_ref, 5)):
        vT = kv[:, cols * LANES:(cols + 1) * LANES].T.astype(BF16)
        for h in range(NSA_KV_HEADS):
            v_ref[0, h, :HEAD_DIM, :] = vT[h * HEAD_DIM:(h + 1) * HEAD_DIM, :]
            v_ref[0, h, HEAD_DIM:, :] = ones_rows


def _nsa_compress_body(x_ref, wd_ref, pos_ref, w2k_ref, w2vT_ref, kn_ref, kc_ref, vcT_ref):
    branch = pl.program_id(1)
    nr = x_ref.shape[1] // CMP_STRIDE
    hid2 = NSA_KV_HEADS * CMP_HIDDEN
    ya = jnp.zeros((nr, hid2), F32)
    yb = jnp.zeros((nr, hid2), F32)
    for l in range(CMP_STRIDE):
        xl = x_ref[0, pl.ds(l, nr, stride=CMP_STRIDE), :].astype(BF16)
        ya = ya + jnp.dot(xl, wd_ref[0, l * LANES:(l + 1) * LANES, :], preferred_element_type=F32)
        yb = yb + jnp.dot(xl, wd_ref[0, (CMP_STRIDE + l) * LANES:(CMP_STRIDE + l + 1) * LANES, :],
                          preferred_element_type=F32)
    bias = jnp.dot(pos_ref[0], wd_ref[0], preferred_element_type=F32)[0:1, :]
    hidden = _silu(ya + pltpu.roll(yb, nr - 1, 0) + bias).astype(BF16)

    @pl.when(branch == 0)
    def _():
        for h in range(NSA_KV_HEADS):
            kc = jnp.dot(hidden[:, h * CMP_HIDDEN:(h + 1) * CMP_HIDDEN], w2k_ref[...], preferred_element_type=F32)
            kc_ref[0, h] = _rms(kc, kn_ref[...]).astype(BF16)

    @pl.when(branch == 1)
    def _():
        for h in range(NSA_KV_HEADS):
            vcT_ref[0, h] = lax.dot_general(w2vT_ref[...], hidden[:, h * CMP_HIDDEN:(h + 1) * CMP_HIDDEN],
                                            (((1,), (1,)), ((), ())), preferred_element_type=F32).astype(BF16)


def _nsa_compress(kvb, cmp_pos, cmp_w1, cmp_w2, k_norm0):
    B, T, _ = kvb.shape
    nr = T // CMP_STRIDE
    w1 = cmp_w1.reshape(2, CMP_LEN, HEAD_DIM, CMP_HIDDEN)
    z = jnp.zeros_like(w1)
    wd = jnp.concatenate([jnp.concatenate([w1, z], axis=3), jnp.concatenate([z, w1], axis=3)], axis=2)
    wd = wd.reshape(2, CMP_LEN * LANES, NSA_KV_HEADS * CMP_HIDDEN).astype(BF16)
    pos = jnp.tile(cmp_pos, (1, 1, NSA_KV_HEADS)).reshape(2, 1, CMP_LEN * LANES)
    pos = jnp.broadcast_to(pos, (2, SUBLANES, CMP_LEN * LANES)).astype(BF16)
    return pl.pallas_call(
        _nsa_compress_body,
        grid=(B, 2),
        in_specs=[pl.BlockSpec((1, T, LANES), lambda b, r: (b, 0, r)),
                  pl.BlockSpec((1,) + wd.shape[1:], lambda b, r: (r, 0, 0)),
                  pl.BlockSpec((1,) + pos.shape[1:], lambda b, r: (r, 0, 0)),
                  _const_spec((CMP_HIDDEN, HEAD_DIM)), _const_spec((HEAD_DIM, CMP_HIDDEN)),
                  _const_spec((1, HEAD_DIM))],
        out_specs=[pl.BlockSpec((1, NSA_KV_HEADS, nr, HEAD_DIM), lambda b, r: (b, 0, 0, 0)),
                   pl.BlockSpec((1, NSA_KV_HEADS, HEAD_DIM, nr), lambda b, r: (b, 0, 0, 0))],
        out_shape=[jax.ShapeDtypeStruct((B, NSA_KV_HEADS, nr, HEAD_DIM), BF16),
                   jax.ShapeDtypeStruct((B, NSA_KV_HEADS, HEAD_DIM, nr), BF16)],
        compiler_params=_params(("parallel", "arbitrary")),
        name="nsa_compress",
    )(kvb, wd, pos, cmp_w2[0].astype(BF16), cmp_w2[1].T.astype(BF16), k_norm0.reshape(1, HEAD_DIM))


def _nsa_attn_body(qT_ref, kc_ref, vcT_ref, ks_ref, vs_ref, kw_ref, vw_ref, gate_ref, selm_ref,
                   o_ref, selb_ref, gT_ref, qaug_ref, m_ref, acc_ref, sa_ref, sb_ref, oc_ref, ow_ref, *, seq):
    G, Dh, Qb = NSA_GROUP, HEAD_DIM, Q_BLOCK
    h = pl.program_id(1)
    t0 = pl.program_id(2) * Qb
    nr = seq // CMP_STRIDE
    ns = seq // SEL_BLOCK
    pos = t0 + lax.broadcasted_iota(jnp.int32, (1, Qb), 1)

    qT = qT_ref[0]
    q3 = jnp.concatenate([qT[g * Dh:(g + 1) * Dh, :] for g in range(G)], axis=1)
    qaug_ref[...] = jnp.concatenate([q3, jnp.zeros_like(q3)], axis=0)
    row = lax.broadcasted_iota(jnp.int32, (2 * Dh, G * Qb), 0)
    qpad = jnp.where((row < Dh) == (h == 0), jnp.concatenate([q3, q3], axis=0), 0).astype(BF16)

    def window_branch():
        ws = pl.multiple_of(jnp.maximum(t0 - WINDOW, 0), Qb)
        sw = jnp.dot(kw_ref[0, pl.ds(ws, WIN_KEYS), :], qpad, preferred_element_type=F32)
        vwx = vw_ref[0, 0, :, pl.ds(ws, WIN_KEYS)]
        kpos = ws + lax.broadcasted_iota(jnp.int32, (WIN_KEYS, 1), 0)
        bias_w = jnp.where((kpos <= pos) & (kpos > pos - WINDOW), 0.0, NEG)
        for g in range(G):
            s = sw[:, g * Qb:(g + 1) * Qb] + bias_w
            e = jnp.exp2(s - jnp.max(s, axis=0, keepdims=True)).astype(BF16)
            ow_ref[g] = jnp.dot(vwx, e, preferred_element_type=F32)

    def compressed_and_select(n_c):
        window_branch()
        n_s = n_c * CMP_STRIDE // SEL_BLOCK
        sc = jnp.dot(kc_ref[0, 0, :n_c, :], q3, preferred_element_type=F32)
        cmp_end = lax.broadcasted_iota(jnp.int32, (n_c, 1), 0) * CMP_STRIDE + (CMP_LEN - 1)
        bias_c = jnp.where(cmp_end <= pos, 0.0, NEG)
        has_c = pos >= CMP_LEN - 1
        lhs = jnp.concatenate([vcT_ref[0, 0, :, :n_c], selm_ref[:n_s, :n_c]], axis=0)
        imp = jnp.zeros((n_s, Qb), F32)
        for g in range(G):
            s = sc[:, g * Qb:(g + 1) * Qb] + bias_c
            e = jnp.exp2(s - jnp.max(s, axis=0, keepdims=True))
            p = (e * jnp.where(has_c, 1.0 / jnp.sum(e, axis=0, keepdims=True), 0.0)).astype(BF16)
            both = jnp.dot(lhs, p, preferred_element_type=F32)
            oc_ref[g] = both[:Dh]
            imp = imp + both[Dh:]

        blk = lax.broadcasted_iota(jnp.int32, (n_s, Qb), 0)
        cur = pos // SEL_BLOCK
        forced = (blk == 0) | (blk == cur) | (blk == cur - 1)
        score = jnp.where(forced | (blk > cur), -jnp.inf, imp)
        for _ in range(SEL_TOPK - 3):
            best = jnp.max(score, axis=0, keepdims=True)
            first = jnp.min(jnp.where(score == best, blk, n_s), axis=0, keepdims=True)
            score = jnp.where(blk == first, -jnp.inf, score)
        selb_ref[:n_s, :] = jnp.where((score == -jnp.inf) & (blk <= cur), 0.0, NEG)

    chunk = min(CMP_CHUNK, nr)
    need = (t0 + Qb) // CMP_STRIDE
    for k in range(1, nr // chunk + 1):
        pl.when((need + chunk - 1) // chunk == k)(functools.partial(compressed_and_select, k * chunk))

    bpt = KEY_TILE // SEL_BLOCK
    krow = lax.broadcasted_iota(jnp.int32, (KEY_TILE, 1), 0)
    m_ref[...] = jnp.full(m_ref.shape, NEG, F32)
    acc_ref[...] = jnp.zeros(acc_ref.shape, F32)

    def scores(kt, slot_ref, nk=KEY_TILE):
        base = pl.multiple_of(kt * KEY_TILE, KEY_TILE)
        sb = selb_ref[pl.ds(pl.multiple_of(kt * bpt, bpt), bpt), :]
        qaug_ref[Dh:Dh + bpt, :] = jnp.concatenate([sb] * G, axis=1).astype(BF16)
        slot_ref[:nk, :] = jnp.dot(ks_ref[0, 0, pl.ds(base, nk), :], qaug_ref[...],
                                   preferred_element_type=F32)

    def consume(kt, slot_ref, diagonal, nk=KEY_TILE):
        base = pl.multiple_of(kt * KEY_TILE, KEY_TILE)
        vx = vs_ref[0, 0, :, pl.ds(base, nk)]
        for g in range(G):
            s = slot_ref[:nk, g * Qb:(g + 1) * Qb]
            if diagonal:
                s = jnp.where(base + krow[:nk] <= pos, s, NEG)
            m_old = m_ref[g:g + 1, :]
            m_new = jnp.maximum(m_old, jnp.max(s, axis=0, keepdims=True))
            e = jnp.exp2(s - m_new).astype(BF16)
            acc_ref[g] = jnp.exp2(m_old - m_new) * acc_ref[g] + jnp.dot(vx, e, preferred_element_type=F32)
            m_ref[g:g + 1, :] = m_new

    n_off = (t0 + Qb + KEY_TILE - 1) // KEY_TILE - 1
    scores(0, sa_ref)

    def pair(p, carry):
        scores(2 * p + 1, sb_ref)
        consume(2 * p, sa_ref, False)
        scores(2 * p + 2, sa_ref)
        consume(2 * p + 1, sb_ref, False)
        return carry

    lax.fori_loop(0, n_off // 2, pair, 0)

    def last_odd(nk):
        scores(n_off, sb_ref, nk)
        consume(n_off - 1, sa_ref, False)
        consume(n_off, sb_ref, True, nk)

    def last_even(nk):
        consume(n_off, sa_ref, True, nk)

    q_in_tile = (t0 % KEY_TILE) // Qb
    for r in range(KEY_TILE // Qb):
        pl.when((n_off % 2 == 1) & (q_in_tile == r))(functools.partial(last_odd, (r + 1) * Qb))
        pl.when((n_off % 2 == 0) & (q_in_tile == r))(functools.partial(last_even, (r + 1) * Qb))

    gT_ref[...] = _sigmoid(gate_ref[0]).T
    gT = gT_ref[0:3 * SUBLANES, :]
    for g in range(G):
        r0 = g * 3
        gates = jnp.where(h == 0, gT[r0:r0 + 3], gT[G * 3 + r0:G * 3 + r0 + 3])
        a_s, a_w = acc_ref[g], ow_ref[g]
        o_s = a_s[:Dh] * (1.0 / a_s[Dh:Dh + 1])
        o_w = a_w[:Dh] * (1.0 / a_w[Dh:Dh + 1])
        o_ref[0, g * Dh:(g + 1) * Dh, :] = gates[0:1] * oc_ref[g] + gates[1:2] * o_s + gates[2:3] * o_w


def _nsa_attn(qT, kc, vcT, ks, vs, kw, vw, gate):
    B, _, T = qT.shape
    assert T % KEY_TILE == 0 and T >= WIN_KEYS
    nr, ns = T // CMP_STRIDE, T // SEL_BLOCK
    G, Dh, Qb = NSA_GROUP, HEAD_DIM, Q_BLOCK
    j = jnp.arange(ns)[:, None]
    c = jnp.arange(nr)[None, :]
    ratio = SEL_BLOCK // CMP_STRIDE
    selm = ((c >= ratio * j - 1) & (c <= ratio * j + ratio - 1) & (c < nr - 1)).astype(BF16)
    vfull = pl.BlockSpec((1, 1, V_ROWS, T), lambda b, h, i: (b, h, 0, 0))
    return pl.pallas_call(
        functools.partial(_nsa_attn_body, seq=T),
        grid=(B, NSA_KV_HEADS, T // Qb),
        in_specs=[pl.BlockSpec((1, G * Dh, Qb), lambda b, h, i: (b, h, i)),
                  pl.BlockSpec((1, 1, nr, Dh), lambda b, h, i: (b, h, 0, 0)),
                  pl.BlockSpec((1, 1, Dh, nr), lambda b, h, i: (b, h, 0, 0)),
                  pl.BlockSpec((1, 1, T, LANES), lambda b, h, i: (b, h, 0, 0)), vfull,
                  pl.BlockSpec((1, T, LANES), lambda b, h, i: (b, 0, 0)), vfull,
                  pl.BlockSpec((1, Qb, LANES), lambda b, h, i: (b, i, 0)),
                  _const_spec(selm.shape)],
        out_specs=pl.BlockSpec((1, G * Dh, Qb), lambda b, h, i: (b, h, i)),
        out_shape=jax.ShapeDtypeStruct((B, NSA_WIDTH, T), F32),
        scratch_shapes=[pltpu.VMEM((ns, Qb), F32), pltpu.VMEM((LANES, Qb), F32),
                        pltpu.VMEM((2 * Dh, G * Qb), BF16), pltpu.VMEM((SUBLANES, Qb), F32),
                        pltpu.VMEM((G, V_ROWS, Qb), F32),
                        pltpu.VMEM((KEY_TILE, G * Qb), F32), pltpu.VMEM((KEY_TILE, G * Qb), F32),
                        pltpu.VMEM((G, Dh, Qb), F32), pltpu.VMEM((G, V_ROWS, Qb), F32)],
        compiler_params=_params(("parallel", "parallel", "arbitrary")),
        name="nsa_attn",
    )(qT, kc, vcT, ks, vs, kw, vw, gate, selm)


POOL_HALO = 16
FF_TILE = 1024


def _pool_tile(u, halo, first_pos, w, scale):
    tm = u.shape[0]
    ue = jnp.concatenate([halo, u], axis=0)
    grp = lax.broadcasted_iota(jnp.int32, (1, POOL_WIDTH), 1) // POOL_GROUP_DIM
    acc = ue
    wsum = jnp.zeros_like(ue)
    width = jnp.zeros((1, POOL_WIDTH), F32)
    for gi, win in enumerate(POOL_WINDOWS):
        acc = acc + pltpu.roll(acc, win // 2, 0)
        wsum = jnp.where(grp == gi, acc, wsum)
        width = jnp.where(grp == gi, float(win), width)
    t1 = (first_pos + 1 + lax.broadcasted_iota(jnp.int32, (tm, 1), 0)).astype(F32)
    mean = wsum[POOL_HALO:] / jnp.minimum(t1, width)
    return jnp.dot((mean - u).astype(BF16), w, preferred_element_type=F32) * scale


def _post_body(x_ref, ya_ref, ybT_ref, u_ref, halo_ref, wa_ref, wb_ref, wc_ref, pw_ref, psc_ref, g_ref,
               w1_ref, w2_ref, o_ref):
    i = pl.program_id(1)
    tm = x_ref.shape[1]
    halo = jnp.where(i > 0, halo_ref[0], 0.0)
    yc = _pool_tile(u_ref[0], halo, i * tm, pw_ref[...], psc_ref[...])
    x = x_ref[0]
    x = x + jnp.dot(ya_ref[0].astype(BF16), wa_ref[...], preferred_element_type=F32)
    x = x + jnp.dot(ybT_ref[0].T.astype(BF16), wb_ref[...], preferred_element_type=F32)
    x = x + jnp.dot(yc.astype(BF16), wc_ref[...], preferred_element_type=F32)
    o_ref[0] = x
    x = o_ref[0]
    h = _rms(x, g_ref[...]).astype(BF16)
    acc = x
    for j in range(D_FF // FF_TILE):
        cols = slice(j * FF_TILE, (j + 1) * FF_TILE)
        a = jnp.maximum(jnp.dot(h, w1_ref[:, cols], preferred_element_type=F32), 0.0)
        acc = acc + jnp.dot((a * a).astype(BF16), w2_ref[cols, :], preferred_element_type=F32)
    o_ref[0] = acc


def _post(x, ya, ybT, u, pool_w, pool_scale, w_out, g, w1, w2, tm=1024):
    B, T, _ = x.shape
    w = w_out.astype(BF16)
    wa, wb, wc = w[:GDN_WIDTH], w[GDN_WIDTH:GDN_WIDTH + NSA_WIDTH], w[GDN_WIDTH + NSA_WIDTH:]
    wblk = jax.scipy.linalg.block_diag(*[pool_w[gi] for gi in range(len(POOL_WINDOWS))]).astype(BF16)
    tok = lambda c: pl.BlockSpec((1, tm, c), lambda b, i: (b, i, 0))
    hb = tm // POOL_HALO
    return pl.pallas_call(
        _post_body,
        grid=(B, T // tm),
        in_specs=[tok(D_MODEL), tok(GDN_WIDTH), pl.BlockSpec((1, NSA_WIDTH, tm), lambda b, i: (b, 0, i)),
                  tok(POOL_WIDTH),
                  pl.BlockSpec((1, POOL_HALO, POOL_WIDTH), lambda b, i: (b, jnp.maximum(i * hb - 1, 0), 0)),
                  _const_spec(wa.shape), _const_spec(wb.shape), _const_spec(wc.shape),
                  _const_spec(wblk.shape), _const_spec((1, POOL_WIDTH)), _const_spec((1, D_MODEL)),
                  _resident_spec(w1.shape), _resident_spec(w2.shape)],
        out_specs=tok(D_MODEL),
        out_shape=jax.ShapeDtypeStruct(x.shape, F32),
        compiler_params=_params(("parallel", "parallel")),
        name="post",
    )(x, ya, ybT, u, u, wa, wb, wc, wblk, pool_scale.reshape(1, POOL_WIDTH), g.reshape(1, D_MODEL),
      w1.astype(BF16), w2.astype(BF16))


def _pad_in_weights(w):
    H = GDN_HEADS
    o_ba = 4 * GDN_WIDTH
    o_q = o_ba + 2 * H
    o_gate = o_q + NSA_WIDTH + 6 * LANES
    o_u = o_gate + 3 * NSA_HEADS
    w = w.astype(BF16)
    pad = lambda a: jnp.pad(a, [(0, 0)] * (a.ndim - 1) + [(0, LANES - a.shape[-1])])
    return jnp.concatenate([w[..., :o_ba], pad(w[..., o_ba:o_q]), w[..., o_q:o_gate], pad(w[..., o_gate:o_u]),
                            w[..., o_u:]], axis=-1)


def kernel(x, norm_mix, w_in, conv_w, a_log, dt_bias, gdn_norm, nsa_q_norm, nsa_k_norm, cmp_pos, cmp_w1, cmp_w2,
           pool_w, pool_scale, w_out, norm_ffn, w_ffn1, w_ffn2):
    w_in_pad = _pad_in_weights(w_in)
    for l in range(w_in.shape[0]):
        qkv, z, ba, kvc, gate, u, qT, ks, kw, vs, vw = _inproj(x, norm_mix[l], w_in_pad[l], nsa_q_norm[l],
                                                               nsa_k_norm[l])
        ya = _gdn(qkv, ba, z, conv_w[l], a_log[l], dt_bias[l], gdn_norm[l])
        kc, vcT = _nsa_compress(kvc, cmp_pos[l], cmp_w1[l], cmp_w2[l], nsa_k_norm[l][0])
        ybT = _nsa_attn(qT, kc, vcT, ks, vs, kw, vw, gate)
        x = _post(x, ya, ybT, u, pool_w[l], pool_scale[l], w_out[l], norm_ffn[l], w_ffn1[l], w_ffn2[l])
    return x
```

```python
import functools

import jax
import jax.numpy as jnp
from jax import lax
from jax.experimental import pallas as pl
from jax.experimental.pallas import tpu as pltpu

F32 = jnp.float32
BF16 = jnp.bfloat16

D_MODEL = 1024
HEAD_DIM = 64
GDN_HEADS = 6
GDN_CHUNK = 64
GDN_WIDTH = GDN_HEADS * HEAD_DIM
NSA_HEADS = 6
NSA_KV_HEADS = 2
NSA_GROUP = NSA_HEADS // NSA_KV_HEADS
NSA_WIDTH = NSA_HEADS * HEAD_DIM
CMP_LEN = 32
CMP_STRIDE = 16
CMP_HIDDEN = 256
SEL_BLOCK = 64
SEL_TOPK = 16
WINDOW = 512
Q_BLOCK = 256
POOL_WINDOWS = (2, 4, 8, 16)
POOL_GROUP_DIM = 64
POOL_WIDTH = 4 * POOL_GROUP_DIM
D_FF = 4 * D_MODEL
EPS = 1e-6

LANES = 128
SUBLANES = 8
VMEM_LIMIT = 56 * 1024 * 1024
NEG = -1e30
LOG2E = 1.4426950408889634
KEY_TILE = 1024
WIN_KEYS = WINDOW + Q_BLOCK
CMP_CHUNK = 128
V_ROWS = 80

IN_GROUPS = (3 * GDN_WIDTH, GDN_WIDTH, LANES, NSA_WIDTH, 6 * LANES, LANES, POOL_WIDTH)


def _sigmoid(x):
    return 1.0 / (1.0 + jnp.exp(-x))


def _silu(x):
    return x * _sigmoid(x)


def _softplus(x):
    return jnp.maximum(x, 0.0) + jnp.log(1.0 + jnp.exp(-jnp.abs(x)))


def _split_bf16(x):
    hi = x.astype(BF16)
    lo = (x - hi.astype(F32)).astype(BF16)
    return hi, lo


def _dot_x_exact(x, m):
    hi, lo = _split_bf16(x)
    return jnp.dot(hi, m, preferred_element_type=F32) + jnp.dot(lo, m, preferred_element_type=F32)


def _mm(a, b):
    return jnp.dot(a.astype(BF16), b.astype(BF16), preferred_element_type=F32)


def _mm_nt(a, b):
    return lax.dot_general(a.astype(BF16), b.astype(BF16), (((1,), (1,)), ((), ())),
                           preferred_element_type=F32)


def _rms(x, g):
    return x * lax.rsqrt(jnp.mean(x * x, axis=-1, keepdims=True) + EPS) * g


def _const_spec(shape):
    nd = len(shape)
    return pl.BlockSpec(shape, lambda *_: (0,) * nd)


def _resident_spec(shape):
    nd = len(shape)
    return pl.BlockSpec(shape, lambda *_: (0,) * nd, pipeline_mode=pl.Buffered(1))


def _params(sem):
    return pltpu.CompilerParams(dimension_semantics=sem, vmem_limit_bytes=VMEM_LIMIT)


def _inproj_body(x_ref, g_ref, w_ref, segq_ref, segk_ref, qn_ref, kn_ref,
                 qkv_ref, z_ref, ba_ref, kvc_ref, gate_ref, u_ref, qT_ref, ks_ref, kw_ref, vs_ref, vw_ref):
    tm = x_ref.shape[1]
    h = _rms(x_ref[0], g_ref[...]).astype(BF16)
    offs = [0]
    for width in IN_GROUPS:
        offs.append(offs[-1] + width)
    proj = lambda gi: jnp.dot(h, w_ref[:, offs[gi]:offs[gi + 1]], preferred_element_type=F32)
    qkv_ref[0] = proj(0)
    z_ref[0] = proj(1)
    ba_ref[0] = proj(2)
    gate_ref[0] = proj(5)
    u_ref[0] = proj(6)
    kv = proj(4)
    kvc_ref[0] = kv[:, :2 * LANES]
    _nsa_layouts(proj(3), kv, pl.program_id(1) * tm, segq_ref[...], segk_ref[...], qn_ref[...], kn_ref[...],
                 qT_ref, ks_ref, kw_ref, vs_ref, vw_ref)


def _inproj(x, g, w_pad, q_norm, k_norm, tm=1024):
    B, T, _ = x.shape
    chq = jnp.arange(NSA_WIDTH) // HEAD_DIM
    segq = (chq[:, None] == chq[None, :]).astype(BF16)
    segk = segq[:LANES, :LANES]
    qn = jnp.tile(q_norm, NSA_HEADS).reshape(1, NSA_WIDTH)
    kn = jnp.tile(k_norm[1:3], (1, NSA_KV_HEADS))
    tok = lambda c: pl.BlockSpec((1, tm, c), lambda b, i: (b, i, 0))
    f32 = lambda c: jax.ShapeDtypeStruct((B, T, c), F32)
    vspec = pl.BlockSpec((1, NSA_KV_HEADS, V_ROWS, tm), lambda b, i: (b, 0, 0, i))
    vshape = jax.ShapeDtypeStruct((B, NSA_KV_HEADS, V_ROWS, T), BF16)
    return pl.pallas_call(
        _inproj_body,
        grid=(B, T // tm),
        in_specs=[tok(D_MODEL), _const_spec((1, D_MODEL)), _resident_spec(w_pad.shape),
                  _const_spec(segq.shape), _const_spec(segk.shape), _const_spec(qn.shape), _const_spec(kn.shape)],
        out_specs=[tok(3 * GDN_WIDTH), tok(GDN_WIDTH), tok(LANES), tok(2 * LANES), tok(LANES), tok(POOL_WIDTH),
                   pl.BlockSpec((1, NSA_WIDTH, tm), lambda b, i: (b, 0, i)),
                   pl.BlockSpec((1, NSA_KV_HEADS, tm, LANES), lambda b, i: (b, 0, i, 0)),
                   tok(LANES), vspec, vspec],
        out_shape=[f32(3 * GDN_WIDTH), f32(GDN_WIDTH), f32(LANES), f32(2 * LANES), f32(LANES), f32(POOL_WIDTH),
                   jax.ShapeDtypeStruct((B, NSA_WIDTH, T), BF16),
                   jax.ShapeDtypeStruct((B, NSA_KV_HEADS, T, LANES), BF16),
                   jax.ShapeDtypeStruct((B, T, LANES), BF16),
                   vshape, vshape],
        compiler_params=_params(("parallel", "parallel")),
        name="inproj",
    )(x, g.reshape(1, D_MODEL), w_pad, segq, segk, qn, kn)


def _gdn_inputs(x, tail, ba, cw, alog, dtb, seg, eb, eg):
    xe = jnp.concatenate([tail, x], axis=0)
    y = x * cw[3:4, :]
    for s in (1, 2, 3):
        y = y + pltpu.roll(xe, s, 0)[SUBLANES:] * cw[3 - s:4 - s, :]
    a = _silu(y)
    q = a[:, :GDN_WIDTH]
    k = a[:, GDN_WIDTH:2 * GDN_WIDTH]
    qn = q * lax.rsqrt(_dot_x_exact(q * q, seg) + EPS) * (HEAD_DIM ** -0.5)
    kn = k * lax.rsqrt(_dot_x_exact(k * k, seg) + EPS)
    gval = -jnp.exp(alog) * _softplus(ba + dtb)
    return qn, kn, a[:, 2 * GDN_WIDTH:], _dot_x_exact(_sigmoid(ba), eb), _dot_x_exact(gval, eg)


def _gdn_body(qkv_ref, ba_ref, z_ref, cw_ref, alog_ref, dtb_ref, seg_ref, eb_ref, eg_ref, tri_ref, gn_ref,
              o_ref, s_ref, tail_ref, *, cps):
    C = GDN_CHUNK
    P = 2 * C
    B = qkv_ref.shape[0]
    NP = GDN_WIDTH // LANES
    tm = qkv_ref.shape[1]

    @pl.when(pl.program_id(0) == 0)
    def _():
        s_ref[...] = jnp.zeros_like(s_ref)
        tail_ref[...] = jnp.zeros_like(tail_ref)

    feats = []
    for b in range(B):
        x = qkv_ref[b]
        feats.append(_gdn_inputs(x, tail_ref[b], ba_ref[b], cw_ref[...], alog_ref[...], dtb_ref[...],
                                 seg_ref[...], eb_ref[...], eg_ref[...]))
        tail_ref[b] = x[tm - SUBLANES:, :]

    lo_lane = lax.broadcasted_iota(jnp.int32, (C, LANES), 1) < HEAD_DIM
    r = lax.broadcasted_iota(jnp.int32, (P, P), 0)
    c = lax.broadcasted_iota(jnp.int32, (P, P), 1)
    same = (r < C) == (c < C)
    causal = same & (r >= c)
    strict = same & (r > c)
    eye = (r == c).astype(F32)
    tri = tri_ref[...]
    gn = gn_ref[...]

    def blk(full, j, p):
        x = full[C * j:C * (j + 1), LANES * p:LANES * (p + 1)]
        return jnp.concatenate([jnp.where(lo_lane, x, 0.0), jnp.where(lo_lane, 0.0, x)], axis=0)

    chains = [(b, p) for b in range(B) for p in range(NP)]
    st = {}
    _gdn_local_stages([(b, p, j) for j in range(cps) for (b, p) in chains], st, feats, blk, tri, causal, strict,
                      same, eye, r, c)

    S = {ch: s_ref[ch[0], ch[1]] for ch in chains}
    for j in range(cps):
        for ch in chains:
            b, p = ch
            d = st[(b, p, j)]
            ws = _mm(jnp.concatenate([d["w"], d["qd"]], axis=0), S[ch])
            vnew = d["u"] - ws[:P]
            ov = _mm(jnp.concatenate([d["qk"], d["kdT"]], axis=0), vnew)
            o2 = ws[P:] + ov[:P]
            S[ch] = S[ch] * d["eg"] + ov[P:]
            ms = jnp.sum(o2 * o2, axis=-1, keepdims=True) * (1.0 / HEAD_DIM)
            on = o2 * lax.rsqrt(ms + EPS) * gn
            z = z_ref[b, C * j:C * (j + 1), LANES * p:LANES * (p + 1)]
            o_ref[b, C * j:C * (j + 1), LANES * p:LANES * (p + 1)] = (on[:C] + on[C:]) * _silu(z)
    for ch in chains:
        s_ref[ch[0], ch[1]] = S[ch]


def _gdn_local_stages(units, st, feats, blk, tri, causal, strict, same, eye, r, c):
    C = GDN_CHUNK
    P = 2 * C
    for u in units:
        b, p, j = u
        q2, k2, v2, b2, g2 = [blk(f, j, p) for f in feats[b]]
        g_hi, g_lo = _split_bf16(g2)
        gcc = jnp.dot(tri, jnp.concatenate([g_hi, g_lo], axis=1), preferred_element_type=F32)
        gc = gcc[:, :LANES] + gcc[:, LANES:]
        decay = jnp.exp(jnp.where(causal, gc - gc.T, NEG))
        kb = k2 * b2
        gram = _mm_nt(jnp.concatenate([kb, q2], axis=0), k2)
        egc = jnp.exp(gc)
        glast = jnp.concatenate([jnp.broadcast_to(gc[C - 1:C, :], (C, LANES)),
                                 jnp.broadcast_to(gc[P - 1:P, :], (C, LANES))], axis=0)
        st[u] = dict(L=jnp.where(strict, gram[:P] * decay, 0.0), qk=gram[P:] * decay,
                     rhs=jnp.concatenate([v2 * b2, kb * egc], axis=1), qd=q2 * egc,
                     kdT=(k2 * jnp.exp(glast - gc)).T, eg=jnp.exp(glast))

    for u in units:
        st[u]["X"] = eye
    s = 1
    while s < C:
        couple = same & ((r // (2 * s)) == (c // (2 * s))) & (((r // s) % 2) == 1) & (((c // s) % 2) == 0)
        for u in units:
            X = st[u]["X"]
            st[u]["X"] = X - _mm(X, _mm(jnp.where(couple, st[u]["L"], 0.0), X))
        s *= 2
    for u in units:
        st[u].pop("L")
        uw = _mm(st[u].pop("X"), st[u].pop("rhs"))
        st[u]["u"], st[u]["w"] = uw[:, :LANES], uw[:, LANES:]


def _gdn(qkv, ba, z, conv_w, a_log, dt_bias, gdn_norm, cps=4):
    B, T, width = qkv.shape
    C = GDN_CHUNK
    H = GDN_HEADS
    tm = C * cps
    alog_p = jnp.zeros((1, LANES), F32).at[0, H:2 * H].set(a_log)
    dtb_p = jnp.zeros((1, LANES), F32).at[0, H:2 * H].set(dt_bias)
    ch = jnp.arange(GDN_WIDTH) // HEAD_DIM
    seg = (ch[:, None] == ch[None, :]).astype(BF16)
    row = jnp.arange(LANES)
    eb = (row[:, None] == ch[None, :]).astype(BF16)
    eg = (row[:, None] - H == ch[None, :]).astype(BF16)
    i2 = jnp.arange(2 * C)
    tri = (((i2[:, None] // C) == (i2[None, :] // C)) & (i2[:, None] >= i2[None, :])).astype(BF16)
    gn2 = jnp.tile(gdn_norm, 2).reshape(1, LANES)
    tok = lambda c: pl.BlockSpec((B, tm, c), lambda i: (0, i, 0))
    consts = (conv_w, alog_p, dtb_p, seg, eb, eg, tri, gn2)
    return pl.pallas_call(
        functools.partial(_gdn_body, cps=cps),
        grid=(T // tm,),
        in_specs=[tok(width), tok(LANES), tok(GDN_WIDTH)] + [_const_spec(c.shape) for c in consts],
        out_specs=tok(GDN_WIDTH),
        out_shape=jax.ShapeDtypeStruct((B, T, GDN_WIDTH), F32),
        scratch_shapes=[pltpu.VMEM((B, GDN_WIDTH // LANES, 2 * C, LANES), F32),
                        pltpu.VMEM((B, SUBLANES, width), F32)],
        compiler_params=_params(("arbitrary",)),
        name="gdn",
    )(qkv, ba, z, *consts)


def _nsa_layouts(q, kv, first_pos, segq, segk, qn, kn, qT_ref, ks_ref, kw_ref, vs_ref, vw_ref):
    tm = q.shape[0]
    ms = _dot_x_exact(q * q, segq) * (1.0 / HEAD_DIM)
    qT_ref[0] = (q * lax.rsqrt(ms + EPS) * qn * (HEAD_DIM ** -0.5 * LOG2E)).T.astype(BF16)

    def knorm(k, g):
        msk = _dot_x_exact(k * k, segk) * (1.0 / HEAD_DIM)
        return k * lax.rsqrt(msk + EPS) * g

    ks = knorm(kv[:, 2 * LANES:3 * LANES], kn[0:1, :])
    lane = lax.broadcasted_iota(jnp.int32, (tm, LANES), 1)
    tok = first_pos + lax.broadcasted_iota(jnp.int32, (tm, LANES), 0)
    onehot = (lane - HEAD_DIM == (tok // SEL_BLOCK) % (KEY_TILE // SEL_BLOCK)).astype(F32)
    ks_ref[0, 0] = jnp.where(lane < HEAD_DIM, ks, onehot).astype(BF16)
    ks_ref[0, 1] = jnp.where(lane < HEAD_DIM, pltpu.roll(ks, HEAD_DIM, 1), onehot).astype(BF16)
    kw_ref[0] = knorm(kv[:, 4 * LANES:5 * LANES], kn[1:2, :]).astype(BF16)
    ones_rows = (lax.broadcasted_iota(jnp.int32, (V_ROWS - HEAD_DIM, tm), 0) == 0).astype(BF16)
    for v_ref, cols in ((vs_ref, 3), (vw_ref, 5)):
        vT = kv[:, cols * LANES:(cols + 1) * LANES].T.astype(BF16)
        for h in range(NSA_KV_HEADS):
            v_ref[0, h, :HEAD_DIM, :] = vT[h * HEAD_DIM:(h + 1) * HEAD_DIM, :]
            v_ref[0, h, HEAD_DIM:, :] = ones_rows


def _nsa_compress_body(x_ref, wd_ref, pos_ref, w2k_ref, w2vT_ref, kn_ref, kc_ref, vcT_ref):
    branch = pl.program_id(1)
    nr = x_ref.shape[1] // CMP_STRIDE
    hid2 = NSA_KV_HEADS * CMP_HIDDEN
    ya = jnp.zeros((nr, hid2), F32)
    yb = jnp.zeros((nr, hid2), F32)
    for l in range(CMP_STRIDE):
        xl = x_ref[0, pl.ds(l, nr, stride=CMP_STRIDE), :].astype(BF16)
        ya = ya + jnp.dot(xl, wd_ref[0, l * LANES:(l + 1) * LANES, :], preferred_element_type=F32)
        yb = yb + jnp.dot(xl, wd_ref[0, (CMP_STRIDE + l) * LANES:(CMP_STRIDE + l + 1) * LANES, :],
                          preferred_element_type=F32)
    bias = jnp.dot(pos_ref[0], wd_ref[0], preferred_element_type=F32)[0:1, :]
    hidden = _silu(ya + pltpu.roll(yb, nr - 1, 0) + bias).astype(BF16)

    @pl.when(branch == 0)
    def _():
        for h in range(NSA_KV_HEADS):
            kc = jnp.dot(hidden[:, h * CMP_HIDDEN:(h + 1) * CMP_HIDDEN], w2k_ref[...], preferred_element_type=F32)
            kc_ref[0, h] = _rms(kc, kn_ref[...]).astype(BF16)

    @pl.when(branch == 1)
    def _():
        for h in range(NSA_KV_HEADS):
            vcT_ref[0, h] = lax.dot_general(w2vT_ref[...], hidden[:, h * CMP_HIDDEN:(h + 1) * CMP_HIDDEN],
                                            (((1,), (1,)), ((), ())), preferred_element_type=F32).astype(BF16)


def _nsa_compress(kvb, cmp_pos, cmp_w1, cmp_w2, k_norm0):
    B, T, _ = kvb.shape
    nr = T // CMP_STRIDE
    w1 = cmp_w1.reshape(2, CMP_LEN, HEAD_DIM, CMP_HIDDEN)
    z = jnp.zeros_like(w1)
    wd = jnp.concatenate([jnp.concatenate([w1, z], axis=3), jnp.concatenate([z, w1], axis=3)], axis=2)
    wd = wd.reshape(2, CMP_LEN * LANES, NSA_KV_HEADS * CMP_HIDDEN).astype(BF16)
    pos = jnp.tile(cmp_pos, (1, 1, NSA_KV_HEADS)).reshape(2, 1, CMP_LEN * LANES)
    pos = jnp.broadcast_to(pos, (2, SUBLANES, CMP_LEN * LANES)).astype(BF16)
    return pl.pallas_call(
        _nsa_compress_body,
        grid=(B, 2),
        in_specs=[pl.BlockSpec((1, T, LANES), lambda b, r: (b, 0, r)),
                  pl.BlockSpec((1,) + wd.shape[1:], lambda b, r: (r, 0, 0)),
                  pl.BlockSpec((1,) + pos.shape[1:], lambda b, r: (r, 0, 0)),
                  _const_spec((CMP_HIDDEN, HEAD_DIM)), _const_spec((HEAD_DIM, CMP_HIDDEN)),
                  _const_spec((1, HEAD_DIM))],
        out_specs=[pl.BlockSpec((1, NSA_KV_HEADS, nr, HEAD_DIM), lambda b, r: (b, 0, 0, 0)),
                   pl.BlockSpec((1, NSA_KV_HEADS, HEAD_DIM, nr), lambda b, r: (b, 0, 0, 0))],
        out_shape=[jax.ShapeDtypeStruct((B, NSA_KV_HEADS, nr, HEAD_DIM), BF16),
                   jax.ShapeDtypeStruct((B, NSA_KV_HEADS, HEAD_DIM, nr), BF16)],
        compiler_params=_params(("parallel", "arbitrary")),
        name="nsa_compress",
    )(kvb, wd, pos, cmp_w2[0].astype(BF16), cmp_w2[1].T.astype(BF16), k_norm0.reshape(1, HEAD_DIM))


def _nsa_attn_body(qT_ref, kc_ref, vcT_ref, ks_ref, vs_ref, kw_ref, vw_ref, gate_ref, selm_ref,
                   o_ref, selb_ref, gT_ref, qaug_ref, m_ref, acc_ref, sa_ref, sb_ref, oc_ref, ow_ref, *, seq):
    G, Dh, Qb = NSA_GROUP, HEAD_DIM, Q_BLOCK
    h = pl.program_id(1)
    t0 = pl.program_id(2) * Qb
    nr = seq // CMP_STRIDE
    ns = seq // SEL_BLOCK
    pos = t0 + lax.broadcasted_iota(jnp.int32, (1, Qb), 1)

    qT = qT_ref[0]
    q3 = jnp.concatenate([qT[g * Dh:(g + 1) * Dh, :] for g in range(G)], axis=1)
    qaug_ref[...] = jnp.concatenate([q3, jnp.zeros_like(q3)], axis=0)
    row = lax.broadcasted_iota(jnp.int32, (2 * Dh, G * Qb), 0)
    qpad = jnp.where((row < Dh) == (h == 0), jnp.concatenate([q3, q3], axis=0), 0).astype(BF16)

    def window_branch():
        ws = pl.multiple_of(jnp.maximum(t0 - WINDOW, 0), Qb)
        sw = jnp.dot(kw_ref[0, pl.ds(ws, WIN_KEYS), :], qpad, preferred_element_type=F32)
        vwx = vw_ref[0, 0, :, pl.ds(ws, WIN_KEYS)]
        kpos = ws + lax.broadcasted_iota(jnp.int32, (WIN_KEYS, 1), 0)
        bias_w = jnp.where((kpos <= pos) & (kpos > pos - WINDOW), 0.0, NEG)
        for g in range(G):
            s = sw[:, g * Qb:(g + 1) * Qb] + bias_w
            e = jnp.exp2(s - jnp.max(s, axis=0, keepdims=True)).astype(BF16)
            ow_ref[g] = jnp.dot(vwx, e, preferred_element_type=F32)

    def compressed_and_select(n_c):
        window_branch()
        n_s = n_c * CMP_STRIDE // SEL_BLOCK
        sc = jnp.dot(kc_ref[0, 0, :n_c, :], q3, preferred_element_type=F32)
        cmp_end = lax.broadcasted_iota(jnp.int32, (n_c, 1), 0) * CMP_STRIDE + (CMP_LEN - 1)
        bias_c = jnp.where(cmp_end <= pos, 0.0, NEG)
        has_c = pos >= CMP_LEN - 1
        lhs = jnp.concatenate([vcT_ref[0, 0, :, :n_c], selm_ref[:n_s, :n_c]], axis=0)
        imp = jnp.zeros((n_s, Qb), F32)
        for g in range(G):
            s = sc[:, g * Qb:(g + 1) * Qb] + bias_c
            e = jnp.exp2(s - jnp.max(s, axis=0, keepdims=True))
            p = (e * jnp.where(has_c, 1.0 / jnp.sum(e, axis=0, keepdims=True), 0.0)).astype(BF16)
            both = jnp.dot(lhs, p, preferred_element_type=F32)
            oc_ref[g] = both[:Dh]
            imp = imp + both[Dh:]

        blk = lax.broadcasted_iota(jnp.int32, (n_s, Qb), 0)
        cur = pos // SEL_BLOCK
        forced = (blk == 0) | (blk == cur) | (blk == cur - 1)
        score = jnp.where(forced | (blk > cur), -jnp.inf, imp)
        for _ in range(SEL_TOPK - 3):
            best = jnp.max(score, axis=0, keepdims=True)
            first = jnp.min(jnp.where(score == best, blk, n_s), axis=0, keepdims=True)
            score = jnp.where(blk == first, -jnp.inf, score)
        selb_ref[:n_s, :] = jnp.where((score == -jnp.inf) & (blk <= cur), 0.0, NEG)

    chunk = min(CMP_CHUNK, nr)
    need = (t0 + Qb) // CMP_STRIDE
    for k in range(1, nr // chunk + 1):
        pl.when((need + chunk - 1) // chunk == k)(functools.partial(compressed_and_select, k * chunk))

    bpt = KEY_TILE // SEL_BLOCK
    krow = lax.broadcasted_iota(jnp.int32, (KEY_TILE, 1), 0)
    m_ref[...] = jnp.full(m_ref.shape, NEG, F32)
    acc_ref[...] = jnp.zeros(acc_ref.shape, F32)

    def scores(kt, slot_ref, nk=KEY_TILE):
        base = pl.multiple_of(kt * KEY_TILE, KEY_TILE)
        sb = selb_ref[pl.ds(pl.multiple_of(kt * bpt, bpt), bpt), :]
        qaug_ref[Dh:Dh + bpt, :] = jnp.concatenate([sb] * G, axis=1).astype(BF16)
        slot_ref[:nk, :] = jnp.dot(ks_ref[0, 0, pl.ds(base, nk), :], qaug_ref[...],
                                   preferred_element_type=F32)

    def consume(kt, slot_ref, diagonal, nk=KEY_TILE):
        base = pl.multiple_of(kt * KEY_TILE, KEY_TILE)
        vx = vs_ref[0, 0, :, pl.ds(base, nk)]
        for g in range(G):
            s = slot_ref[:nk, g * Qb:(g + 1) * Qb]
            if diagonal:
                s = jnp.where(base + krow[:nk] <= pos, s, NEG)
            m_old = m_ref[g:g + 1, :]
            m_new = jnp.maximum(m_old, jnp.max(s, axis=0, keepdims=True))
            e = jnp.exp2(s - m_new).astype(BF16)
            acc_ref[g] = jnp.exp2(m_old - m_new) * acc_ref[g] + jnp.dot(vx, e, preferred_element_type=F32)
            m_ref[g:g + 1, :] = m_new

    n_off = (t0 + Qb + KEY_TILE - 1) // KEY_TILE - 1
    scores(0, sa_ref)

    def pair(p, carry):
        scores(2 * p + 1, sb_ref)
        consume(2 * p, sa_ref, False)
        scores(2 * p + 2, sa_ref)
        consume(2 * p + 1, sb_ref, False)
        return carry

    lax.fori_loop(0, n_off // 2, pair, 0)

    def last_odd(nk):
        scores(n_off, sb_ref, nk)
        consume(n_off - 1, sa_ref, False)
        consume(n_off, sb_ref, True, nk)

    def last_even(nk):
        consume(n_off, sa_ref, True, nk)

    q_in_tile = (t0 % KEY_TILE) // Qb
    for r in range(KEY_TILE // Qb):
        pl.when((n_off % 2 == 1) & (q_in_tile == r))(functools.partial(last_odd, (r + 1) * Qb))
        pl.when((n_off % 2 == 0) & (q_in_tile == r))(functools.partial(last_even, (r + 1) * Qb))

    gT_ref[...] = _sigmoid(gate_ref[0]).T
    gT = gT_ref[0:3 * SUBLANES, :]
    for g in range(G):
        r0 = g * 3
        gates = jnp.where(h == 0, gT[r0:r0 + 3], gT[G * 3 + r0:G * 3 + r0 + 3])
        a_s, a_w = acc_ref[g], ow_ref[g]
        o_s = a_s[:Dh] * (1.0 / a_s[Dh:Dh + 1])
        o_w = a_w[:Dh] * (1.0 / a_w[Dh:Dh + 1])
        o_ref[0, g * Dh:(g + 1) * Dh, :] = gates[0:1] * oc_ref[g] + gates[1:2] * o_s + gates[2:3] * o_w


def _nsa_attn(qT, kc, vcT, ks, vs, kw, vw, gate):
    B, _, T = qT.shape
    assert T % KEY_TILE == 0 and T >= WIN_KEYS
    nr, ns = T // CMP_STRIDE, T // SEL_BLOCK
    G, Dh, Qb = NSA_GROUP, HEAD_DIM, Q_BLOCK
    j = jnp.arange(ns)[:, None]
    c = jnp.arange(nr)[None, :]
    ratio = SEL_BLOCK // CMP_STRIDE
    selm = ((c >= ratio * j - 1) & (c <= ratio * j + ratio - 1) & (c < nr - 1)).astype(BF16)
    vfull = pl.BlockSpec((1, 1, V_ROWS, T), lambda b, h, i: (b, h, 0, 0))
    return pl.pallas_call(
        functools.partial(_nsa_attn_body, seq=T),
        grid=(B, NSA_KV_HEADS, T // Qb),
        in_specs=[pl.BlockSpec((1, G * Dh, Qb), lambda b, h, i: (b, h, i)),
                  pl.BlockSpec((1, 1, nr, Dh), lambda b, h, i: (b, h, 0, 0)),
                  pl.BlockSpec((1, 1, Dh, nr), lambda b, h, i: (b, h, 0, 0)),
                  pl.BlockSpec((1, 1, T, LANES), lambda b, h, i: (b, h, 0, 0)), vfull,
                  pl.BlockSpec((1, T, LANES), lambda b, h, i: (b, 0, 0)), vfull,
                  pl.BlockSpec((1, Qb, LANES), lambda b, h, i: (b, i, 0)),
                  _const_spec(selm.shape)],
        out_specs=pl.BlockSpec((1, G * Dh, Qb), lambda b, h, i: (b, h, i)),
        out_shape=jax.ShapeDtypeStruct((B, NSA_WIDTH, T), F32),
        scratch_shapes=[pltpu.VMEM((ns, Qb), F32), pltpu.VMEM((LANES, Qb), F32),
                        pltpu.VMEM((2 * Dh, G * Qb), BF16), pltpu.VMEM((SUBLANES, Qb), F32),
                        pltpu.VMEM((G, V_ROWS, Qb), F32),
                        pltpu.VMEM((KEY_TILE, G * Qb), F32), pltpu.VMEM((KEY_TILE, G * Qb), F32),
                        pltpu.VMEM((G, Dh, Qb), F32), pltpu.VMEM((G, V_ROWS, Qb), F32)],
        compiler_params=_params(("parallel", "parallel", "arbitrary")),
        name="nsa_attn",
    )(qT, kc, vcT, ks, vs, kw, vw, gate, selm)


POOL_HALO = 16
FF_TILE = 1024


def _pool_tile(u, halo, first_pos, w, scale):
    tm = u.shape[0]
    ue = jnp.concatenate([halo, u], axis=0)
    grp = lax.broadcasted_iota(jnp.int32, (1, POOL_WIDTH), 1) // POOL_GROUP_DIM
    acc = ue
    wsum = jnp.zeros_like(ue)
    width = jnp.zeros((1, POOL_WIDTH), F32)
    for gi, win in enumerate(POOL_WINDOWS):
        acc = acc + pltpu.roll(acc, win // 2, 0)
        wsum = jnp.where(grp == gi, acc, wsum)
        width = jnp.where(grp == gi, float(win), width)
    t1 = (first_pos + 1 + lax.broadcasted_iota(jnp.int32, (tm, 1), 0)).astype(F32)
    mean = wsum[POOL_HALO:] / jnp.minimum(t1, width)
    return jnp.dot((mean - u).astype(BF16), w, preferred_element_type=F32) * scale


def _post_body(x_ref, ya_ref, ybT_ref, u_ref, halo_ref, wa_ref, wb_ref, wc_ref, pw_ref, psc_ref, g_ref,
               w1_ref, w2_ref, o_ref):
    i = pl.program_id(1)
    tm = x_ref.shape[1]
    halo = jnp.where(i > 0, halo_ref[0], 0.0)
    yc = _pool_tile(u_ref[0], halo, i * tm, pw_ref[...], psc_ref[...])
    x = x_ref[0]
    x = x + jnp.dot(ya_ref[0].astype(BF16), wa_ref[...], preferred_element_type=F32)
    x = x + jnp.dot(ybT_ref[0].T.astype(BF16), wb_ref[...], preferred_element_type=F32)
    x = x + jnp.dot(yc.astype(BF16), wc_ref[...], preferred_element_type=F32)
    o_ref[0] = x
    x = o_ref[0]
    h = _rms(x, g_ref[...]).astype(BF16)
    acc = x
    for j in range(D_FF // FF_TILE):
        cols = slice(j * FF_TILE, (j + 1) * FF_TILE)
        a = jnp.maximum(jnp.dot(h, w1_ref[:, cols], preferred_element_type=F32), 0.0)
        acc = acc + jnp.dot((a * a).astype(BF16), w2_ref[cols, :], preferred_element_type=F32)
    o_ref[0] = acc


def _post(x, ya, ybT, u, pool_w, pool_scale, w_out, g, w1, w2, tm=1024):
    B, T, _ = x.shape
    w = w_out.astype(BF16)
    wa, wb, wc = w[:GDN_WIDTH], w[GDN_WIDTH:GDN_WIDTH + NSA_WIDTH], w[GDN_WIDTH + NSA_WIDTH:]
    wblk = jax.scipy.linalg.block_diag(*[pool_w[gi] for gi in range(len(POOL_WINDOWS))]).astype(BF16)
    tok = lambda c: pl.BlockSpec((1, tm, c), lambda b, i: (b, i, 0))
    hb = tm // POOL_HALO
    return pl.pallas_call(
        _post_body,
        grid=(B, T // tm),
        in_specs=[tok(D_MODEL), tok(GDN_WIDTH), pl.BlockSpec((1, NSA_WIDTH, tm), lambda b, i: (b, 0, i)),
                  tok(POOL_WIDTH),
                  pl.BlockSpec((1, POOL_HALO, POOL_WIDTH), lambda b, i: (b, jnp.maximum(i * hb - 1, 0), 0)),
                  _const_spec(wa.shape), _const_spec(wb.shape), _const_spec(wc.shape),
                  _const_spec(wblk.shape), _const_spec((1, POOL_WIDTH)), _const_spec((1, D_MODEL)),
                  _resident_spec(w1.shape), _resident_spec(w2.shape)],
        out_specs=tok(D_MODEL),
        out_shape=jax.ShapeDtypeStruct(x.shape, F32),
        compiler_params=_params(("parallel", "parallel")),
        name="post",
    )(x, ya, ybT, u, u, wa, wb, wc, wblk, pool_scale.reshape(1, POOL_WIDTH), g.reshape(1, D_MODEL),
      w1.astype(BF16), w2.astype(BF16))


def _pad_in_weights(w):
    H = GDN_HEADS
    o_ba = 4 * GDN_WIDTH
    o_q = o_ba + 2 * H
    o_gate = o_q + NSA_WIDTH + 6 * LANES
    o_u = o_gate + 3 * NSA_HEADS
    w = w.astype(BF16)
    pad = lambda a: jnp.pad(a, [(0, 0)] * (a.ndim - 1) + [(0, LANES - a.shape[-1])])
    return jnp.concatenate([w[..., :o_ba], pad(w[..., o_ba:o_q]), w[..., o_q:o_gate], pad(w[..., o_gate:o_u]),
                            w[..., o_u:]], axis=-1)


def kernel(x, norm_mix, w_in, conv_w, a_log, dt_bias, gdn_norm, nsa_q_norm, nsa_k_norm, cmp_pos, cmp_w1, cmp_w2,
           pool_w, pool_scale, w_out, norm_ffn, w_ffn1, w_ffn2):
    w_in_pad = _pad_in_weights(w_in)
    for l in range(w_in.shape[0]):
        qkv, z, ba, kvc, gate, u, qT, ks, kw, vs, vw = _inproj(x, norm_mix[l], w_in_pad[l], nsa_q_norm[l],
                                                               nsa_k_norm[l])
        ya = _gdn(qkv, ba, z, conv_w[l], a_log[l], dt_bias[l], gdn_norm[l])
        kc, vcT = _nsa_compress(kvc, cmp_pos[l], cmp_w1[l], cmp_w2[l], nsa_k_norm[l][0])
        ybT = _nsa_attn(qT, kc, vcT, ks, vs, kw, vw, gate)
        x = _post(x, ya, ybT, u, pool_w[l], pool_scale[l], w_out[l], norm_ffn[l], w_ffn1[l], w_ffn2[l])
    return x
```

```python
import functools

import jax
import jax.numpy as jnp
from jax import lax
from jax.experimental import pallas as pl
from jax.experimental.pallas import tpu as pltpu

F32 = jnp.float32
BF16 = jnp.bfloat16

D_MODEL = 1024
HEAD_DIM = 64
GDN_HEADS = 6
GDN_CHUNK = 64
GDN_WIDTH = GDN_HEADS * HEAD_DIM
NSA_HEADS = 6
NSA_KV_HEADS = 2
NSA_GROUP = NSA_HEADS // NSA_KV_HEADS
NSA_WIDTH = NSA_HEADS * HEAD_DIM
CMP_LEN = 32
CMP_STRIDE = 16
CMP_HIDDEN = 256
SEL_BLOCK = 64
SEL_TOPK = 16
WINDOW = 512
Q_BLOCK = 256
POOL_WINDOWS = (2, 4, 8, 16)
POOL_GROUP_DIM = 64
POOL_WIDTH = 4 * POOL_GROUP_DIM
D_FF = 4 * D_MODEL
EPS = 1e-6

LANES = 128
SUBLANES = 8
VMEM_LIMIT = 56 * 1024 * 1024
NEG = -1e30
LOG2E = 1.4426950408889634
KEY_TILE = 1024
WIN_KEYS = WINDOW + Q_BLOCK
CMP_CHUNK = 128
V_ROWS = 80

IN_GROUPS = (3 * GDN_WIDTH, GDN_WIDTH, LANES, NSA_WIDTH, 6 * LANES, LANES, POOL_WIDTH)


def _sigmoid(x):
    return 1.0 / (1.0 + jnp.exp(-x))


def _silu(x):
    return x * _sigmoid(x)


def _softplus(x):
    return jnp.maximum(x, 0.0) + jnp.log(1.0 + jnp.exp(-jnp.abs(x)))


def _split_bf16(x):
    hi = x.astype(BF16)
    lo = (x - hi.astype(F32)).astype(BF16)
    return hi, lo


def _dot_x_exact(x, m):
    hi, lo = _split_bf16(x)
    return jnp.dot(hi, m, preferred_element_type=F32) + jnp.dot(lo, m, preferred_element_type=F32)


def _mm(a, b):
    return jnp.dot(a.astype(BF16), b.astype(BF16), preferred_element_type=F32)


def _mm_nt(a, b):
    return lax.dot_general(a.astype(BF16), b.astype(BF16), (((1,), (1,)), ((), ())),
                           preferred_element_type=F32)


def _rms(x, g):
    return x * lax.rsqrt(jnp.mean(x * x, axis=-1, keepdims=True) + EPS) * g


def _const_spec(shape):
    nd = len(shape)
    return pl.BlockSpec(shape, lambda *_: (0,) * nd)


def _resident_spec(shape):
    nd = len(shape)
    return pl.BlockSpec(shape, lambda *_: (0,) * nd, pipeline_mode=pl.Buffered(1))


def _params(sem):
    return pltpu.CompilerParams(dimension_semantics=sem, vmem_limit_bytes=VMEM_LIMIT)


def _inproj_body(x_ref, g_ref, w_ref, segq_ref, segk_ref, qn_ref, kn_ref,
                 qkv_ref, z_ref, ba_ref, kvc_ref, gate_ref, u_ref, qT_ref, ks_ref, kw_ref, vs_ref, vw_ref):
    tm = x_ref.shape[1]
    h = _rms(x_ref[0], g_ref[...]).astype(BF16)
    offs = [0]
    for width in IN_GROUPS:
        offs.append(offs[-1] + width)
    split = offs[2]
    wide = (jnp.dot(h, w_ref[:, :split], preferred_element_type=F32),
            jnp.dot(h, w_ref[:, split:], preferred_element_type=F32))
    proj = lambda gi: (wide[0][:, offs[gi]:offs[gi + 1]] if gi < 2 else
                       wide[1][:, offs[gi] - split:offs[gi + 1] - split])
    qkv_ref[0] = proj(0)
    z_ref[0] = proj(1)
    ba_ref[0] = proj(2)
    gate_ref[0] = proj(5)
    u_ref[0] = proj(6)
    kv = proj(4)
    kvc_ref[0] = kv[:, :2 * LANES]
    _nsa_layouts(proj(3), kv, pl.program_id(1) * tm, segq_ref[...], segk_ref[...], qn_ref[...], kn_ref[...],
                 qT_ref, ks_ref, kw_ref, vs_ref, vw_ref)


def _inproj(x, g, w_pad, q_norm, k_norm, tm=1024):
    B, T, _ = x.shape
    chq = jnp.arange(NSA_WIDTH) // HEAD_DIM
    segq = (chq[:, None] == chq[None, :]).astype(BF16)
    segk = segq[:LANES, :LANES]
    qn = jnp.tile(q_norm, NSA_HEADS).reshape(1, NSA_WIDTH)
    kn = jnp.tile(k_norm[1:3], (1, NSA_KV_HEADS))
    tok = lambda c: pl.BlockSpec((1, tm, c), lambda b, i: (b, i, 0))
    f32 = lambda c: jax.ShapeDtypeStruct((B, T, c), F32)
    vspec = pl.BlockSpec((1, NSA_KV_HEADS, V_ROWS, tm), lambda b, i: (b, 0, 0, i))
    vshape = jax.ShapeDtypeStruct((B, NSA_KV_HEADS, V_ROWS, T), BF16)
    return pl.pallas_call(
        _inproj_body,
        grid=(B, T // tm),
        in_specs=[tok(D_MODEL), _const_spec((1, D_MODEL)), _resident_spec(w_pad.shape),
                  _const_spec(segq.shape), _const_spec(segk.shape), _const_spec(qn.shape), _const_spec(kn.shape)],
        out_specs=[tok(3 * GDN_WIDTH), tok(GDN_WIDTH), tok(LANES), tok(2 * LANES), tok(LANES), tok(POOL_WIDTH),
                   pl.BlockSpec((1, NSA_WIDTH, tm), lambda b, i: (b, 0, i)),
                   pl.BlockSpec((1, NSA_KV_HEADS, tm, LANES), lambda b, i: (b, 0, i, 0)),
                   tok(LANES), vspec, vspec],
        out_shape=[f32(3 * GDN_WIDTH), f32(GDN_WIDTH), f32(LANES), f32(2 * LANES), f32(LANES), f32(POOL_WIDTH),
                   jax.ShapeDtypeStruct((B, NSA_WIDTH, T), BF16),
                   jax.ShapeDtypeStruct((B, NSA_KV_HEADS, T, LANES), BF16),
                   jax.ShapeDtypeStruct((B, T, LANES), BF16),
                   vshape, vshape],
        compiler_params=_params(("parallel", "parallel")),
        name="inproj",
    )(x, g.reshape(1, D_MODEL), w_pad, segq, segk, qn, kn)


def _gdn_inputs(x, tail, ba, cw, alog, dtb, seg, eb, eg):
    xe = jnp.concatenate([tail, x], axis=0)
    y = x * cw[3:4, :]
    for s in (1, 2, 3):
        y = y + pltpu.roll(xe, s, 0)[SUBLANES:] * cw[3 - s:4 - s, :]
    a = _silu(y)
    q = a[:, :GDN_WIDTH]
    k = a[:, GDN_WIDTH:2 * GDN_WIDTH]
    qn = q * lax.rsqrt(_dot_x_exact(q * q, seg) + EPS) * (HEAD_DIM ** -0.5)
    kn = k * lax.rsqrt(_dot_x_exact(k * k, seg) + EPS)
    gval = -jnp.exp(alog) * _softplus(ba + dtb)
    return qn, kn, a[:, 2 * GDN_WIDTH:], _dot_x_exact(_sigmoid(ba), eb), _dot_x_exact(gval, eg)


def _gdn_body(qkv_ref, ba_ref, z_ref, cw_ref, alog_ref, dtb_ref, seg_ref, eb_ref, eg_ref, tri_ref, gn_ref,
              o_ref, s_ref, tail_ref, *, cps):
    C = GDN_CHUNK
    P = 2 * C
    B = qkv_ref.shape[0]
    NP = GDN_WIDTH // LANES
    tm = qkv_ref.shape[1]

    @pl.when(pl.program_id(0) == 0)
    def _():
        s_ref[...] = jnp.zeros_like(s_ref)
        tail_ref[...] = jnp.zeros_like(tail_ref)

    feats = []
    for b in range(B):
        x = qkv_ref[b]
        feats.append(_gdn_inputs(x, tail_ref[b], ba_ref[b], cw_ref[...], alog_ref[...], dtb_ref[...],
                                 seg_ref[...], eb_ref[...], eg_ref[...]))
        tail_ref[b] = x[tm - SUBLANES:, :]

    lo_lane = lax.broadcasted_iota(jnp.int32, (C, LANES), 1) < HEAD_DIM
    r = lax.broadcasted_iota(jnp.int32, (P, P), 0)
    c = lax.broadcasted_iota(jnp.int32, (P, P), 1)
    same = (r < C) == (c < C)
    causal = same & (r >= c)
    strict = same & (r > c)
    eye = (r == c).astype(F32)
    tri = tri_ref[...]
    gn = gn_ref[...]

    def blk(full, j, p):
        x = full[C * j:C * (j + 1), LANES * p:LANES * (p + 1)]
        return jnp.concatenate([jnp.where(lo_lane, x, 0.0), jnp.where(lo_lane, 0.0, x)], axis=0)

    chains = [(b, p) for b in range(B) for p in range(NP)]
    st = {}
    _gdn_local_stages([(b, p, j) for j in range(cps) for (b, p) in chains], st, feats, blk, tri, causal, strict,
                      same, eye, r, c)

    S = {ch: s_ref[ch[0], ch[1]] for ch in chains}
    for j in range(cps):
        for ch in chains:
            b, p = ch
            d = st[(b, p, j)]
            ws = _mm(jnp.concatenate([d["w"], d["qd"]], axis=0), S[ch])
            vnew = d["u"] - ws[:P]
            ov = _mm(jnp.concatenate([d["qk"], d["kdT"]], axis=0), vnew)
            o2 = ws[P:] + ov[:P]
            S[ch] = S[ch] * d["eg"] + ov[P:]
            ms = jnp.sum(o2 * o2, axis=-1, keepdims=True) * (1.0 / HEAD_DIM)
            on = o2 * lax.rsqrt(ms + EPS) * gn
            z = z_ref[b, C * j:C * (j + 1), LANES * p:LANES * (p + 1)]
            o_ref[b, C * j:C * (j + 1), LANES * p:LANES * (p + 1)] = (on[:C] + on[C:]) * _silu(z)
    for ch in chains:
        s_ref[ch[0], ch[1]] = S[ch]


def _gdn_local_stages(units, st, feats, blk, tri, causal, strict, same, eye, r, c):
    C = GDN_CHUNK
    P = 2 * C
    for u in units:
        b, p, j = u
        q2, k2, v2, b2, g2 = [blk(f, j, p) for f in feats[b]]
        g_hi, g_lo = _split_bf16(g2)
        gcc = jnp.dot(tri, jnp.concatenate([g_hi, g_lo], axis=1), preferred_element_type=F32)
        gc = gcc[:, :LANES] + gcc[:, LANES:]
        decay = jnp.exp(jnp.where(causal, gc - gc.T, NEG))
        kb = k2 * b2
        gram = _mm_nt(jnp.concatenate([kb, q2], axis=0), k2)
        egc = jnp.exp(gc)
        glast = jnp.concatenate([jnp.broadcast_to(gc[C - 1:C, :], (C, LANES)),
                                 jnp.broadcast_to(gc[P - 1:P, :], (C, LANES))], axis=0)
        st[u] = dict(L=jnp.where(strict, gram[:P] * decay, 0.0), qk=gram[P:] * decay,
                     rhs=jnp.concatenate([v2 * b2, kb * egc], axis=1), qd=q2 * egc,
                     kdT=(k2 * jnp.exp(glast - gc)).T, eg=jnp.exp(glast))

    for u in units:
        st[u]["X"] = eye
    s = 1
    while s < C:
        couple = same & ((r // (2 * s)) == (c // (2 * s))) & (((r // s) % 2) == 1) & (((c // s) % 2) == 0)
        for u in units:
            X = st[u]["X"]
            st[u]["X"] = X - _mm(X, _mm(jnp.where(couple, st[u]["L"], 0.0), X))
        s *= 2
    for u in units:
        st[u].pop("L")
        uw = _mm(st[u].pop("X"), st[u].pop("rhs"))
        st[u]["u"], st[u]["w"] = uw[:, :LANES], uw[:, LANES:]


def _gdn(qkv, ba, z, conv_w, a_log, dt_bias, gdn_norm, cps=4):
    B, T, width = qkv.shape
    C = GDN_CHUNK
    H = GDN_HEADS
    tm = C * cps
    alog_p = jnp.zeros((1, LANES), F32).at[0, H:2 * H].set(a_log)
    dtb_p = jnp.zeros((1, LANES), F32).at[0, H:2 * H].set(dt_bias)
    ch = jnp.arange(GDN_WIDTH) // HEAD_DIM
    seg = (ch[:, None] == ch[None, :]).astype(BF16)
    row = jnp.arange(LANES)
    eb = (row[:, None] == ch[None, :]).astype(BF16)
    eg = (row[:, None] - H == ch[None, :]).astype(BF16)
    i2 = jnp.arange(2 * C)
    tri = (((i2[:, None] // C) == (i2[None, :] // C)) & (i2[:, None] >= i2[None, :])).astype(BF16)
    gn2 = jnp.tile(gdn_norm, 2).reshape(1, LANES)
    tok = lambda c: pl.BlockSpec((B, tm, c), lambda i: (0, i, 0))
    consts = (conv_w, alog_p, dtb_p, seg, eb, eg, tri, gn2)
    return pl.pallas_call(
        functools.partial(_gdn_body, cps=cps),
        grid=(T // tm,),
        in_specs=[tok(width), tok(LANES), tok(GDN_WIDTH)] + [_const_spec(c.shape) for c in consts],
        out_specs=tok(GDN_WIDTH),
        out_shape=jax.ShapeDtypeStruct((B, T, GDN_WIDTH), F32),
        scratch_shapes=[pltpu.VMEM((B, GDN_WIDTH // LANES, 2 * C, LANES), F32),
                        pltpu.VMEM((B, SUBLANES, width), F32)],
        compiler_params=_params(("arbitrary",)),
        name="gdn",
    )(qkv, ba, z, *consts)


def _nsa_layouts(q, kv, first_pos, segq, segk, qn, kn, qT_ref, ks_ref, kw_ref, vs_ref, vw_ref):
    tm = q.shape[0]
    ms = _dot_x_exact(q * q, segq) * (1.0 / HEAD_DIM)
    qT_ref[0] = (q * lax.rsqrt(ms + EPS) * qn * (HEAD_DIM ** -0.5 * LOG2E)).T.astype(BF16)

    def knorm(k, g):
        msk = _dot_x_exact(k * k, segk) * (1.0 / HEAD_DIM)
        return k * lax.rsqrt(msk + EPS) * g

    ks = knorm(kv[:, 2 * LANES:3 * LANES], kn[0:1, :])
    lane = lax.broadcasted_iota(jnp.int32, (tm, LANES), 1)
    tok = first_pos + lax.broadcasted_iota(jnp.int32, (tm, LANES), 0)
    onehot = (lane - HEAD_DIM == (tok // SEL_BLOCK) % (KEY_TILE // SEL_BLOCK)).astype(F32)
    ks_ref[0, 0] = jnp.where(lane < HEAD_DIM, ks, onehot).astype(BF16)
    ks_ref[0, 1] = jnp.where(lane < HEAD_DIM, pltpu.roll(ks, HEAD_DIM, 1), onehot).astype(BF16)
    kw_ref[0] = knorm(kv[:, 4 * LANES:5 * LANES], kn[1:2, :]).astype(BF16)
    ones_rows = (lax.broadcasted_iota(jnp.int32, (V_ROWS - HEAD_DIM, tm), 0) == 0).astype(BF16)
    for v_ref, cols in ((vs_ref, 3), (vw_ref, 5)):
        vT = kv[:, cols * LANES:(cols + 1) * LANES].T.astype(BF16)
        for h in range(NSA_KV_HEADS):
            v_ref[0, h, :HEAD_DIM, :] = vT[h * HEAD_DIM:(h + 1) * HEAD_DIM, :]
            v_ref[0, h, HEAD_DIM:, :] = ones_rows


def _nsa_compress_body(x_ref, wd_ref, pos_ref, w2k_ref, w2vT_ref, kn_ref, kc_ref, vcT_ref):
    branch = pl.program_id(1)
    nr = x_ref.shape[1] // CMP_STRIDE
    hid2 = NSA_KV_HEADS * CMP_HIDDEN
    ya = jnp.zeros((nr, hid2), F32)
    yb = jnp.zeros((nr, hid2), F32)
    for l in range(CMP_STRIDE):
        xl = x_ref[0, pl.ds(l, nr, stride=CMP_STRIDE), :].astype(BF16)
        ya = ya + jnp.dot(xl, wd_ref[0, l * LANES:(l + 1) * LANES, :], preferred_element_type=F32)
        yb = yb + jnp.dot(xl, wd_ref[0, (CMP_STRIDE + l) * LANES:(CMP_STRIDE + l + 1) * LANES, :],
                          preferred_element_type=F32)
    bias = jnp.dot(pos_ref[0], wd_ref[0], preferred_element_type=F32)[0:1, :]
    hidden = _silu(ya + pltpu.roll(yb, nr - 1, 0) + bias).astype(BF16)

    @pl.when(branch == 0)
    def _():
        for h in range(NSA_KV_HEADS):
            kc = jnp.dot(hidden[:, h * CMP_HIDDEN:(h + 1) * CMP_HIDDEN], w2k_ref[...], preferred_element_type=F32)
            kc_ref[0, h] = _rms(kc, kn_ref[...]).astype(BF16)

    @pl.when(branch == 1)
    def _():
        for h in range(NSA_KV_HEADS):
            vcT_ref[0, h] = lax.dot_general(w2vT_ref[...], hidden[:, h * CMP_HIDDEN:(h + 1) * CMP_HIDDEN],
                                            (((1,), (1,)), ((), ())), preferred_element_type=F32).astype(BF16)


def _nsa_compress(kvb, cmp_pos, cmp_w1, cmp_w2, k_norm0):
    B, T, _ = kvb.shape
    nr = T // CMP_STRIDE
    w1 = cmp_w1.reshape(2, CMP_LEN, HEAD_DIM, CMP_HIDDEN)
    z = jnp.zeros_like(w1)
    wd = jnp.concatenate([jnp.concatenate([w1, z], axis=3), jnp.concatenate([z, w1], axis=3)], axis=2)
    wd = wd.reshape(2, CMP_LEN * LANES, NSA_KV_HEADS * CMP_HIDDEN).astype(BF16)
    pos = jnp.tile(cmp_pos, (1, 1, NSA_KV_HEADS)).reshape(2, 1, CMP_LEN * LANES)
    pos = jnp.broadcast_to(pos, (2, SUBLANES, CMP_LEN * LANES)).astype(BF16)
    return pl.pallas_call(
        _nsa_compress_body,
        grid=(B, 2),
        in_specs=[pl.BlockSpec((1, T, LANES), lambda b, r: (b, 0, r)),
                  pl.BlockSpec((1,) + wd.shape[1:], lambda b, r: (r, 0, 0)),
                  pl.BlockSpec((1,) + pos.shape[1:], lambda b, r: (r, 0, 0)),
                  _const_spec((CMP_HIDDEN, HEAD_DIM)), _const_spec((HEAD_DIM, CMP_HIDDEN)),
                  _const_spec((1, HEAD_DIM))],
        out_specs=[pl.BlockSpec((1, NSA_KV_HEADS, nr, HEAD_DIM), lambda b, r: (b, 0, 0, 0)),
                   pl.BlockSpec((1, NSA_KV_HEADS, HEAD_DIM, nr), lambda b, r: (b, 0, 0, 0))],
        out_shape=[jax.ShapeDtypeStruct((B, NSA_KV_HEADS, nr, HEAD_DIM), BF16),
                   jax.ShapeDtypeStruct((B, NSA_KV_HEADS, HEAD_DIM, nr), BF16)],
        compiler_params=_params(("parallel", "arbitrary")),
        name="nsa_compress",
    )(kvb, wd, pos, cmp_w2[0].astype(BF16), cmp_w2[1].T.astype(BF16), k_norm0.reshape(1, HEAD_DIM))


def _nsa_attn_body(qT_ref, kc_ref, vcT_ref, ks_ref, vs_ref, kw_ref, vw_ref, gate_ref, selm_ref,
                   o_ref, selb_ref, gT_ref, qaug_ref, m_ref, acc_ref, sa_ref, sb_ref, oc_ref, ow_ref, *, seq):
    G, Dh, Qb = NSA_GROUP, HEAD_DIM, Q_BLOCK
    h = pl.program_id(1)
    t0 = pl.program_id(2) * Qb
    nr = seq // CMP_STRIDE
    ns = seq // SEL_BLOCK
    pos = t0 + lax.broadcasted_iota(jnp.int32, (1, Qb), 1)

    qT = qT_ref[0]
    q3 = jnp.concatenate([qT[g * Dh:(g + 1) * Dh, :] for g in range(G)], axis=1)
    qaug_ref[...] = jnp.concatenate([q3, jnp.zeros_like(q3)], axis=0)
    row = lax.broadcasted_iota(jnp.int32, (2 * Dh, G * Qb), 0)
    qpad = jnp.where((row < Dh) == (h == 0), jnp.concatenate([q3, q3], axis=0), 0).astype(BF16)

    def window_branch():
        ws = pl.multiple_of(jnp.maximum(t0 - WINDOW, 0), Qb)
        sw = jnp.dot(kw_ref[0, pl.ds(ws, WIN_KEYS), :], qpad, preferred_element_type=F32)
        vwx = vw_ref[0, 0, :, pl.ds(ws, WIN_KEYS)]
        kpos = ws + lax.broadcasted_iota(jnp.int32, (WIN_KEYS, 1), 0)
        bias_w = jnp.where((kpos <= pos) & (kpos > pos - WINDOW), 0.0, NEG)
        for g in range(G):
            s = sw[:, g * Qb:(g + 1) * Qb] + bias_w
            e = jnp.exp2(s - jnp.max(s, axis=0, keepdims=True)).astype(BF16)
            ow_ref[g] = jnp.dot(vwx, e, preferred_element_type=F32)

    def compressed_and_select(n_c):
        window_branch()
        n_s = n_c * CMP_STRIDE // SEL_BLOCK
        sc = jnp.dot(kc_ref[0, 0, :n_c, :], q3, preferred_element_type=F32)
        cmp_end = lax.broadcasted_iota(jnp.int32, (n_c, 1), 0) * CMP_STRIDE + (CMP_LEN - 1)
        bias_c = jnp.where(cmp_end <= pos, 0.0, NEG)
        has_c = pos >= CMP_LEN - 1
        lhs = jnp.concatenate([vcT_ref[0, 0, :, :n_c], selm_ref[:n_s, :n_c]], axis=0)
        imp = jnp.zeros((n_s, Qb), F32)
        for g in range(G):
            s = sc[:, g * Qb:(g + 1) * Qb] + bias_c
            e = jnp.exp2(s - jnp.max(s, axis=0, keepdims=True))
            p = (e * jnp.where(has_c, 1.0 / jnp.sum(e, axis=0, keepdims=True), 0.0)).astype(BF16)
            both = jnp.dot(lhs, p, preferred_element_type=F32)
            oc_ref[g] = both[:Dh]
            imp = imp + both[Dh:]

        blk = lax.broadcasted_iota(jnp.int32, (n_s, Qb), 0)
        cur = pos // SEL_BLOCK
        forced = (blk == 0) | (blk == cur) | (blk == cur - 1)
        score = jnp.where(forced | (blk > cur), -jnp.inf, imp)
        for _ in range(SEL_TOPK - 3):
            best = jnp.max(score, axis=0, keepdims=True)
            first = jnp.min(jnp.where(score == best, blk, n_s), axis=0, keepdims=True)
            score = jnp.where(blk == first, -jnp.inf, score)
        selb_ref[:n_s, :] = jnp.where((score == -jnp.inf) & (blk <= cur), 0.0, NEG)

    chunk = min(CMP_CHUNK, nr)
    need = (t0 + Qb) // CMP_STRIDE
    for k in range(1, nr // chunk + 1):
        pl.when((need + chunk - 1) // chunk == k)(functools.partial(compressed_and_select, k * chunk))

    bpt = KEY_TILE // SEL_BLOCK
    krow = lax.broadcasted_iota(jnp.int32, (KEY_TILE, 1), 0)
    m_ref[...] = jnp.full(m_ref.shape, NEG, F32)
    acc_ref[...] = jnp.zeros(acc_ref.shape, F32)

    def scores(kt, slot_ref, nk=KEY_TILE):
        base = pl.multiple_of(kt * KEY_TILE, KEY_TILE)
        sb = selb_ref[pl.ds(pl.multiple_of(kt * bpt, bpt), bpt), :]
        qaug_ref[Dh:Dh + bpt, :] = jnp.concatenate([sb] * G, axis=1).astype(BF16)
        slot_ref[:nk, :] = jnp.dot(ks_ref[0, 0, pl.ds(base, nk), :], qaug_ref[...],
                                   preferred_element_type=F32)

    def consume(kt, slot_ref, diagonal, nk=KEY_TILE):
        base = pl.multiple_of(kt * KEY_TILE, KEY_TILE)
        vx = vs_ref[0, 0, :, pl.ds(base, nk)]
        for g in range(G):
            s = slot_ref[:nk, g * Qb:(g + 1) * Qb]
            if diagonal:
                s = jnp.where(base + krow[:nk] <= pos, s, NEG)
            m_old = m_ref[g:g + 1, :]
            m_new = jnp.maximum(m_old, jnp.max(s, axis=0, keepdims=True))
            e = jnp.exp2(s - m_new).astype(BF16)
            acc_ref[g] = jnp.exp2(m_old - m_new) * acc_ref[g] + jnp.dot(vx, e, preferred_element_type=F32)
            m_ref[g:g + 1, :] = m_new

    n_off = (t0 + Qb + KEY_TILE - 1) // KEY_TILE - 1
    scores(0, sa_ref)

    def pair(p, carry):
        scores(2 * p + 1, sb_ref)
        consume(2 * p, sa_ref, False)
        scores(2 * p + 2, sa_ref)
        consume(2 * p + 1, sb_ref, False)
        return carry

    lax.fori_loop(0, n_off // 2, pair, 0)

    def last_odd(nk):
        scores(n_off, sb_ref, nk)
        consume(n_off - 1, sa_ref, False)
        consume(n_off, sb_ref, True, nk)

    def last_even(nk):
        consume(n_off, sa_ref, True, nk)

    q_in_tile = (t0 % KEY_TILE) // Qb
    for r in range(KEY_TILE // Qb):
        pl.when((n_off % 2 == 1) & (q_in_tile == r))(functools.partial(last_odd, (r + 1) * Qb))
        pl.when((n_off % 2 == 0) & (q_in_tile == r))(functools.partial(last_even, (r + 1) * Qb))

    gT_ref[...] = _sigmoid(gate_ref[0]).T
    gT = gT_ref[0:3 * SUBLANES, :]
    for g in range(G):
        r0 = g * 3
        gates = jnp.where(h == 0, gT[r0:r0 + 3], gT[G * 3 + r0:G * 3 + r0 + 3])
        a_s, a_w = acc_ref[g], ow_ref[g]
        o_s = a_s[:Dh] * (1.0 / a_s[Dh:Dh + 1])
        o_w = a_w[:Dh] * (1.0 / a_w[Dh:Dh + 1])
        o_ref[0, g * Dh:(g + 1) * Dh, :] = gates[0:1] * oc_ref[g] + gates[1:2] * o_s + gates[2:3] * o_w


def _nsa_attn(qT, kc, vcT, ks, vs, kw, vw, gate):
    B, _, T = qT.shape
    assert T % KEY_TILE == 0 and T >= WIN_KEYS
    nr, ns = T // CMP_STRIDE, T // SEL_BLOCK
    G, Dh, Qb = NSA_GROUP, HEAD_DIM, Q_BLOCK
    j = jnp.arange(ns)[:, None]
    c = jnp.arange(nr)[None, :]
    ratio = SEL_BLOCK // CMP_STRIDE
    selm = ((c >= ratio * j - 1) & (c <= ratio * j + ratio - 1) & (c < nr - 1)).astype(BF16)
    vfull = pl.BlockSpec((1, 1, V_ROWS, T), lambda b, h, i: (b, h, 0, 0))
    return pl.pallas_call(
        functools.partial(_nsa_attn_body, seq=T),
        grid=(B, NSA_KV_HEADS, T // Qb),
        in_specs=[pl.BlockSpec((1, G * Dh, Qb), lambda b, h, i: (b, h, i)),
                  pl.BlockSpec((1, 1, nr, Dh), lambda b, h, i: (b, h, 0, 0)),
                  pl.BlockSpec((1, 1, Dh, nr), lambda b, h, i: (b, h, 0, 0)),
                  pl.BlockSpec((1, 1, T, LANES), lambda b, h, i: (b, h, 0, 0)), vfull,
                  pl.BlockSpec((1, T, LANES), lambda b, h, i: (b, 0, 0)), vfull,
                  pl.BlockSpec((1, Qb, LANES), lambda b, h, i: (b, i, 0)),
                  _const_spec(selm.shape)],
        out_specs=pl.BlockSpec((1, G * Dh, Qb), lambda b, h, i: (b, h, i)),
        out_shape=jax.ShapeDtypeStruct((B, NSA_WIDTH, T), F32),
        scratch_shapes=[pltpu.VMEM((ns, Qb), F32), pltpu.VMEM((LANES, Qb), F32),
                        pltpu.VMEM((2 * Dh, G * Qb), BF16), pltpu.VMEM((SUBLANES, Qb), F32),
                        pltpu.VMEM((G, V_ROWS, Qb), F32),
                        pltpu.VMEM((KEY_TILE, G * Qb), F32), pltpu.VMEM((KEY_TILE, G * Qb), F32),
                        pltpu.VMEM((G, Dh, Qb), F32), pltpu.VMEM((G, V_ROWS, Qb), F32)],
        compiler_params=_params(("parallel", "parallel", "arbitrary")),
        name="nsa_attn",
    )(qT, kc, vcT, ks, vs, kw, vw, gate, selm)


POOL_HALO = 16
FF_TILE = 1024


def _pool_tile(u, halo, first_pos, w, scale):
    tm = u.shape[0]
    ue = jnp.concatenate([halo, u], axis=0)
    grp = lax.broadcasted_iota(jnp.int32, (1, POOL_WIDTH), 1) // POOL_GROUP_DIM
    acc = ue
    wsum = jnp.zeros_like(ue)
    width = jnp.zeros((1, POOL_WIDTH), F32)
    for gi, win in enumerate(POOL_WINDOWS):
        acc = acc + pltpu.roll(acc, win // 2, 0)
        wsum = jnp.where(grp == gi, acc, wsum)
        width = jnp.where(grp == gi, float(win), width)
    t1 = (first_pos + 1 + lax.broadcasted_iota(jnp.int32, (tm, 1), 0)).astype(F32)
    mean = wsum[POOL_HALO:] / jnp.minimum(t1, width)
    return jnp.dot((mean - u).astype(BF16), w, preferred_element_type=F32) * scale


def _post_body(x_ref, ya_ref, ybT_ref, u_ref, halo_ref, wo_ref, pw_ref, psc_ref, g_ref, w1_ref, w2_ref, o_ref):
    i = pl.program_id(1)
    tm = x_ref.shape[1]
    halo = jnp.where(i > 0, halo_ref[0], 0.0)
    yc = _pool_tile(u_ref[0], halo, i * tm, pw_ref[...], psc_ref[...])
    y = jnp.concatenate([ya_ref[0].astype(BF16), ybT_ref[0].T.astype(BF16), yc.astype(BF16)], axis=1)
    x = x_ref[0] + jnp.dot(y, wo_ref[...], preferred_element_type=F32)
    o_ref[0] = x
    x = o_ref[0]
    h = _rms(x, g_ref[...]).astype(BF16)
    acc = x
    for j in range(D_FF // FF_TILE):
        cols = slice(j * FF_TILE, (j + 1) * FF_TILE)
        a = jnp.maximum(jnp.dot(h, w1_ref[:, cols], preferred_element_type=F32), 0.0)
        acc = acc + jnp.dot((a * a).astype(BF16), w2_ref[cols, :], preferred_element_type=F32)
    o_ref[0] = acc


def _post(x, ya, ybT, u, pool_w, pool_scale, w_out, g, w1, w2, tm=1024):
    B, T, _ = x.shape
    wblk =jax.scipy.linalg.block_diag(*[pool_w[gi] for gi in range(len(POOL_WINDOWS))]).astype(BF16)
    tok = lambda c: pl.BlockSpec((1, tm, c), lambda b, i: (b, i, 0))
    hb = tm // POOL_HALO
    return pl.pallas_call(
        _post_body,
        grid=(B, T // tm),
        in_specs=[tok(D_MODEL), tok(GDN_WIDTH), pl.BlockSpec((1, NSA_WIDTH, tm), lambda b, i: (b, 0, i)),
                  tok(POOL_WIDTH),
                  pl.BlockSpec((1, POOL_HALO, POOL_WIDTH), lambda b, i: (b, jnp.maximum(i * hb - 1, 0), 0)),
                  _const_spec(w_out.shape), _const_spec(wblk.shape), _const_spec((1, POOL_WIDTH)), _const_spec((1, D_MODEL)),
                  _resident_spec(w1.shape), _resident_spec(w2.shape)],
        out_specs=tok(D_MODEL),
        out_shape=jax.ShapeDtypeStruct(x.shape, F32),
        compiler_params=_params(("parallel", "parallel")),
        name="post",
    )(x, ya, ybT, u, u, w_out.astype(BF16), wblk, pool_scale.reshape(1, POOL_WIDTH), g.reshape(1, D_MODEL),
      w1.astype(BF16), w2.astype(BF16))


def _pad_in_weights(w):
    H = GDN_HEADS
    o_ba = 4 * GDN_WIDTH
    o_q = o_ba + 2 * H
    o_gate = o_q + NSA_WIDTH + 6 * LANES
    o_u = o_gate + 3 * NSA_HEADS
    w = w.astype(BF16)
    pad = lambda a: jnp.pad(a, [(0, 0)] * (a.ndim - 1) + [(0, LANES - a.shape[-1])])
    return jnp.concatenate([w[..., :o_ba], pad(w[..., o_ba:o_q]), w[..., o_q:o_gate], pad(w[..., o_gate:o_u]),
                            w[..., o_u:]], axis=-1)


def kernel(x, norm_mix, w_in, conv_w, a_log, dt_bias, gdn_norm, nsa_q_norm, nsa_k_norm, cmp_pos, cmp_w1, cmp_w2,
           pool_w, pool_scale, w_out, norm_ffn, w_ffn1, w_ffn2):
    w_in_pad = _pad_in_weights(w_in)
    for l in range(w_in.shape[0]):
        qkv, z, ba, kvc, gate, u, qT, ks, kw, vs, vw = _inproj(x, norm_mix[l], w_in_pad[l], nsa_q_norm[l],
                                                               nsa_k_norm[l])
        ya = _gdn(qkv, ba, z, conv_w[l], a_log[l], dt_bias[l], gdn_norm[l])
        kc, vcT = _nsa_compress(kvc, cmp_pos[l], cmp_w1[l], cmp_w2[l], nsa_k_norm[l][0])
        ybT = _nsa_attn(qT, kc, vcT, ks, vs, kw, vw, gate)
        x = _post(x, ya, ybT, u, pool_w[l], pool_scale[l], w_out[l], norm_ffn[l], w_ffn1[l], w_ffn2[l])
    return x
```

```python
import functools

import jax
import jax.numpy as jnp
from jax import lax
from jax.experimental import pallas as pl
from jax.experimental.pallas import tpu as pltpu

F32 = jnp.float32
BF16 = jnp.bfloat16

D_MODEL = 1024
HEAD_DIM = 64
GDN_HEADS = 6
GDN_CHUNK = 64
GDN_WIDTH = GDN_HEADS * HEAD_DIM
NSA_HEADS = 6
NSA_KV_HEADS = 2
NSA_GROUP = NSA_HEADS // NSA_KV_HEADS
NSA_WIDTH = NSA_HEADS * HEAD_DIM
CMP_LEN = 32
CMP_STRIDE = 16
CMP_HIDDEN = 256
SEL_BLOCK = 64
SEL_TOPK = 16
WINDOW = 512
Q_BLOCK = 256
POOL_WINDOWS = (2, 4, 8, 16)
POOL_GROUP_DIM = 64
POOL_WIDTH = 4 * POOL_GROUP_DIM
D_FF = 4 * D_MODEL
EPS = 1e-6

LANES = 128
SUBLANES = 8
VMEM_LIMIT = 56 * 1024 * 1024
NEG = -1e30
LOG2E = 1.4426950408889634
KEY_TILE = 1024
WIN_KEYS = WINDOW + Q_BLOCK
CMP_CHUNK = 128
V_ROWS = 80

IN_GROUPS = (3 * GDN_WIDTH, GDN_WIDTH, LANES, NSA_WIDTH, 6 * LANES, LANES, POOL_WIDTH)


def _sigmoid(x):
    return 1.0 / (1.0 + jnp.exp(-x))


def _silu(x):
    return x * _sigmoid(x)


def _softplus(x):
    return jnp.maximum(x, 0.0) + jnp.log(1.0 + jnp.exp(-jnp.abs(x)))


def _split_bf16(x):
    hi = x.astype(BF16)
    lo = (x - hi.astype(F32)).astype(BF16)
    return hi, lo


def _dot_x_exact(x, m):
    hi, lo = _split_bf16(x)
    return jnp.dot(hi, m, preferred_element_type=F32) + jnp.dot(lo, m, preferred_element_type=F32)


def _mm(a, b):
    return jnp.dot(a.astype(BF16), b.astype(BF16), preferred_element_type=F32)


def _mm_nt(a, b):
    return lax.dot_general(a.astype(BF16), b.astype(BF16), (((1,), (1,)), ((), ())),
                           preferred_element_type=F32)


def _rms(x, g):
    return x * lax.rsqrt(jnp.mean(x * x, axis=-1, keepdims=True) + EPS) * g


def _const_spec(shape):
    nd = len(shape)
    return pl.BlockSpec(shape, lambda *_: (0,) * nd)


def _resident_spec(shape):
    nd = len(shape)
    return pl.BlockSpec(shape, lambda *_: (0,) * nd, pipeline_mode=pl.Buffered(1))


def _params(sem):
    return pltpu.CompilerParams(dimension_semantics=sem, vmem_limit_bytes=VMEM_LIMIT)


def _inproj_body(x_ref, g_ref, w_ref, segq_ref, segk_ref, qn_ref, kn_ref,
                 qkv_ref, z_ref, ba_ref, kvc_ref, gate_ref, u_ref, qT_ref, ks_ref, kw_ref, vs_ref, vw_ref):
    tm = x_ref.shape[1]
    h = _rms(x_ref[0], g_ref[...]).astype(BF16)
    offs = [0]
    for width in IN_GROUPS:
        offs.append(offs[-1] + width)
    split = offs[2]
    wide = (jnp.dot(h, w_ref[:, :split], preferred_element_type=F32),
            jnp.dot(h, w_ref[:, split:], preferred_element_type=F32))
    proj = lambda gi: (wide[0][:, offs[gi]:offs[gi + 1]] if gi < 2 else
                       wide[1][:, offs[gi] - split:offs[gi + 1] - split])
    qkv_ref[0] = proj(0)
    z_ref[0] = proj(1)
    ba_ref[0] = proj(2)
    gate_ref[0] = proj(5)
    u_ref[0] = proj(6)
    kv = proj(4)
    kvc_ref[0] = kv[:, :2 * LANES]
    _nsa_layouts(proj(3), kv, pl.program_id(1) * tm, segq_ref[...], segk_ref[...], qn_ref[...], kn_ref[...],
                 qT_ref, ks_ref, kw_ref, vs_ref, vw_ref)


def _inproj(x, g, w_pad, q_norm, k_norm, tm=1024):
    B, T, _ = x.shape
    chq = jnp.arange(NSA_WIDTH) // HEAD_DIM
    segq = (chq[:, None] == chq[None, :]).astype(BF16)
    segk = segq[:LANES, :LANES]
    qn = jnp.tile(q_norm, NSA_HEADS).reshape(1, NSA_WIDTH)
    kn = jnp.tile(k_norm[1:3], (1, NSA_KV_HEADS))
    tok = lambda c: pl.BlockSpec((1, tm, c), lambda b, i: (b, i, 0))
    f32 = lambda c: jax.ShapeDtypeStruct((B, T, c), F32)
    vspec = pl.BlockSpec((1, NSA_KV_HEADS, V_ROWS, tm), lambda b, i: (b, 0, 0, i))
    vshape = jax.ShapeDtypeStruct((B, NSA_KV_HEADS, V_ROWS, T), BF16)
    return pl.pallas_call(
        _inproj_body,
        grid=(B, T // tm),
        in_specs=[tok(D_MODEL), _const_spec((1, D_MODEL)), _resident_spec(w_pad.shape),
                  _const_spec(segq.shape), _const_spec(segk.shape), _const_spec(qn.shape), _const_spec(kn.shape)],
        out_specs=[tok(3 * GDN_WIDTH), tok(GDN_WIDTH), tok(LANES), tok(2 * LANES), tok(LANES), tok(POOL_WIDTH),
                   pl.BlockSpec((1, NSA_WIDTH, tm), lambda b, i: (b, 0, i)),
                   pl.BlockSpec((1, NSA_KV_HEADS, tm, LANES), lambda b, i: (b, 0, i, 0)),
                   tok(LANES), vspec, vspec],
        out_shape=[f32(3 * GDN_WIDTH), f32(GDN_WIDTH), f32(LANES), f32(2 * LANES), f32(LANES), f32(POOL_WIDTH),
                   jax.ShapeDtypeStruct((B, NSA_WIDTH, T), BF16),
                   jax.ShapeDtypeStruct((B, NSA_KV_HEADS, T, LANES), BF16),
                   jax.ShapeDtypeStruct((B, T, LANES), BF16),
                   vshape, vshape],
        compiler_params=_params(("parallel", "parallel")),
        name="inproj",
    )(x, g.reshape(1, D_MODEL), w_pad, segq, segk, qn, kn)


def _gdn_inputs(x, tail, ba, cw, alog, dtb, seg, eb, eg):
    xe = jnp.concatenate([tail, x], axis=0)
    y = x * cw[3:4, :]
    for s in (1, 2, 3):
        y = y + pltpu.roll(xe, s, 0)[SUBLANES:] * cw[3 - s:4 - s, :]
    a = _silu(y)
    q = a[:, :GDN_WIDTH]
    k = a[:, GDN_WIDTH:2 * GDN_WIDTH]
    qn = q * lax.rsqrt(_dot_x_exact(q * q, seg) + EPS) * (HEAD_DIM ** -0.5)
    kn = k * lax.rsqrt(_dot_x_exact(k * k, seg) + EPS)
    gval = -jnp.exp(alog) * _softplus(ba + dtb)
    return qn, kn, a[:, 2 * GDN_WIDTH:], _dot_x_exact(_sigmoid(ba), eb), _dot_x_exact(gval, eg)


def _gdn_body(qkv_ref, ba_ref, z_ref, cw_ref, alog_ref, dtb_ref, seg_ref, eb_ref, eg_ref, tri_ref, gn_ref,
              o_ref, s_ref, tail_ref, *, cps):
    C = GDN_CHUNK
    P = 2 * C
    B = qkv_ref.shape[0]
    NP = GDN_WIDTH // LANES
    tm = qkv_ref.shape[1]

    @pl.when(pl.program_id(0) == 0)
    def _():
        s_ref[...] = jnp.zeros_like(s_ref)
        tail_ref[...] = jnp.zeros_like(tail_ref)

    feats = []
    for b in range(B):
        x = qkv_ref[b]
        feats.append(_gdn_inputs(x, tail_ref[b], ba_ref[b], cw_ref[...], alog_ref[...], dtb_ref[...],
                                 seg_ref[...], eb_ref[...], eg_ref[...]))
        tail_ref[b] = x[tm - SUBLANES:, :]

    lo_lane = lax.broadcasted_iota(jnp.int32, (C, LANES), 1) < HEAD_DIM
    r = lax.broadcasted_iota(jnp.int32, (P, P), 0)
    c = lax.broadcasted_iota(jnp.int32, (P, P), 1)
    same = (r < C) == (c < C)
    causal = same & (r >= c)
    strict = same & (r > c)
    eye = (r == c).astype(F32)
    tri = tri_ref[...]
    gn = gn_ref[...]

    def blk(full, j, p):
        x = full[C * j:C * (j + 1), LANES * p:LANES * (p + 1)]
        return jnp.concatenate([jnp.where(lo_lane, x, 0.0), jnp.where(lo_lane, 0.0, x)], axis=0)

    chains = [(b, p) for b in range(B) for p in range(NP)]
    st = {}
    _gdn_local_stages([(b, p, j) for j in range(cps) for (b, p) in chains], st, feats, blk, tri, causal, strict,
                      same, eye, r, c)

    S = {ch: s_ref[ch[0], ch[1]] for ch in chains}
    for j in range(cps):
        for ch in chains:
            b, p = ch
            d = st[(b, p, j)]
            ws = _mm(jnp.concatenate([d["w"], d["qd"]], axis=0), S[ch])
            vnew = d["u"] - ws[:P]
            ov = _mm(jnp.concatenate([d["qk"], d["kdT"]], axis=0), vnew)
            o2 = ws[P:] + ov[:P]
            S[ch] = S[ch] * d["eg"] + ov[P:]
            ms = jnp.sum(o2 * o2, axis=-1, keepdims=True) * (1.0 / HEAD_DIM)
            on = o2 * lax.rsqrt(ms + EPS) * gn
            z = z_ref[b, C * j:C * (j + 1), LANES * p:LANES * (p + 1)]
            o_ref[b, C * j:C * (j + 1), LANES * p:LANES * (p + 1)] = (on[:C] + on[C:]) * _silu(z)
    for ch in chains:
        s_ref[ch[0], ch[1]] = S[ch]


def _gdn_local_stages(units, st, feats, blk, tri, causal, strict, same, eye, r, c):
    C = GDN_CHUNK
    P = 2 * C
    for u in units:
        b, p, j = u
        q2, k2, v2, b2, g2 = [blk(f, j, p) for f in feats[b]]
        g_hi, g_lo = _split_bf16(g2)
        gcc = jnp.dot(tri, jnp.concatenate([g_hi, g_lo], axis=1), preferred_element_type=F32)
        gc = gcc[:, :LANES] + gcc[:, LANES:]
        decay = jnp.exp(jnp.where(causal, gc - gc.T, NEG))
        kb = k2 * b2
        gram = _mm_nt(jnp.concatenate([kb, q2], axis=0), k2)
        egc = jnp.exp(gc)
        glast = jnp.concatenate([jnp.broadcast_to(gc[C - 1:C, :], (C, LANES)),
                                 jnp.broadcast_to(gc[P - 1:P, :], (C, LANES))], axis=0)
        st[u] = dict(L=jnp.where(strict, gram[:P] * decay, 0.0), qk=gram[P:] * decay,
                     rhs=jnp.concatenate([v2 * b2, kb * egc], axis=1), qd=q2 * egc,
                     kdT=(k2 * jnp.exp(glast - gc)).T, eg=jnp.exp(glast))

    def couple(s):
        return same & ((r // (2 * s)) == (c // (2 * s))) & (((r // s) % 2) == 1) & (((c // s) % 2) == 0)

    for u in units:
        st[u]["X"] = eye - jnp.where(couple(1), st[u]["L"], 0.0)
    s = 2
    while s < C:
        mask = couple(s)
        for u in units:
            X = st[u]["X"]
            st[u]["X"] = X - _mm(X, _mm(jnp.where(mask, st[u]["L"], 0.0), X))
        s *= 2
    for u in units:
        st[u].pop("L")
        uw = _mm(st[u].pop("X"), st[u].pop("rhs"))
        st[u]["u"], st[u]["w"] = uw[:, :LANES], uw[:, LANES:]


def _gdn(qkv, ba, z, conv_w, a_log, dt_bias, gdn_norm, cps=4):
    B, T, width = qkv.shape
    C = GDN_CHUNK
    H = GDN_HEADS
    tm = C * cps
    alog_p = jnp.zeros((1, LANES), F32).at[0, H:2 * H].set(a_log)
    dtb_p = jnp.zeros((1, LANES), F32).at[0, H:2 * H].set(dt_bias)
    ch = jnp.arange(GDN_WIDTH) // HEAD_DIM
    seg = (ch[:, None] == ch[None, :]).astype(BF16)
    row = jnp.arange(LANES)
    eb = (row[:, None] == ch[None, :]).astype(BF16)
    eg = (row[:, None] - H == ch[None, :]).astype(BF16)
    i2 = jnp.arange(2 * C)
    tri = (((i2[:, None] // C) == (i2[None, :] // C)) & (i2[:, None] >= i2[None, :])).astype(BF16)
    gn2 = jnp.tile(gdn_norm, 2).reshape(1, LANES)
    tok = lambda c: pl.BlockSpec((B, tm, c), lambda i: (0, i, 0))
    consts = (conv_w, alog_p, dtb_p, seg, eb, eg, tri, gn2)
    return pl.pallas_call(
        functools.partial(_gdn_body, cps=cps),
        grid=(T // tm,),
        in_specs=[tok(width), tok(LANES), tok(GDN_WIDTH)] + [_const_spec(c.shape) for c in consts],
        out_specs=tok(GDN_WIDTH),
        out_shape=jax.ShapeDtypeStruct((B, T, GDN_WIDTH), F32),
        scratch_shapes=[pltpu.VMEM((B, GDN_WIDTH // LANES, 2 * C, LANES), F32),
                        pltpu.VMEM((B, SUBLANES, width), F32)],
        compiler_params=_params(("arbitrary",)),
        name="gdn",
    )(qkv, ba, z, *consts)


def _nsa_layouts(q, kv, first_pos, segq, segk, qn, kn, qT_ref, ks_ref, kw_ref, vs_ref, vw_ref):
    tm = q.shape[0]
    ms = _dot_x_exact(q * q, segq) * (1.0 / HEAD_DIM)
    qT_ref[0] = (q * lax.rsqrt(ms + EPS) * qn * (HEAD_DIM ** -0.5 * LOG2E)).T.astype(BF16)

    def knorm(k, g):
        msk = _dot_x_exact(k * k, segk) * (1.0 / HEAD_DIM)
        return k * lax.rsqrt(msk + EPS) * g

    ks = knorm(kv[:, 2 * LANES:3 * LANES], kn[0:1, :])
    lane = lax.broadcasted_iota(jnp.int32, (tm, LANES), 1)
    tok = first_pos + lax.broadcasted_iota(jnp.int32, (tm, LANES), 0)
    onehot = (lane - HEAD_DIM == (tok // SEL_BLOCK) % (KEY_TILE // SEL_BLOCK)).astype(F32)
    ks_ref[0, 0] = jnp.where(lane < HEAD_DIM, ks, onehot).astype(BF16)
    ks_ref[0, 1] = jnp.where(lane < HEAD_DIM, pltpu.roll(ks, HEAD_DIM, 1), onehot).astype(BF16)
    kw_ref[0] = knorm(kv[:, 4 * LANES:5 * LANES], kn[1:2, :]).astype(BF16)
    ones_rows = (lax.broadcasted_iota(jnp.int32, (V_ROWS - HEAD_DIM, tm), 0) == 0).astype(BF16)
    for v_ref, cols in ((vs_ref, 3), (vw_ref, 5)):
        vT = kv[:, cols * LANES:(cols + 1) * LANES].T.astype(BF16)
        for h in range(NSA_KV_HEADS):
            v_ref[0, h, :HEAD_DIM, :] = vT[h * HEAD_DIM:(h + 1) * HEAD_DIM, :]
            v_ref[0, h, HEAD_DIM:, :] = ones_rows


def _nsa_compress_body(x_ref, wd_ref, pos_ref, w2k_ref, w2vT_ref, kn_ref, kc_ref, vcT_ref):
    branch = pl.program_id(1)
    nr = x_ref.shape[1] // CMP_STRIDE
    hid2 = NSA_KV_HEADS * CMP_HIDDEN
    ya = jnp.zeros((nr, hid2), F32)
    yb = jnp.zeros((nr, hid2), F32)
    for l in range(CMP_STRIDE):
        xl = x_ref[0, pl.ds(l, nr, stride=CMP_STRIDE), :].astype(BF16)
        ya = ya + jnp.dot(xl, wd_ref[0, l * LANES:(l + 1) * LANES, :], preferred_element_type=F32)
        yb = yb + jnp.dot(xl, wd_ref[0, (CMP_STRIDE + l) * LANES:(CMP_STRIDE + l + 1) * LANES, :],
                          preferred_element_type=F32)
    bias = jnp.dot(pos_ref[0], wd_ref[0], preferred_element_type=F32)[0:1, :]
    hidden = _silu(ya + pltpu.roll(yb, nr - 1, 0) + bias).astype(BF16)

    @pl.when(branch == 0)
    def _():
        for h in range(NSA_KV_HEADS):
            kc = jnp.dot(hidden[:, h * CMP_HIDDEN:(h + 1) * CMP_HIDDEN], w2k_ref[...], preferred_element_type=F32)
            kc_ref[0, h] = _rms(kc, kn_ref[...]).astype(BF16)

    @pl.when(branch == 1)
    def _():
        for h in range(NSA_KV_HEADS):
            vcT_ref[0, h] = lax.dot_general(w2vT_ref[...], hidden[:, h * CMP_HIDDEN:(h + 1) * CMP_HIDDEN],
                                            (((1,), (1,)), ((), ())), preferred_element_type=F32).astype(BF16)


def _nsa_compress(kvb, cmp_pos, cmp_w1, cmp_w2, k_norm0):
    B, T, _ = kvb.shape
    nr = T // CMP_STRIDE
    w1 = cmp_w1.reshape(2, CMP_LEN, HEAD_DIM, CMP_HIDDEN)
    z = jnp.zeros_like(w1)
    wd = jnp.concatenate([jnp.concatenate([w1, z], axis=3), jnp.concatenate([z, w1], axis=3)], axis=2)
    wd = wd.reshape(2, CMP_LEN * LANES, NSA_KV_HEADS * CMP_HIDDEN).astype(BF16)
    pos = jnp.tile(cmp_pos, (1, 1, NSA_KV_HEADS)).reshape(2, 1, CMP_LEN * LANES)
    pos = jnp.broadcast_to(pos, (2, SUBLANES, CMP_LEN * LANES)).astype(BF16)
    return pl.pallas_call(
        _nsa_compress_body,
        grid=(B, 2),
        in_specs=[pl.BlockSpec((1, T, LANES), lambda b, r: (b, 0, r)),
                  pl.BlockSpec((1,) + wd.shape[1:], lambda b, r: (r, 0, 0)),
                  pl.BlockSpec((1,) + pos.shape[1:], lambda b, r: (r, 0, 0)),
                  _const_spec((CMP_HIDDEN, HEAD_DIM)), _const_spec((HEAD_DIM, CMP_HIDDEN)),
                  _const_spec((1, HEAD_DIM))],
        out_specs=[pl.BlockSpec((1, NSA_KV_HEADS, nr, HEAD_DIM), lambda b, r: (b, 0, 0, 0)),
                   pl.BlockSpec((1, NSA_KV_HEADS, HEAD_DIM, nr), lambda b, r: (b, 0, 0, 0))],
        out_shape=[jax.ShapeDtypeStruct((B, NSA_KV_HEADS, nr, HEAD_DIM), BF16),
                   jax.ShapeDtypeStruct((B, NSA_KV_HEADS, HEAD_DIM, nr), BF16)],
        compiler_params=_params(("parallel", "arbitrary")),
        name="nsa_compress",
    )(kvb, wd, pos, cmp_w2[0].astype(BF16), cmp_w2[1].T.astype(BF16), k_norm0.reshape(1, HEAD_DIM))


def _nsa_attn_body(qT_ref, kc_ref, vcT_ref, ks_ref, vs_ref, kw_ref, vw_ref, gate_ref, selm_ref,
                   o_ref, selb_ref, gT_ref, qaug_ref, m_ref, acc_ref, sa_ref, sb_ref, oc_ref, ow_ref, *, seq):
    G, Dh, Qb = NSA_GROUP, HEAD_DIM, Q_BLOCK
    h = pl.program_id(1)
    t0 = pl.program_id(2) * Qb
    nr = seq // CMP_STRIDE
    ns = seq // SEL_BLOCK
    pos = t0 + lax.broadcasted_iota(jnp.int32, (1, Qb), 1)

    qT = qT_ref[0]
    q3 = jnp.concatenate([qT[g * Dh:(g + 1) * Dh, :] for g in range(G)], axis=1)
    qaug_ref[...] = jnp.concatenate([q3, jnp.zeros_like(q3)], axis=0)
    row = lax.broadcasted_iota(jnp.int32, (2 * Dh, G * Qb), 0)
    qpad = jnp.where((row < Dh) == (h == 0), jnp.concatenate([q3, q3], axis=0), 0).astype(BF16)

    def window_branch():
        ws = pl.multiple_of(jnp.maximum(t0 - WINDOW, 0), Qb)
        sw = jnp.dot(kw_ref[0, pl.ds(ws, WIN_KEYS), :], qpad, preferred_element_type=F32)
        vwx = vw_ref[0, 0, :, pl.ds(ws, WIN_KEYS)]
        kpos = ws + lax.broadcasted_iota(jnp.int32, (WIN_KEYS, 1), 0)
        bias_w = jnp.where((kpos <= pos) & (kpos > pos - WINDOW), 0.0, NEG)
        for g in range(G):
            s = sw[:, g * Qb:(g + 1) * Qb] + bias_w
            e = jnp.exp2(s - jnp.max(s, axis=0, keepdims=True)).astype(BF16)
            ow_ref[g] = jnp.dot(vwx, e, preferred_element_type=F32)

    def compressed_and_select(n_c):
        n_s = n_c * CMP_STRIDE // SEL_BLOCK
        sc = jnp.dot(kc_ref[0, 0, :n_c, :], q3, preferred_element_type=F32)
        cmp_end = lax.broadcasted_iota(jnp.int32, (n_c, 1), 0) * CMP_STRIDE + (CMP_LEN - 1)
        bias_c = jnp.where(cmp_end <= pos, 0.0, NEG)
        has_c = pos >= CMP_LEN - 1
        lhs = jnp.concatenate([vcT_ref[0, 0, :, :n_c], selm_ref[:n_s, :n_c]], axis=0)
        imp = jnp.zeros((n_s, Qb), F32)
        for g in range(G):
            s = sc[:, g * Qb:(g + 1) * Qb] + bias_c
            e = jnp.exp2(s - jnp.max(s, axis=0, keepdims=True))
            p = (e * jnp.where(has_c, 1.0 / jnp.sum(e, axis=0, keepdims=True), 0.0)).astype(BF16)
            both = jnp.dot(lhs, p, preferred_element_type=F32)
            oc_ref[g] = both[:Dh]
            imp = imp + both[Dh:]

        blk = lax.broadcasted_iota(jnp.int32, (n_s, Qb), 0)
        cur = pos // SEL_BLOCK
        forced = (blk == 0) | (blk == cur) | (blk == cur - 1)
        score = jnp.where(forced | (blk > cur), -jnp.inf, imp)
        for _ in range(SEL_TOPK - 3):
            best = jnp.max(score, axis=0, keepdims=True)
            first = jnp.min(jnp.where(score == best, blk, n_s), axis=0, keepdims=True)
            score = jnp.where(blk == first, -jnp.inf, score)
        selb_ref[:n_s, :] = jnp.where((score == -jnp.inf) & (blk <= cur), 0.0, NEG)

    chunk = min(CMP_CHUNK, nr)
    need = (t0 + Qb) // CMP_STRIDE
    for k in range(1, nr // chunk + 1):
        pl.when((need + chunk - 1) // chunk == k)(functools.partial(compressed_and_select, k * chunk))

    bpt = KEY_TILE // SEL_BLOCK
    krow = lax.broadcasted_iota(jnp.int32, (KEY_TILE, 1), 0)
    m_ref[...] = jnp.full(m_ref.shape, NEG, F32)
    acc_ref[...] = jnp.zeros(acc_ref.shape, F32)

    def scores(kt, slot_ref, nk=KEY_TILE):
        base = pl.multiple_of(kt * KEY_TILE, KEY_TILE)
        sb = selb_ref[pl.ds(pl.multiple_of(kt * bpt, bpt), bpt), :]
        qaug_ref[Dh:Dh + bpt, :] = jnp.concatenate([sb] * G, axis=1).astype(BF16)
        slot_ref[:nk, :] = jnp.dot(ks_ref[0, 0, pl.ds(base, nk), :], qaug_ref[...],
                                   preferred_element_type=F32)

    def consume(kt, slot_ref, diagonal, nk=KEY_TILE):
        base = pl.multiple_of(kt * KEY_TILE, KEY_TILE)
        vx = vs_ref[0, 0, :, pl.ds(base, nk)]
        for g in range(G):
            s = slot_ref[:nk, g * Qb:(g + 1) * Qb]
            if diagonal:
                s = jnp.where(base + krow[:nk] <= pos, s, NEG)
            m_old = m_ref[g:g + 1, :]
            m_new = jnp.maximum(m_old, jnp.max(s, axis=0, keepdims=True))
            e = jnp.exp2(s - m_new).astype(BF16)
            acc_ref[g] = jnp.exp2(m_old - m_new) * acc_ref[g] + jnp.dot(vx, e, preferred_element_type=F32)
            m_ref[g:g + 1, :] = m_new

    n_off = (t0 + Qb + KEY_TILE - 1) // KEY_TILE - 1
    scores(0, sa_ref)
    window_branch()

    def pair(p, carry):
        scores(2 * p + 1, sb_ref)
        consume(2 * p, sa_ref, False)
        scores(2 * p + 2, sa_ref)
        consume(2 * p + 1, sb_ref, False)
        return carry

    lax.fori_loop(0, n_off // 2, pair, 0)

    def last_odd(nk):
        scores(n_off, sb_ref, nk)
        consume(n_off - 1, sa_ref, False)
        consume(n_off, sb_ref, True, nk)

    def last_even(nk):
        consume(n_off, sa_ref, True, nk)

    q_in_tile = (t0 % KEY_TILE) // Qb
    for r in range(KEY_TILE // Qb):
        pl.when((n_off % 2 == 1) & (q_in_tile == r))(functools.partial(last_odd, (r + 1) * Qb))
        pl.when((n_off % 2 == 0) & (q_in_tile == r))(functools.partial(last_even, (r + 1) * Qb))

    gT_ref[...] = _sigmoid(gate_ref[0]).T
    gT = gT_ref[0:3 * SUBLANES, :]
    for g in range(G):
        r0 = g * 3
        gates = jnp.where(h == 0, gT[r0:r0 + 3], gT[G * 3 + r0:G * 3 + r0 + 3])
        a_s, a_w = acc_ref[g], ow_ref[g]
        o_s = a_s[:Dh] * (1.0 / a_s[Dh:Dh + 1])
        o_w = a_w[:Dh] * (1.0 / a_w[Dh:Dh + 1])
        o_ref[0, g * Dh:(g + 1) * Dh, :] = gates[0:1] * oc_ref[g] + gates[1:2] * o_s + gates[2:3] * o_w


def _nsa_attn(qT, kc, vcT, ks, vs, kw, vw, gate):
    B, _, T = qT.shape
    assert T % KEY_TILE == 0 and T >= WIN_KEYS
    nr, ns = T // CMP_STRIDE, T // SEL_BLOCK
    G, Dh, Qb = NSA_GROUP, HEAD_DIM, Q_BLOCK
    j = jnp.arange(ns)[:, None]
    c = jnp.arange(nr)[None, :]
    ratio = SEL_BLOCK // CMP_STRIDE
    selm = ((c >= ratio * j - 1) & (c <= ratio * j + ratio - 1) & (c < nr - 1)).astype(BF16)
    vfull = pl.BlockSpec((1, 1, V_ROWS, T), lambda b, h, i: (b, h, 0, 0))
    return pl.pallas_call(
        functools.partial(_nsa_attn_body, seq=T),
        grid=(B, NSA_KV_HEADS, T // Qb),
        in_specs=[pl.BlockSpec((1, G * Dh, Qb), lambda b, h, i: (b, h, i)),
                  pl.BlockSpec((1, 1, nr, Dh), lambda b, h, i: (b, h, 0, 0)),
                  pl.BlockSpec((1, 1, Dh, nr), lambda b, h, i: (b, h, 0, 0)),
                  pl.BlockSpec((1, 1, T, LANES), lambda b, h, i: (b, h, 0, 0)), vfull,
                  pl.BlockSpec((1, T, LANES), lambda b, h, i: (b, 0, 0)), vfull,
                  pl.BlockSpec((1, Qb, LANES), lambda b, h, i: (b, i, 0)),
                  _const_spec(selm.shape)],
        out_specs=pl.BlockSpec((1, G * Dh, Qb), lambda b, h, i: (b, h, i)),
        out_shape=jax.ShapeDtypeStruct((B, NSA_WIDTH, T), F32),
        scratch_shapes=[pltpu.VMEM((ns, Qb), F32), pltpu.VMEM((LANES, Qb), F32),
                        pltpu.VMEM((2 * Dh, G * Qb), BF16), pltpu.VMEM((SUBLANES, Qb), F32),
                        pltpu.VMEM((G, V_ROWS, Qb), F32),
                        pltpu.VMEM((KEY_TILE, G * Qb), F32), pltpu.VMEM((KEY_TILE, G * Qb), F32),
                        pltpu.VMEM((G, Dh, Qb), F32), pltpu.VMEM((G, V_ROWS, Qb), F32)],
        compiler_params=_params(("parallel", "parallel", "arbitrary")),
        name="nsa_attn",
    )(qT, kc, vcT, ks, vs, kw, vw, gate, selm)


POOL_HALO = 16
FF_TILE = 1024


def _pool_tile(u, halo, first_pos, w, scale):
    tm = u.shape[0]
    ue = jnp.concatenate([halo, u], axis=0)
    grp = lax.broadcasted_iota(jnp.int32, (1, POOL_WIDTH), 1) // POOL_GROUP_DIM
    acc = ue
    wsum = jnp.zeros_like(ue)
    width = jnp.zeros((1, POOL_WIDTH), F32)
    for gi, win in enumerate(POOL_WINDOWS):
        acc = acc + pltpu.roll(acc, win // 2, 0)
        wsum = jnp.where(grp == gi, acc, wsum)
        width = jnp.where(grp == gi, float(win), width)
    t1 = (first_pos + 1 + lax.broadcasted_iota(jnp.int32, (tm, 1), 0)).astype(F32)
    mean = wsum[POOL_HALO:] / jnp.minimum(t1, width)
    return jnp.dot((mean - u).astype(BF16), w, preferred_element_type=F32) * scale


def _post_body(x_ref, ya_ref, ybT_ref, u_ref, halo_ref, wo_ref, pw_ref, psc_ref, g_ref, w1_ref, w2_ref, o_ref):
    i = pl.program_id(1)
    tm = x_ref.shape[1]
    halo = jnp.where(i > 0, halo_ref[0], 0.0)
    yc = _pool_tile(u_ref[0], halo, i * tm, pw_ref[...], psc_ref[...])
    y = jnp.concatenate([ya_ref[0].astype(BF16), ybT_ref[0].T.astype(BF16), yc.astype(BF16)], axis=1)
    x = x_ref[0] + jnp.dot(y, wo_ref[...], preferred_element_type=F32)
    o_ref[0] = x
    x = o_ref[0]
    h = _rms(x, g_ref[...]).astype(BF16)
    acc = x
    for j in range(D_FF // FF_TILE):
        cols = slice(j * FF_TILE, (j + 1) * FF_TILE)
        a = jnp.maximum(jnp.dot(h, w1_ref[:, cols], preferred_element_type=F32), 0.0)
        acc = acc + jnp.dot((a * a).astype(BF16), w2_ref[cols, :], preferred_element_type=F32)
    o_ref[0] = acc


def _post(x, ya, ybT, u, pool_w, pool_scale, w_out, g, w1, w2, tm=1024):
    B, T, _ = x.shape
    wblk =jax.scipy.linalg.block_diag(*[pool_w[gi] for gi in range(len(POOL_WINDOWS))]).astype(BF16)
    tok = lambda c: pl.BlockSpec((1, tm, c), lambda b, i: (b, i, 0))
    hb = tm // POOL_HALO
    return pl.pallas_call(
        _post_body,
        grid=(B, T // tm),
        in_specs=[tok(D_MODEL), tok(GDN_WIDTH), pl.BlockSpec((1, NSA_WIDTH, tm), lambda b, i: (b, 0, i)),
                  tok(POOL_WIDTH),
                  pl.BlockSpec((1, POOL_HALO, POOL_WIDTH), lambda b, i: (b, jnp.maximum(i * hb - 1, 0), 0)),
                  _const_spec(w_out.shape), _const_spec(wblk.shape), _const_spec((1, POOL_WIDTH)), _const_spec((1, D_MODEL)),
                  _resident_spec(w1.shape), _resident_spec(w2.shape)],
        out_specs=tok(D_MODEL),
        out_shape=jax.ShapeDtypeStruct(x.shape, F32),
        compiler_params=_params(("parallel", "parallel")),
        name="post",
    )(x, ya, ybT, u, u, w_out.astype(BF16), wblk, pool_scale.reshape(1, POOL_WIDTH), g.reshape(1, D_MODEL),
      w1.astype(BF16), w2.astype(BF16))


def _pad_in_weights(w):
    H = GDN_HEADS
    o_ba = 4 * GDN_WIDTH
    o_q = o_ba + 2 * H
    o_gate = o_q + NSA_WIDTH + 6 * LANES
    o_u = o_gate + 3 * NSA_HEADS
    w = w.astype(BF16)
    pad = lambda a: jnp.pad(a, [(0, 0)] * (a.ndim - 1) + [(0, LANES - a.shape[-1])])
    return jnp.concatenate([w[..., :o_ba], pad(w[..., o_ba:o_q]), w[..., o_q:o_gate], pad(w[..., o_gate:o_u]),
                            w[..., o_u:]], axis=-1)


def kernel(x, norm_mix, w_in, conv_w, a_log, dt_bias, gdn_norm, nsa_q_norm, nsa_k_norm, cmp_pos, cmp_w1, cmp_w2,
           pool_w, pool_scale, w_out, norm_ffn, w_ffn1, w_ffn2):
    w_in_pad = _pad_in_weights(w_in)
    for l in range(w_in.shape[0]):
        qkv, z, ba, kvc, gate, u, qT, ks, kw, vs, vw = _inproj(x, norm_mix[l], w_in_pad[l], nsa_q_norm[l],
                                                               nsa_k_norm[l])
        ya = _gdn(qkv, ba, z, conv_w[l], a_log[l], dt_bias[l], gdn_norm[l])
        kc, vcT = _nsa_compress(kvc, cmp_pos[l], cmp_w1[l], cmp_w2[l], nsa_k_norm[l][0])
        ybT = _nsa_attn(qT, kc, vcT, ks, vs, kw, vw, gate)
        x = _post(x, ya, ybT, u, pool_w[l], pool_scale[l], w_out[l], norm_ffn[l], w_ffn1[l], w_ffn2[l])
    return x
```

```python
import functools

import jax
import jax.numpy as jnp
from jax import lax
from jax.experimental import pallas as pl
from jax.experimental.pallas import tpu as pltpu

F32 = jnp.float32
BF16 = jnp.bfloat16

D_MODEL = 1024
HEAD_DIM = 64
GDN_HEADS = 6
GDN_CHUNK = 64
GDN_WIDTH = GDN_HEADS * HEAD_DIM
NSA_HEADS = 6
NSA_KV_HEADS = 2
NSA_GROUP = NSA_HEADS // NSA_KV_HEADS
NSA_WIDTH = NSA_HEADS * HEAD_DIM
CMP_LEN = 32
CMP_STRIDE = 16
CMP_HIDDEN = 256
SEL_BLOCK = 64
SEL_TOPK = 16
WINDOW = 512
Q_BLOCK = 256
POOL_WINDOWS = (2, 4, 8, 16)
POOL_GROUP_DIM = 64
POOL_WIDTH = 4 * POOL_GROUP_DIM
D_FF = 4 * D_MODEL
EPS = 1e-6

LANES = 128
SUBLANES = 8
VMEM_LIMIT = 56 * 1024 * 1024
NEG = -1e30
LOG2E = 1.4426950408889634
KEY_TILE = 1024
WIN_KEYS = WINDOW + Q_BLOCK
CMP_CHUNK = 128
V_ROWS = 80

IN_GROUPS = (3 * GDN_WIDTH, GDN_WIDTH, LANES, NSA_WIDTH, 6 * LANES, LANES, POOL_WIDTH)


def _sigmoid(x):
    return 1.0 / (1.0 + jnp.exp(-x))


def _silu(x):
    return x * _sigmoid(x)


def _softplus(x):
    return jnp.maximum(x, 0.0) + jnp.log(1.0 + jnp.exp(-jnp.abs(x)))


def _split_bf16(x):
    hi = x.astype(BF16)
    lo = (x - hi.astype(F32)).astype(BF16)
    return hi, lo


def _dot_x_exact(x, m):
    hi, lo = _split_bf16(x)
    return jnp.dot(hi, m, preferred_element_type=F32) + jnp.dot(lo, m, preferred_element_type=F32)


def _mm(a, b):
    return jnp.dot(a.astype(BF16), b.astype(BF16), preferred_element_type=F32)


def _mm_nt(a, b):
    return lax.dot_general(a.astype(BF16), b.astype(BF16), (((1,), (1,)), ((), ())),
                           preferred_element_type=F32)


def _rms(x, g):
    return x * lax.rsqrt(jnp.mean(x * x, axis=-1, keepdims=True) + EPS) * g


def _const_spec(shape):
    nd = len(shape)
    return pl.BlockSpec(shape, lambda *_: (0,) * nd)


def _resident_spec(shape):
    nd = len(shape)
    return pl.BlockSpec(shape, lambda *_: (0,) * nd, pipeline_mode=pl.Buffered(1))


def _params(sem):
    return pltpu.CompilerParams(dimension_semantics=sem, vmem_limit_bytes=VMEM_LIMIT)


def _inproj_body(x_ref, g_ref, w_ref, segq_ref, segk_ref, qn_ref, kn_ref,
                 qkv_ref, z_ref, ba_ref, kvc_ref, gate_ref, u_ref, qT_ref, ks_ref, kw_ref, vs_ref, vw_ref):
    tm = x_ref.shape[1]
    h = _rms(x_ref[0], g_ref[...]).astype(BF16)
    offs = [0]
    for width in IN_GROUPS:
        offs.append(offs[-1] + width)
    split = offs[2]
    wide = (jnp.dot(h, w_ref[:, :split], preferred_element_type=F32),
            jnp.dot(h, w_ref[:, split:], preferred_element_type=F32))
    proj = lambda gi: (wide[0][:, offs[gi]:offs[gi + 1]] if gi < 2 else
                       wide[1][:, offs[gi] - split:offs[gi + 1] - split])
    qkv_ref[0] = proj(0)
    z_ref[0] = proj(1)
    ba_ref[0] = proj(2)
    gate_ref[0] = proj(5)
    u_ref[0] = proj(6)
    kv = proj(4)
    kvc_ref[0] = kv[:, :2 * LANES]
    _nsa_layouts(proj(3), kv, pl.program_id(1) * tm, segq_ref[...], segk_ref[...], qn_ref[...], kn_ref[...],
                 qT_ref, ks_ref, kw_ref, vs_ref, vw_ref)


def _inproj(x, g, w_pad, q_norm, k_norm, tm=1024):
    B, T, _ = x.shape
    chq = jnp.arange(NSA_WIDTH) // HEAD_DIM
    segq = (chq[:, None] == chq[None, :]).astype(BF16)
    segk = segq[:LANES, :LANES]
    qn = jnp.tile(q_norm, NSA_HEADS).reshape(1, NSA_WIDTH)
    kn = jnp.tile(k_norm[1:3], (1, NSA_KV_HEADS))
    tok = lambda c: pl.BlockSpec((1, tm, c), lambda b, i: (b, i, 0))
    f32 = lambda c: jax.ShapeDtypeStruct((B, T, c), F32)
    vspec = pl.BlockSpec((1, NSA_KV_HEADS, V_ROWS, tm), lambda b, i: (b, 0, 0, i))
    vshape = jax.ShapeDtypeStruct((B, NSA_KV_HEADS, V_ROWS, T), BF16)
    return pl.pallas_call(
        _inproj_body,
        grid=(B, T // tm),
        in_specs=[tok(D_MODEL), _const_spec((1, D_MODEL)), _resident_spec(w_pad.shape),
                  _const_spec(segq.shape), _const_spec(segk.shape), _const_spec(qn.shape), _const_spec(kn.shape)],
        out_specs=[tok(3 * GDN_WIDTH), tok(GDN_WIDTH), tok(LANES), tok(2 * LANES), tok(LANES), tok(POOL_WIDTH),
                   pl.BlockSpec((1, NSA_WIDTH, tm), lambda b, i: (b, 0, i)),
                   pl.BlockSpec((1, NSA_KV_HEADS, tm, LANES), lambda b, i: (b, 0, i, 0)),
                   tok(LANES), vspec, vspec],
        out_shape=[f32(3 * GDN_WIDTH), f32(GDN_WIDTH), f32(LANES), f32(2 * LANES), f32(LANES), f32(POOL_WIDTH),
                   jax.ShapeDtypeStruct((B, NSA_WIDTH, T), BF16),
                   jax.ShapeDtypeStruct((B, NSA_KV_HEADS, T, LANES), BF16),
                   jax.ShapeDtypeStruct((B, T, LANES), BF16),
                   vshape, vshape],
        compiler_params=_params(("parallel", "parallel")),
        name="inproj",
    )(x, g.reshape(1, D_MODEL), w_pad, segq, segk, qn, kn)


def _gdn_inputs(x, tail, ba, cw, alog, dtb, seg, eb, eg):
    xe = jnp.concatenate([tail, x], axis=0)
    y = x * cw[3:4, :]
    for s in (1, 2, 3):
        y = y + pltpu.roll(xe, s, 0)[SUBLANES:] * cw[3 - s:4 - s, :]
    a = _silu(y)
    q = a[:, :GDN_WIDTH]
    k = a[:, GDN_WIDTH:2 * GDN_WIDTH]
    qn = q * lax.rsqrt(_dot_x_exact(q * q, seg) + EPS) * (HEAD_DIM ** -0.5)
    kn = k * lax.rsqrt(_dot_x_exact(k * k, seg) + EPS)
    gval = -jnp.exp(alog) * _softplus(ba + dtb)
    return qn, kn, a[:, 2 * GDN_WIDTH:], _dot_x_exact(_sigmoid(ba), eb), _dot_x_exact(gval, eg)


def _gdn_body(qkv_ref, ba_ref, z_ref, cw_ref, alog_ref, dtb_ref, seg_ref, eb_ref, eg_ref, tri_ref, gn_ref,
              o_ref, s_ref, tail_ref, *, cps):
    C = GDN_CHUNK
    P = 2 * C
    B = qkv_ref.shape[0]
    NP = GDN_WIDTH // LANES
    tm = qkv_ref.shape[1]

    @pl.when(pl.program_id(0) == 0)
    def _():
        s_ref[...] = jnp.zeros_like(s_ref)
        tail_ref[...] = jnp.zeros_like(tail_ref)

    feats = []
    for b in range(B):
        x = qkv_ref[b]
        feats.append(_gdn_inputs(x, tail_ref[b], ba_ref[b], cw_ref[...], alog_ref[...], dtb_ref[...],
                                 seg_ref[...], eb_ref[...], eg_ref[...]))
        tail_ref[b] = x[tm - SUBLANES:, :]

    lo_lane = lax.broadcasted_iota(jnp.int32, (C, LANES), 1) < HEAD_DIM
    r = lax.broadcasted_iota(jnp.int32, (P, P), 0)
    c = lax.broadcasted_iota(jnp.int32, (P, P), 1)
    same = (r < C) == (c < C)
    causal = same & (r >= c)
    strict = same & (r > c)
    eye = (r == c).astype(F32)
    tri = tri_ref[...]
    gn = gn_ref[...]

    def blk(full, j, p):
        x = full[C * j:C * (j + 1), LANES * p:LANES * (p + 1)]
        return jnp.concatenate([jnp.where(lo_lane, x, 0.0), jnp.where(lo_lane, 0.0, x)], axis=0)

    chains = [(b, p) for b in range(B) for p in range(NP)]
    st = {}
    _gdn_local_stages([(b, p, j) for j in range(cps) for (b, p) in chains], st, feats, blk, tri, causal, strict,
                      same, eye, r, c)

    S = {ch: s_ref[ch[0], ch[1]] for ch in chains}
    for j in range(cps):
        for ch in chains:
            b, p = ch
            d = st[(b, p, j)]
            ws = _mm(jnp.concatenate([d["w"], d["qd"]], axis=0), S[ch])
            vnew = d["u"] - ws[:P]
            ov = _mm(jnp.concatenate([d["qk"], d["kdT"]], axis=0), vnew)
            o2 = ws[P:] + ov[:P]
            S[ch] = S[ch] * d["eg"] + ov[P:]
            ms = jnp.sum(o2 * o2, axis=-1, keepdims=True) * (1.0 / HEAD_DIM)
            on = o2 * lax.rsqrt(ms + EPS) * gn
            z = z_ref[b, C * j:C * (j + 1), LANES * p:LANES * (p + 1)]
            o_ref[b, C * j:C * (j + 1), LANES * p:LANES * (p + 1)] = (on[:C] + on[C:]) * _silu(z)
    for ch in chains:
        s_ref[ch[0], ch[1]] = S[ch]


def _gdn_local_stages(units, st, feats, blk, tri, causal, strict, same, eye, r, c):
    C = GDN_CHUNK
    P = 2 * C
    for u in units:
        b, p, j = u
        q2, k2, v2, b2, g2 = [blk(f, j, p) for f in feats[b]]
        g_hi, g_lo = _split_bf16(g2)
        gcc = jnp.dot(tri, jnp.concatenate([g_hi, g_lo], axis=1), preferred_element_type=F32)
        gc = gcc[:, :LANES] + gcc[:, LANES:]
        decay = jnp.exp(jnp.where(causal, gc - gc.T, NEG))
        kb = k2 * b2
        gram = _mm_nt(jnp.concatenate([kb, q2], axis=0), k2)
        egc = jnp.exp(gc)
        glast = jnp.concatenate([jnp.broadcast_to(gc[C - 1:C, :], (C, LANES)),
                                 jnp.broadcast_to(gc[P - 1:P, :], (C, LANES))], axis=0)
        st[u] = dict(L=jnp.where(strict, gram[:P] * decay, 0.0), qk=gram[P:] * decay,
                     rhs=jnp.concatenate([v2 * b2, kb * egc], axis=1), qd=q2 * egc,
                     kdT=(k2 * jnp.exp(glast - gc)).T, eg=jnp.exp(glast))

    def couple(s):
        return same & ((r // (2 * s)) == (c // (2 * s))) & (((r // s) % 2) == 1) & (((c // s) % 2) == 0)

    for u in units:
        st[u]["X"] = eye - jnp.where(couple(1), st[u]["L"], 0.0)
    s = 2
    while s < C:
        mask = couple(s)
        for u in units:
            X = st[u]["X"]
            st[u]["X"] = X - _mm(X, _mm(jnp.where(mask, st[u]["L"], 0.0), X))
        s *= 2
    for u in units:
        st[u].pop("L")
        uw = _mm(st[u].pop("X"), st[u].pop("rhs"))
        st[u]["u"], st[u]["w"] = uw[:, :LANES], uw[:, LANES:]


def _gdn(qkv, ba, z, conv_w, a_log, dt_bias, gdn_norm, cps=4):
    B, T, width = qkv.shape
    C = GDN_CHUNK
    H = GDN_HEADS
    tm = C * cps
    alog_p = jnp.zeros((1, LANES), F32).at[0, H:2 * H].set(a_log)
    dtb_p = jnp.zeros((1, LANES), F32).at[0, H:2 * H].set(dt_bias)
    ch = jnp.arange(GDN_WIDTH) // HEAD_DIM
    seg = (ch[:, None] == ch[None, :]).astype(BF16)
    row = jnp.arange(LANES)
    eb = (row[:, None] == ch[None, :]).astype(BF16)
    eg = (row[:, None] - H == ch[None, :]).astype(BF16)
    i2 = jnp.arange(2 * C)
    tri = (((i2[:, None] // C) == (i2[None, :] // C)) & (i2[:, None] >= i2[None, :])).astype(BF16)
    gn2 = jnp.tile(gdn_norm, 2).reshape(1, LANES)
    tok = lambda c: pl.BlockSpec((B, tm, c), lambda i: (0, i, 0))
    consts = (conv_w, alog_p, dtb_p, seg, eb, eg, tri, gn2)
    return pl.pallas_call(
        functools.partial(_gdn_body, cps=cps),
        grid=(T // tm,),
        in_specs=[tok(width), tok(LANES), tok(GDN_WIDTH)] + [_const_spec(c.shape) for c in consts],
        out_specs=tok(GDN_WIDTH),
        out_shape=jax.ShapeDtypeStruct((B, T, GDN_WIDTH), F32),
        scratch_shapes=[pltpu.VMEM((B, GDN_WIDTH // LANES, 2 * C, LANES), F32),
                        pltpu.VMEM((B, SUBLANES, width), F32)],
        compiler_params=_params(("arbitrary",)),
        name="gdn",
    )(qkv, ba, z, *consts)


def _nsa_layouts(q, kv, first_pos, segq, segk, qn, kn, qT_ref, ks_ref, kw_ref, vs_ref, vw_ref):
    tm = q.shape[0]
    ms = _dot_x_exact(q * q, segq) * (1.0 / HEAD_DIM)
    qT_ref[0] = (q * lax.rsqrt(ms + EPS) * qn * (HEAD_DIM ** -0.5 * LOG2E)).T.astype(BF16)

    def knorm(k, g):
        msk = _dot_x_exact(k * k, segk) * (1.0 / HEAD_DIM)
        return k * lax.rsqrt(msk + EPS) * g

    ks = knorm(kv[:, 2 * LANES:3 * LANES], kn[0:1, :])
    lane = lax.broadcasted_iota(jnp.int32, (tm, LANES), 1)
    tok = first_pos + lax.broadcasted_iota(jnp.int32, (tm, LANES), 0)
    onehot = (lane - HEAD_DIM == (tok // SEL_BLOCK) % (KEY_TILE // SEL_BLOCK)).astype(F32)
    ks_ref[0, 0] = jnp.where(lane < HEAD_DIM, ks, onehot).astype(BF16)
    ks_ref[0, 1] = jnp.where(lane < HEAD_DIM, pltpu.roll(ks, HEAD_DIM, 1), onehot).astype(BF16)
    kw_ref[0] = knorm(kv[:, 4 * LANES:5 * LANES], kn[1:2, :]).astype(BF16)
    ones_rows = (lax.broadcasted_iota(jnp.int32, (V_ROWS - HEAD_DIM, tm), 0) == 0).astype(BF16)
    for v_ref, cols in ((vs_ref, 3), (vw_ref, 5)):
        vT = kv[:, cols * LANES:(cols + 1) * LANES].T.astype(BF16)
        for h in range(NSA_KV_HEADS):
            v_ref[0, h, :HEAD_DIM, :] = vT[h * HEAD_DIM:(h + 1) * HEAD_DIM, :]
            v_ref[0, h, HEAD_DIM:, :] = ones_rows


def _nsa_compress_body(x_ref, wd_ref, pos_ref, w2k_ref, w2vT_ref, kn_ref, kc_ref, vcT_ref):
    branch = pl.program_id(1)
    nr = x_ref.shape[1] // CMP_STRIDE
    hid2 = NSA_KV_HEADS * CMP_HIDDEN
    ya = jnp.zeros((nr, hid2), F32)
    yb = jnp.zeros((nr, hid2), F32)
    for l in range(0, CMP_STRIDE, 2):
        xl = jnp.concatenate([x_ref[0, pl.ds(l + d, nr, stride=CMP_STRIDE), :].astype(BF16) for d in (0, 1)], axis=1)
        ya = ya + jnp.dot(xl, wd_ref[0, l * LANES:(l + 2) * LANES, :], preferred_element_type=F32)
        yb = yb + jnp.dot(xl, wd_ref[0, (CMP_STRIDE + l) * LANES:(CMP_STRIDE + l + 2) * LANES, :],
                          preferred_element_type=F32)
    bias = jnp.dot(pos_ref[0], wd_ref[0], preferred_element_type=F32)[0:1, :]
    hidden = _silu(ya + pltpu.roll(yb, nr - 1, 0) + bias).astype(BF16)

    @pl.when(branch == 0)
    def _():
        for h in range(NSA_KV_HEADS):
            kc = jnp.dot(hidden[:, h * CMP_HIDDEN:(h + 1) * CMP_HIDDEN], w2k_ref[...], preferred_element_type=F32)
            kc_ref[0, h] = _rms(kc, kn_ref[...]).astype(BF16)

    @pl.when(branch == 1)
    def _():
        for h in range(NSA_KV_HEADS):
            vcT_ref[0, h] = lax.dot_general(w2vT_ref[...], hidden[:, h * CMP_HIDDEN:(h + 1) * CMP_HIDDEN],
                                            (((1,), (1,)), ((), ())), preferred_element_type=F32).astype(BF16)


def _nsa_compress(kvb, cmp_pos, cmp_w1, cmp_w2, k_norm0):
    B, T, _ = kvb.shape
    nr = T // CMP_STRIDE
    w1 = cmp_w1.reshape(2, CMP_LEN, HEAD_DIM, CMP_HIDDEN)
    z = jnp.zeros_like(w1)
    wd = jnp.concatenate([jnp.concatenate([w1, z], axis=3), jnp.concatenate([z, w1], axis=3)], axis=2)
    wd = wd.reshape(2, CMP_LEN * LANES, NSA_KV_HEADS * CMP_HIDDEN).astype(BF16)
    pos = jnp.tile(cmp_pos, (1, 1, NSA_KV_HEADS)).reshape(2, 1, CMP_LEN * LANES)
    pos = jnp.broadcast_to(pos, (2, SUBLANES, CMP_LEN * LANES)).astype(BF16)
    return pl.pallas_call(
        _nsa_compress_body,
        grid=(B, 2),
        in_specs=[pl.BlockSpec((1, T, LANES), lambda b, r: (b, 0, r)),
                  pl.BlockSpec((1,) + wd.shape[1:], lambda b, r: (r, 0, 0)),
                  pl.BlockSpec((1,) + pos.shape[1:], lambda b, r: (r, 0, 0)),
                  _const_spec((CMP_HIDDEN, HEAD_DIM)), _const_spec((HEAD_DIM, CMP_HIDDEN)),
                  _const_spec((1, HEAD_DIM))],
        out_specs=[pl.BlockSpec((1, NSA_KV_HEADS, nr, HEAD_DIM), lambda b, r: (b, 0, 0, 0)),
                   pl.BlockSpec((1, NSA_KV_HEADS, HEAD_DIM, nr), lambda b, r: (b, 0, 0, 0))],
        out_shape=[jax.ShapeDtypeStruct((B, NSA_KV_HEADS, nr, HEAD_DIM), BF16),
                   jax.ShapeDtypeStruct((B, NSA_KV_HEADS, HEAD_DIM, nr), BF16)],
        compiler_params=_params(("parallel", "arbitrary")),
        name="nsa_compress",
    )(kvb, wd, pos, cmp_w2[0].astype(BF16), cmp_w2[1].T.astype(BF16), k_norm0.reshape(1, HEAD_DIM))


def _nsa_attn_body(qT_ref, kc_ref, vcT_ref, ks_ref, vs_ref, kw_ref, vw_ref, gate_ref, selm_ref,
                   o_ref, selb_ref, gT_ref, qaug_ref, m_ref, acc_ref, sa_ref, sb_ref, oc_ref, ow_ref, *, seq):
    G, Dh, Qb = NSA_GROUP, HEAD_DIM, Q_BLOCK
    h = pl.program_id(1)
    t0 = pl.program_id(2) * Qb
    nr = seq // CMP_STRIDE
    ns = seq // SEL_BLOCK
    pos = t0 + lax.broadcasted_iota(jnp.int32, (1, Qb), 1)

    qT = qT_ref[0]
    q3 = jnp.concatenate([qT[g * Dh:(g + 1) * Dh, :] for g in range(G)], axis=1)
    qaug_ref[...] = jnp.concatenate([q3, jnp.zeros_like(q3)], axis=0)
    row = lax.broadcasted_iota(jnp.int32, (2 * Dh, G * Qb), 0)
    qpad = jnp.where((row < Dh) == (h == 0), jnp.concatenate([q3, q3], axis=0), 0).astype(BF16)

    def window_branch():
        ws = pl.multiple_of(jnp.maximum(t0 - WINDOW, 0), Qb)
        sw = jnp.dot(kw_ref[0, pl.ds(ws, WIN_KEYS), :], qpad, preferred_element_type=F32)
        vwx = vw_ref[0, 0, :, pl.ds(ws, WIN_KEYS)]
        kpos = ws + lax.broadcasted_iota(jnp.int32, (WIN_KEYS, 1), 0)
        bias_w = jnp.where((kpos <= pos) & (kpos > pos - WINDOW), 0.0, NEG)
        for g in range(G):
            s = sw[:, g * Qb:(g + 1) * Qb] + bias_w
            e = jnp.exp2(s - jnp.max(s, axis=0, keepdims=True)).astype(BF16)
            ow_ref[g] = jnp.dot(vwx, e, preferred_element_type=F32)

    def compressed_and_select(n_c):
        n_s = n_c * CMP_STRIDE // SEL_BLOCK
        sc = jnp.dot(kc_ref[0, 0, :n_c, :], q3, preferred_element_type=F32)
        cmp_end = lax.broadcasted_iota(jnp.int32, (n_c, 1), 0) * CMP_STRIDE + (CMP_LEN - 1)
        bias_c = jnp.where(cmp_end <= pos, 0.0, NEG)
        has_c = pos >= CMP_LEN - 1
        lhs = jnp.concatenate([vcT_ref[0, 0, :, :n_c], selm_ref[:n_s, :n_c]], axis=0)
        imp = jnp.zeros((n_s, Qb), F32)
        for g in range(G):
            s = sc[:, g * Qb:(g + 1) * Qb] + bias_c
            e = jnp.exp2(s - jnp.max(s, axis=0, keepdims=True))
            p = (e * jnp.where(has_c, 1.0 / jnp.sum(e, axis=0, keepdims=True), 0.0)).astype(BF16)
            both = jnp.dot(lhs, p, preferred_element_type=F32)
            oc_ref[g] = both[:Dh]
            imp = imp + both[Dh:]

        blk = lax.broadcasted_iota(jnp.int32, (n_s, Qb), 0)
        cur = pos // SEL_BLOCK
        forced = (blk == 0) | (blk == cur) | (blk == cur - 1)
        score = jnp.where(forced | (blk > cur), -jnp.inf, imp)
        for _ in range(SEL_TOPK - 3):
            best = jnp.max(score, axis=0, keepdims=True)
            first = jnp.min(jnp.where(score == best, blk, n_s), axis=0, keepdims=True)
            score = jnp.where(blk == first, -jnp.inf, score)
        selb_ref[:n_s, :] = jnp.where((score == -jnp.inf) & (blk <= cur), 0.0, NEG)

    chunk = min(CMP_CHUNK, nr)
    need = (t0 + Qb) // CMP_STRIDE
    for k in range(1, nr // chunk + 1):
        pl.when((need + chunk - 1) // chunk == k)(functools.partial(compressed_and_select, k * chunk))

    bpt = KEY_TILE // SEL_BLOCK
    krow = lax.broadcasted_iota(jnp.int32, (KEY_TILE, 1), 0)
    m_ref[...] = jnp.full(m_ref.shape, NEG, F32)
    acc_ref[...] = jnp.zeros(acc_ref.shape, F32)

    def scores(kt, slot_ref, nk=KEY_TILE):
        base = pl.multiple_of(kt * KEY_TILE, KEY_TILE)
        sb = selb_ref[pl.ds(pl.multiple_of(kt * bpt, bpt), bpt), :]
        qaug_ref[Dh:Dh + bpt, :] = jnp.concatenate([sb] * G, axis=1).astype(BF16)
        slot_ref[:nk, :] = jnp.dot(ks_ref[0, 0, pl.ds(base, nk), :], qaug_ref[...],
                                   preferred_element_type=F32)

    def consume(kt, slot_ref, diagonal, nk=KEY_TILE):
        base = pl.multiple_of(kt * KEY_TILE, KEY_TILE)
        vx = vs_ref[0, 0, :, pl.ds(base, nk)]
        for g in range(G):
            s = slot_ref[:nk, g * Qb:(g + 1) * Qb]
            if diagonal:
                s = jnp.where(base + krow[:nk] <= pos, s, NEG)
            m_old = m_ref[g:g + 1, :]
            m_new = jnp.maximum(m_old, jnp.max(s, axis=0, keepdims=True))
            e = jnp.exp2(s - m_new).astype(BF16)
            acc_ref[g] = jnp.exp2(m_old - m_new) * acc_ref[g] + jnp.dot(vx, e, preferred_element_type=F32)
            m_ref[g:g + 1, :] = m_new

    n_off = (t0 + Qb + KEY_TILE - 1) // KEY_TILE - 1
    scores(0, sa_ref)
    window_branch()

    def pair(p, carry):
        scores(2 * p + 1, sb_ref)
        consume(2 * p, sa_ref, False)
        scores(2 * p + 2, sa_ref)
        consume(2 * p + 1, sb_ref, False)
        return carry

    lax.fori_loop(0, n_off // 2, pair, 0)

    def last_odd(nk):
        scores(n_off, sb_ref, nk)
        consume(n_off - 1, sa_ref, False)
        consume(n_off, sb_ref, True, nk)

    def last_even(nk):
        consume(n_off, sa_ref, True, nk)

    q_in_tile = (t0 % KEY_TILE) // Qb
    for r in range(KEY_TILE // Qb):
        pl.when((n_off % 2 == 1) & (q_in_tile == r))(functools.partial(last_odd, (r + 1) * Qb))
        pl.when((n_off % 2 == 0) & (q_in_tile == r))(functools.partial(last_even, (r + 1) * Qb))

    gT_ref[...] = _sigmoid(gate_ref[0]).T
    gT = gT_ref[0:3 * SUBLANES, :]
    for g in range(G):
        r0 = g * 3
        gates = jnp.where(h == 0, gT[r0:r0 + 3], gT[G * 3 + r0:G * 3 + r0 + 3])
        a_s, a_w = acc_ref[g], ow_ref[g]
        o_s = a_s[:Dh] * (1.0 / a_s[Dh:Dh + 1])
        o_w = a_w[:Dh] * (1.0 / a_w[Dh:Dh + 1])
        o_ref[0, g * Dh:(g + 1) * Dh, :] = gates[0:1] * oc_ref[g] + gates[1:2] * o_s + gates[2:3] * o_w


def _nsa_attn(qT, kc, vcT, ks, vs, kw, vw, gate):
    B, _, T = qT.shape
    assert T % KEY_TILE == 0 and T >= WIN_KEYS
    nr, ns = T // CMP_STRIDE, T // SEL_BLOCK
    G, Dh, Qb = NSA_GROUP, HEAD_DIM, Q_BLOCK
    j = jnp.arange(ns)[:, None]
    c = jnp.arange(nr)[None, :]
    ratio = SEL_BLOCK // CMP_STRIDE
    selm = ((c >= ratio * j - 1) & (c <= ratio * j + ratio - 1) & (c < nr - 1)).astype(BF16)
    vfull = pl.BlockSpec((1, 1, V_ROWS, T), lambda b, h, i: (b, h, 0, 0))
    return pl.pallas_call(
        functools.partial(_nsa_attn_body, seq=T),
        grid=(B, NSA_KV_HEADS, T // Qb),
        in_specs=[pl.BlockSpec((1, G * Dh, Qb), lambda b, h, i: (b, h, i)),
                  pl.BlockSpec((1, 1, nr, Dh), lambda b, h, i: (b, h, 0, 0)),
                  pl.BlockSpec((1, 1, Dh, nr), lambda b, h, i: (b, h, 0, 0)),
                  pl.BlockSpec((1, 1, T, LANES), lambda b, h, i: (b, h, 0, 0)), vfull,
                  pl.BlockSpec((1, T, LANES), lambda b, h, i: (b, 0, 0)), vfull,
                  pl.BlockSpec((1, Qb, LANES), lambda b, h, i: (b, i, 0)),
                  _const_spec(selm.shape)],
        out_specs=pl.BlockSpec((1, G * Dh, Qb), lambda b, h, i: (b, h, i)),
        out_shape=jax.ShapeDtypeStruct((B, NSA_WIDTH, T), F32),
        scratch_shapes=[pltpu.VMEM((ns, Qb), F32), pltpu.VMEM((LANES, Qb), F32),
                        pltpu.VMEM((2 * Dh, G * Qb), BF16), pltpu.VMEM((SUBLANES, Qb), F32),
                        pltpu.VMEM((G, V_ROWS, Qb), F32),
                        pltpu.VMEM((KEY_TILE, G * Qb), F32), pltpu.VMEM((KEY_TILE, G * Qb), F32),
                        pltpu.VMEM((G, Dh, Qb), F32), pltpu.VMEM((G, V_ROWS, Qb), F32)],
        compiler_params=_params(("parallel", "parallel", "arbitrary")),
        name="nsa_attn",
    )(qT, kc, vcT, ks, vs, kw, vw, gate, selm)


POOL_HALO = 16
FF_TILE = 1024


def _pool_tile(u, halo, first_pos, w, scale):
    tm = u.shape[0]
    ue = jnp.concatenate([halo, u], axis=0)
    grp = lax.broadcasted_iota(jnp.int32, (1, POOL_WIDTH), 1) // POOL_GROUP_DIM
    acc = ue
    wsum = jnp.zeros_like(ue)
    width = jnp.zeros((1, POOL_WIDTH), F32)
    for gi, win in enumerate(POOL_WINDOWS):
        acc = acc + pltpu.roll(acc, win // 2, 0)
        wsum = jnp.where(grp == gi, acc, wsum)
        width = jnp.where(grp == gi, float(win), width)
    t1 = (first_pos + 1 + lax.broadcasted_iota(jnp.int32, (tm, 1), 0)).astype(F32)
    mean = wsum[POOL_HALO:] / jnp.minimum(t1, width)
    return jnp.dot((mean - u).astype(BF16), w, preferred_element_type=F32) * scale


def _post_body(x_ref, ya_ref, ybT_ref, u_ref, halo_ref, wo_ref, pw_ref, psc_ref, g_ref, w1_ref, w2_ref, o_ref):
    i = pl.program_id(1)
    tm = x_ref.shape[1]
    halo = jnp.where(i > 0, halo_ref[0], 0.0)
    yc = _pool_tile(u_ref[0], halo, i * tm, pw_ref[...], psc_ref[...])
    y = jnp.concatenate([ya_ref[0].astype(BF16), ybT_ref[0].T.astype(BF16), yc.astype(BF16)], axis=1)
    x = x_ref[0] + jnp.dot(y, wo_ref[...], preferred_element_type=F32)
    o_ref[0] = x
    x = o_ref[0]
    h = _rms(x, g_ref[...]).astype(BF16)
    acc = x
    for j in range(D_FF // FF_TILE):
        cols = slice(j * FF_TILE, (j + 1) * FF_TILE)
        a = jnp.maximum(jnp.dot(h, w1_ref[:, cols], preferred_element_type=F32), 0.0)
        acc = acc + jnp.dot((a * a).astype(BF16), w2_ref[cols, :], preferred_element_type=F32)
    o_ref[0] = acc


def _post(x, ya, ybT, u, pool_w, pool_scale, w_out, g, w1, w2, tm=1024):
    B, T, _ = x.shape
    wblk =jax.scipy.linalg.block_diag(*[pool_w[gi] for gi in range(len(POOL_WINDOWS))]).astype(BF16)
    tok = lambda c: pl.BlockSpec((1, tm, c), lambda b, i: (b, i, 0))
    hb = tm // POOL_HALO
    return pl.pallas_call(
        _post_body,
        grid=(B, T // tm),
        in_specs=[tok(D_MODEL), tok(GDN_WIDTH), pl.BlockSpec((1, NSA_WIDTH, tm), lambda b, i: (b, 0, i)),
                  tok(POOL_WIDTH),
                  pl.BlockSpec((1, POOL_HALO, POOL_WIDTH), lambda b, i: (b, jnp.maximum(i * hb - 1, 0), 0)),
                  _const_spec(w_out.shape), _const_spec(wblk.shape), _const_spec((1, POOL_WIDTH)), _const_spec((1, D_MODEL)),
                  _resident_spec(w1.shape), _resident_spec(w2.shape)],
        out_specs=tok(D_MODEL),
        out_shape=jax.ShapeDtypeStruct(x.shape, F32),
        compiler_params=_params(("parallel", "parallel")),
        name="post",
    )(x, ya, ybT, u, u, w_out.astype(BF16), wblk, pool_scale.reshape(1, POOL_WIDTH), g.reshape(1, D_MODEL),
      w1.astype(BF16), w2.astype(BF16))


def _pad_in_weights(w):
    H = GDN_HEADS
    o_ba = 4 * GDN_WIDTH
    o_q = o_ba + 2 * H
    o_gate = o_q + NSA_WIDTH + 6 * LANES
    o_u = o_gate + 3 * NSA_HEADS
    w = w.astype(BF16)
    pad = lambda a: jnp.pad(a, [(0, 0)] * (a.ndim - 1) + [(0, LANES - a.shape[-1])])
    return jnp.concatenate([w[..., :o_ba], pad(w[..., o_ba:o_q]), w[..., o_q:o_gate], pad(w[..., o_gate:o_u]),
                            w[..., o_u:]], axis=-1)


def kernel(x, norm_mix, w_in, conv_w, a_log, dt_bias, gdn_norm, nsa_q_norm, nsa_k_norm, cmp_pos, cmp_w1, cmp_w2,
           pool_w, pool_scale, w_out, norm_ffn, w_ffn1, w_ffn2):
    w_in_pad = _pad_in_weights(w_in)
    for l in range(w_in.shape[0]):
        qkv, z, ba, kvc, gate, u, qT, ks, kw, vs, vw = _inproj(x, norm_mix[l], w_in_pad[l], nsa_q_norm[l],
                                                               nsa_k_norm[l])
        ya = _gdn(qkv, ba, z, conv_w[l], a_log[l], dt_bias[l], gdn_norm[l])
        kc, vcT = _nsa_compress(kvc, cmp_pos[l], cmp_w1[l], cmp_w2[l], nsa_k_norm[l][0])
        ybT = _nsa_attn(qT, kc, vcT, ks, vs, kw, vw, gate)
        x = _post(x, ya, ybT, u, pool_w[l], pool_scale[l], w_out[l], norm_ffn[l], w_ffn1[l], w_ffn2[l])
    return x
```
